```python
import jax
import jax.numpy as jnp
from jax import lax
import numpy as np

D_MODEL = 2048
BATCH = 1
SEQ = 16384
DEPTH = 4

N_BRANCH = 4
MIX_W = D_MODEL // N_BRANCH
CONV_K = 4
LRU_BLOCKS = 8
LRU_BW = MIX_W // LRU_BLOCKS
LRU_C = 8.0
SSD_HEADS = 8
SSD_P = MIX_W // SSD_HEADS
SSD_G = 2
SSD_HPG = SSD_HEADS // SSD_G
SSD_N = 128
SSD_CHUNK = 128
MLSTM_HEADS = 4
MLSTM_DH = MIX_W // MLSTM_HEADS
MLSTM_CHUNK = 128
HGRN_HEADS = 4
HGRN_DK = MIX_W // HGRN_HEADS
HGRN_DV = MIX_W // HGRN_HEADS
HGRN_CHUNK = 64
N_GROUPS = 4
EXP_PER_GROUP = 8
N_EXPERTS = N_GROUPS * EXP_PER_GROUP
TOP_K = 2
D_FF = D_MODEL // 4
MOE_BLOCK = 128
ALPHA = (2.0 * DEPTH) ** 0.25
BETA = (8.0 * DEPTH) ** -0.25
EPS = 1e-5
NEG_BIG = -1e30

IN_SIZES = (MIX_W, MIX_W, MIX_W, MIX_W, SSD_G * SSD_N, SSD_G * SSD_N, SSD_HEADS, MIX_W, MIX_W, MIX_W, MLSTM_HEADS, MLSTM_HEADS, MIX_W, MIX_W, MIX_W, MIX_W, MIX_W)
IN_COLS = sum(IN_SIZES)

kernel_name = 'hybrid_rglru_ssd_mlstm_hgrn2_hmoe_trunk'


def _layer_norm(x, w, b):
    xf = x.astype(jnp.float32)
    mu = xf.mean(-1, keepdims=True)
    var = jnp.square(xf - mu).mean(-1, keepdims=True)
    return ((xf - mu) * lax.rsqrt(var + EPS) * w + b).astype(x.dtype)


def _rms_norm(x, w):
    xf = x.astype(jnp.float32)
    return xf * lax.rsqrt(jnp.mean(xf * xf, -1, keepdims=True) + EPS) * w


def _causal_conv(x, w, b):
    y = lax.conv_general_dilated(x, w[:, None, :].astype(x.dtype), window_strides=(1,), padding=[(CONV_K - 1, 0)], dimension_numbers=('NWC', 'WIO', 'NWC'), feature_group_count=x.shape[-1])
    return y + b


def _to_chunks(t, size):
    b, s = t.shape[:2]
    return jnp.moveaxis(t.reshape((b, s // size, size) + t.shape[2:]), 1, 0)


def _from_chunks(t):
    nc, b, size = t.shape[:3]
    return jnp.moveaxis(t, 0, 1).reshape((b, nc * size) + t.shape[3:])


def _seg_decay(cs):
    size = cs.shape[1]
    mask = np.tril(np.ones((size, size), dtype=bool)).reshape((1, size, size) + (1,) * (cs.ndim - 2))
    return jnp.exp(jnp.where(mask, cs[:, :, None] - cs[:, None], NEG_BIG))


def _linear_recurrence_op(e1, e2):
    a1, b1 = e1
    a2, b2 = e2
    return a1 * a2, a2 * b1 + b2


def _rglru_branch(a_x, a_g, conv_w, conv_b, wa, ba, wx, bx, lam):
    b, s = a_x.shape[:2]
    xc = _causal_conv(a_x, conv_w, conv_b).astype(jnp.float32)
    xb = xc.reshape(b, s, LRU_BLOCKS, LRU_BW)
    r = jax.nn.sigmoid(jnp.einsum('bsni,nij->bsnj', xb, wa.astype(jnp.float32)).reshape(b, s, MIX_W) + ba)
    i = jax.nn.sigmoid(jnp.einsum('bsni,nij->bsnj', xb, wx.astype(jnp.float32)).reshape(b, s, MIX_W) + bx)
    log_a = -LRU_C * r * jax.nn.softplus(-lam.astype(jnp.float32))
    u = jnp.sqrt(-jnp.expm1(2.0 * log_a)) * (i * xc)
    _, hs = lax.associative_scan(_linear_recurrence_op, (jnp.exp(log_a), u), axis=1)
    return hs * jax.nn.gelu(a_g.astype(jnp.float32))


def _ssd_scan(xdt, adt, bm, cm):
    b = xdt.shape[0]

    def step(state, inp):
        xc, ac, bc, cc = inp
        cs = jnp.cumsum(ac, axis=1)
        scores = jnp.einsum('blgn,bsgn->blsg', cc, bc)[..., None] * _seg_decay(cs)
        y = jnp.einsum('blsgh,bsghp->blghp', scores, xc)
        y = y + jnp.einsum('blgn,bghpn->blghp', cc, state) * jnp.exp(cs)[..., None]
        w_end = jnp.exp(cs[:, -1:] - cs)
        state = jnp.exp(cs[:, -1])[..., None, None] * state + jnp.einsum('bsgn,bsghp->bghpn', bc, xc * w_end[..., None])
        return state, y

    state0 = jnp.zeros((b, SSD_G, SSD_HPG, SSD_P, SSD_N), jnp.float32)
    _, y = lax.scan(step, state0, tuple(_to_chunks(t, SSD_CHUNK) for t in (xdt, adt, bm, cm)))
    return _from_chunks(y)


def _ssd_branch(b_z, b_x, b_b, b_c, b_dt, conv_w, conv_b, dt_bias, a_log, d_skip, norm_w):
    b, s = b_x.shape[:2]
    xbc = jax.nn.silu(_causal_conv(jnp.concatenate([b_x, b_b, b_c], axis=-1), conv_w, conv_b).astype(jnp.float32))
    xv = xbc[..., :MIX_W].reshape(b, s, SSD_G, SSD_HPG, SSD_P)
    bm = xbc[..., MIX_W:MIX_W + SSD_G * SSD_N].reshape(b, s, SSD_G, SSD_N)
    cm = xbc[..., MIX_W + SSD_G * SSD_N:].reshape(b, s, SSD_G, SSD_N)
    dt = jax.nn.softplus(b_dt.astype(jnp.float32) + dt_bias).reshape(b, s, SSD_G, SSD_HPG)
    a = -jnp.exp(a_log.astype(jnp.float32)).reshape(SSD_G, SSD_HPG)
    y = _ssd_scan(xv * dt[..., None], dt * a, bm, cm)
    y = y + d_skip.astype(jnp.float32).reshape(SSD_G, SSD_HPG, 1) * xv
    return _rms_norm(y.reshape(b, s, MIX_W) * jax.nn.silu(b_z.astype(jnp.float32)), norm_w)


def _mlstm_scan(q, k, v, ig, lf):
    b = q.shape[0]
    size = MLSTM_CHUNK
    mask = np.tril(np.ones((size, size), dtype=bool))[None, :, :, None]

    def step(carry, inp):
        cmem, nmem, m = carry
        qc, kc, vc, ic, fc = inp
        cs = jnp.cumsum(fc, axis=1)
        dmat = jnp.where(mask, cs[:, :, None] - cs[:, None] + ic[:, None], NEG_BIG)
        inter = cs + m[:, None]
        m_row = jnp.maximum(dmat.max(2), inter)
        w = jnp.exp(dmat - m_row[:, :, None]) * jnp.einsum('blhd,bshd->blsh', qc, kc)
        w_inter = jnp.exp(inter - m_row)
        num = jnp.einsum('blsh,bshv->blhv', w, vc) + w_inter[..., None] * jnp.einsum('blhd,bhdv->blhv', qc, cmem)
        den = w.sum(2) + w_inter * jnp.einsum('blhd,bhd->blh', qc, nmem)
        h = num / jnp.maximum(jnp.abs(den), jnp.exp(-m_row))[..., None]
        g_tot = cs[:, -1]
        s_end = g_tot[:, None] - cs + ic
        m_new = jnp.maximum(g_tot + m, s_end.max(1))
        w_end = jnp.exp(s_end - m_new[:, None])[..., None]
        decay = jnp.exp(g_tot + m - m_new)
        cmem = decay[..., None, None] * cmem + jnp.einsum('bshd,bshv->bhdv', kc * w_end, vc)
        nmem = decay[..., None] * nmem + jnp.einsum('bshd->bhd', kc * w_end)
        return (cmem, nmem, m_new), h

    init = (jnp.zeros((b, MLSTM_HEADS, MLSTM_DH, MLSTM_DH), jnp.float32), jnp.zeros((b, MLSTM_HEADS, MLSTM_DH), jnp.float32), jnp.zeros((b, MLSTM_HEADS), jnp.float32))
    _, hs = lax.scan(step, init, tuple(_to_chunks(t, size) for t in (q, k, v, ig, lf)))
    return _from_chunks(hs)


def _mlstm_branch(c_q, c_k, c_v, c_i, c_f, c_o, conv_w, conv_b, i_bias, f_bias, norm_w):
    b, s = c_q.shape[:2]
    qk = jax.nn.silu(_causal_conv(jnp.concatenate([c_q, c_k], axis=-1), conv_w, conv_b).astype(jnp.float32))
    q = qk[..., :MIX_W].reshape(b, s, MLSTM_HEADS, MLSTM_DH) * MLSTM_DH ** -0.5
    k = qk[..., MIX_W:].reshape(b, s, MLSTM_HEADS, MLSTM_DH)
    v = c_v.astype(jnp.float32).reshape(b, s, MLSTM_HEADS, MLSTM_DH)
    ig = c_i.astype(jnp.float32) + i_bias
    lf = jax.nn.log_sigmoid(c_f.astype(jnp.float32) + f_bias)
    hs = _mlstm_scan(q, k, v, ig, lf)
    hs = _rms_norm(hs, norm_w.reshape(MLSTM_HEADS, MLSTM_DH)).reshape(b, s, MIX_W)
    return jax.nn.sigmoid(c_o.astype(jnp.float32)) * hs


def _hgrn2_scan(q, k, v, log_f):
    b = q.shape[0]

    def step(state, inp):
        qc, kc, vc, gc = inp
        cs = jnp.cumsum(gc, axis=1)
        scores = jnp.einsum('blshk,bshk->blsh', qc[:, :, None] * _seg_decay(cs), kc)
        y = jnp.einsum('blsh,bshv->blhv', scores, vc) + jnp.einsum('blhk,bhkv->blhv', qc * jnp.exp(cs), state)
        state = jnp.exp(cs[:, -1])[..., None] * state + jnp.einsum('bshk,bshv->bhkv', kc * jnp.exp(cs[:, -1:] - cs), vc)
        return state, y

    state0 = jnp.zeros((b, HGRN_HEADS, HGRN_DK, HGRN_DV), jnp.float32)
    _, y = lax.scan(step, state0, tuple(_to_chunks(t, HGRN_CHUNK) for t in (q, k, v, log_f)))
    return _from_chunks(y)


def _hgrn2_branch(d_q, d_f, d_i, d_g, lb, norm_w):
    b, s = d_q.shape[:2]
    lb = lb.reshape(HGRN_HEADS, HGRN_DK)
    q = jax.nn.silu(d_q.astype(jnp.float32)).reshape(b, s, HGRN_HEADS, HGRN_DK)
    u = d_f.astype(jnp.float32).reshape(b, s, HGRN_HEADS, HGRN_DK)
    f = lb + (1.0 - lb) * jax.nn.sigmoid(u)
    log_f = jnp.log(jnp.maximum(f, 1e-30))
    k = (1.0 - lb) * jax.nn.sigmoid(-u)
    v = d_i.astype(jnp.float32).reshape(b, s, HGRN_HEADS, HGRN_DV)
    o = _hgrn2_scan(q, k, v, log_f)
    o = _rms_norm(o, norm_w.reshape(HGRN_HEADS, HGRN_DV)).reshape(b, s, MIX_W)
    return o * jax.nn.silu(d_g.astype(jnp.float32))


def _token_mixer(h, lb, w_in, lru_conv_w, lru_conv_b, lru_wa, lru_ba, lru_wx, lru_bx, lru_lambda, ssd_conv_w, ssd_conv_b, ssd_dt_bias, ssd_a_log, ssd_d, ssd_norm_w, mlstm_conv_w, mlstm_conv_b, mlstm_i_bias, mlstm_f_bias, mlstm_norm_w, hgrn_norm_w, w_up, w_gate, b_gate, w_out):
    proj = h @ w_in
    (a_x, a_g, b_z, b_x, b_b, b_c, b_dt, c_q, c_k, c_v, c_i, c_f, c_o, d_q, d_f, d_i, d_g) = jnp.split(proj, list(np.cumsum(IN_SIZES)[:-1]), axis=-1)
    branches = (
        _rglru_branch(a_x, a_g, lru_conv_w, lru_conv_b, lru_wa, lru_ba, lru_wx, lru_bx, lru_lambda),
        _ssd_branch(b_z, b_x, b_b, b_c, b_dt, ssd_conv_w, ssd_conv_b, ssd_dt_bias, ssd_a_log, ssd_d, ssd_norm_w),
        _mlstm_branch(c_q, c_k, c_v, c_i, c_f, c_o, mlstm_conv_w, mlstm_conv_b, mlstm_i_bias, mlstm_f_bias, mlstm_norm_w),
        _hgrn2_branch(d_q, d_f, d_i, d_g, lb, hgrn_norm_w),
    )
    merged = jnp.zeros_like(h)
    for br in range(N_BRANCH):
        up = branches[br].astype(h.dtype) @ w_up[br]
        gate = jax.nn.sigmoid(h @ w_gate[br] + b_gate[br])
        merged = merged + gate * up
    return merged @ w_out


def _hier_moe(h, wg_r, bg_r, we_r, be_r, w_g, w_u, w_d):
    b, s, d = h.shape
    t = b * s
    xt = h.reshape(t, d)
    g_prob = jax.nn.softmax((xt @ wg_r + bg_r).astype(jnp.float32), axis=-1)
    p_grp, grp = lax.top_k(g_prob, 1)
    e_logits = (xt @ we_r + be_r).astype(jnp.float32).reshape(t, N_GROUPS, EXP_PER_GROUP)
    e_in = e_logits[jnp.arange(t), grp[:, 0]]
    top_v, top_i = lax.top_k(e_in, TOP_K)
    wts = jax.nn.softmax(top_v, axis=-1) * p_grp
    eid = (grp * EXP_PER_GROUP + top_i).reshape(-1).astype(jnp.int32)
    tok = jnp.repeat(jnp.arange(t, dtype=jnp.int32), TOP_K)
    wflat = wts.reshape(-1)
    m = t * TOP_K
    order = jnp.argsort(eid)
    eid_s, tok_s, w_s = eid[order], tok[order], wflat[order]
    counts = jnp.zeros((N_EXPERTS,), jnp.int32).at[eid].add(1)
    starts = jnp.cumsum(counts) - counts
    pcounts = (counts + MOE_BLOCK - 1) // MOE_BLOCK * MOE_BLOCK
    pends = jnp.cumsum(pcounts)
    dest = (pends - pcounts)[eid_s] + jnp.arange(m, dtype=jnp.int32) - starts[eid_s]
    n_blocks = -(-m // MOE_BLOCK) + N_EXPERTS
    slot_tok = jnp.full((n_blocks * MOE_BLOCK,), t, jnp.int32).at[dest].set(tok_s)
    slot_w = jnp.zeros((n_blocks * MOE_BLOCK,), jnp.float32).at[dest].set(w_s)
    blk_e = jnp.minimum(jnp.searchsorted(pends, jnp.arange(n_blocks, dtype=jnp.int32) * MOE_BLOCK, side='right'), N_EXPERTS - 1)
    x_pad = jnp.concatenate([xt, jnp.zeros((1, d), xt.dtype)], axis=0)

    def block(acc, inp):
        toks, bw, e = inp
        rows = x_pad[toks]
        hid = jax.nn.silu(rows @ w_g[e]) * (rows @ w_u[e])
        y = (hid @ w_d[e]) * bw[:, None].astype(rows.dtype)
        return acc.at[toks].add(y), None

    acc, _ = lax.scan(block, jnp.zeros((t + 1, d), xt.dtype), (slot_tok.reshape(n_blocks, MOE_BLOCK), slot_w.reshape(n_blocks, MOE_BLOCK), blk_e))
    return acc[:t].reshape(b, s, d)


def setup_inputs(seed: int = 0) -> dict:
    key = jax.random.key(seed)
    ks = iter(jax.random.split(key, 48))
    L, D, F, E = DEPTH, D_MODEL, D_FF, N_EXPERTS

    def nrm(shape, scale):
        return scale * jax.random.normal(next(ks), shape, jnp.float32)

    def unif(shape, lo, hi):
        return jax.random.uniform(next(ks), shape, jnp.float32, lo, hi)

    gate_offset = jnp.tile(jnp.repeat(jnp.array([0.0, 0.0, 1.0], jnp.float32), D), 2)
    s_lru = unif((L, MIX_W), 0.9, 0.999) ** (1.0 / LRU_C)
    dt0 = jnp.exp(unif((L, SSD_HEADS), float(np.log(1e-3)), float(np.log(1e-1))))
    return {
        'x': nrm((BATCH, SEQ, D), 1.0),
        'c': nrm((BATCH, D), 1.0),
        'ada_w': nrm((L, D, 6 * D), 0.2 * D ** -0.5),
        'ada_b': nrm((L, 6 * D), 0.02) + gate_offset,
        'w_in': nrm((L, D, IN_COLS), D ** -0.5),
        'lru_conv_w': nrm((L, CONV_K, MIX_W), CONV_K ** -0.5),
        'lru_conv_b': nrm((L, MIX_W), 0.02),
        'lru_wa': nrm((L, LRU_BLOCKS, LRU_BW, LRU_BW), LRU_BW ** -0.5),
        'lru_ba': nrm((L, MIX_W), 0.02),
        'lru_wx': nrm((L, LRU_BLOCKS, LRU_BW, LRU_BW), LRU_BW ** -0.5),
        'lru_bx': nrm((L, MIX_W), 0.02),
        'lru_lambda': jnp.log(s_lru) - jnp.log1p(-s_lru),
        'ssd_conv_w': nrm((L, CONV_K, MIX_W + 2 * SSD_G * SSD_N), CONV_K ** -0.5),
        'ssd_conv_b': nrm((L, MIX_W + 2 * SSD_G * SSD_N), 0.02),
        'ssd_dt_bias': dt0 + jnp.log(-jnp.expm1(-dt0)),
        'ssd_a_log': jnp.log(unif((L, SSD_HEADS), 1.0, 16.0)),
        'ssd_d': 1.0 + nrm((L, SSD_HEADS), 0.1),
        'ssd_norm_w': 1.0 + nrm((L, MIX_W), 0.02),
        'mlstm_conv_w': nrm((L, CONV_K, 2 * MIX_W), CONV_K ** -0.5),
        'mlstm_conv_b': nrm((L, 2 * MIX_W), 0.02),
        'mlstm_i_bias': nrm((L, MLSTM_HEADS), 0.1),
        'mlstm_f_bias': jnp.linspace(3.0, 6.0, MLSTM_HEADS, dtype=jnp.float32) + nrm((L, MLSTM_HEADS), 0.1),
        'mlstm_norm_w': 1.0 + nrm((L, MIX_W), 0.02),
        'hgrn_lb_param': nrm((L, MIX_W), 1.0),
        'hgrn_norm_w': 1.0 + nrm((L, MIX_W), 0.02),
        'w_up': nrm((L, N_BRANCH, MIX_W, D), MIX_W ** -0.5),
        'w_gate': nrm((L, N_BRANCH, D, D), D ** -0.5),
        'b_gate': nrm((L, N_BRANCH, D), 0.02),
        'w_out': nrm((L, D, D), BETA * D ** -0.5),
        'ln1_w': 1.0 + nrm((L, D), 0.02),
        'ln1_b': nrm((L, D), 0.02),
        'router_group_w': nrm((L, D, N_GROUPS), D ** -0.5),
        'router_group_b': nrm((L, N_GROUPS), 0.01),
        'router_expert_w': nrm((L, D, E), D ** -0.5),
        'router_expert_b': nrm((L, E), 0.01),
        'exp_w_gate': nrm((L, E, D, F), D ** -0.5),
        'exp_w_up': nrm((L, E, D, F), D ** -0.5),
        'exp_w_down': nrm((L, E, F, D), BETA * F ** -0.5),
        'ln2_w': 1.0 + nrm((L, D), 0.02),
        'ln2_b': nrm((L, D), 0.02),
    }


def reference(x, c, ada_w, ada_b, w_in, lru_conv_w, lru_conv_b, lru_wa, lru_ba, lru_wx, lru_bx, lru_lambda, ssd_conv_w, ssd_conv_b, ssd_dt_bias, ssd_a_log, ssd_d, ssd_norm_w, mlstm_conv_w, mlstm_conv_b, mlstm_i_bias, mlstm_f_bias, mlstm_norm_w, hgrn_lb_param, hgrn_norm_w, w_up, w_gate, b_gate, w_out, ln1_w, ln1_b, router_group_w, router_group_b, router_expert_w, router_expert_b, exp_w_gate, exp_w_up, exp_w_down, ln2_w, ln2_b):
    lb_soft = jax.nn.softmax(hgrn_lb_param.astype(jnp.float32), axis=0)
    lb_all = jnp.cumsum(lb_soft, axis=0) - lb_soft[0]
    ada = jnp.einsum('bd,lde->lbe', jax.nn.silu(c), ada_w) + ada_b[:, None]
    for l in range(DEPTH):
        shift1, scale1, gate1, shift2, scale2, gate2 = jnp.split(ada[l][:, None, :], 6, axis=-1)
        h = x * (1.0 + scale1) + shift1
        y = _token_mixer(h, lb_all[l], w_in[l], lru_conv_w[l], lru_conv_b[l], lru_wa[l], lru_ba[l], lru_wx[l], lru_bx[l], lru_lambda[l], ssd_conv_w[l], ssd_conv_b[l], ssd_dt_bias[l], ssd_a_log[l], ssd_d[l], ssd_norm_w[l], mlstm_conv_w[l], mlstm_conv_b[l], mlstm_i_bias[l], mlstm_f_bias[l], mlstm_norm_w[l], hgrn_norm_w[l], w_up[l], w_gate[l], b_gate[l], w_out[l])
        x = _layer_norm(ALPHA * x + gate1 * y, ln1_w[l], ln1_b[l])
        h = x * (1.0 + scale2) + shift2
        y = _hier_moe(h, router_group_w[l], router_group_b[l], router_expert_w[l], router_expert_b[l], exp_w_gate[l], exp_w_up[l], exp_w_down[l])
        x = _layer_norm(ALPHA * x + gate2 * y, ln2_w[l], ln2_b[l])
    return x
```

```python
import functools

import jax
import jax.numpy as jnp
from jax import lax
from jax.experimental import pallas as pl
from jax.experimental.pallas import tpu as pltpu

F32 = jnp.float32
BF16 = jnp.bfloat16
I32 = jnp.int32

D_MODEL = 2048
DEPTH = 4
MIX_W = 512
CONV_K = 4
LRU_BLOCKS = 8
LRU_BW = MIX_W // LRU_BLOCKS
LRU_C = 8.0
SSD_HEADS = 8
SSD_P = 64
SSD_G = 2
SSD_HPG = 4
SSD_N = 128
MLSTM_HEADS = 4
MLSTM_DH = 128
HGRN_HEADS = 4
HGRN_DK = 128
N_GROUPS = 4
EXP_PER_GROUP = 8
N_EXPERTS = 32
D_FF = 512
ALPHA = (2.0 * DEPTH) ** 0.25
EPS = 1e-5
NEG_BIG = -1e30

LANES = 128
SUBLANES = 8
VMEM_LIMIT = 56 * 1024 * 1024

SCAN_CHUNK = 128
LRU_CHUNK = 256
MOE_BM = 256
ROW_TILE = 512
LN_TILE = 256
DISPATCH_TILE = 512

B_COLS = 3 * MIX_W + LANES + MIX_W
C_COLS = 3 * MIX_W + LANES + MIX_W


def _params(sem, vmem=VMEM_LIMIT):
    return pltpu.CompilerParams(dimension_semantics=sem, vmem_limit_bytes=vmem)


def _dot(a, b):
    return jnp.dot(a, b, preferred_element_type=F32)


def _dot_nt(a, b):
    return lax.dot_general(a, b, (((1,), (1,)), ((), ())), preferred_element_type=F32)


def _sigmoid(x):
    return jax.nn.sigmoid(x)


def _silu(x):
    return x * jax.nn.sigmoid(x)


def _softplus(x):
    return jnp.maximum(x, 0.0) + jnp.log1p(jnp.exp(-jnp.abs(x)))


def _gelu_tanh(x):
    return 0.5 * x * (1.0 + jnp.tanh(0.7978845608028654 * (x + 0.044715 * (x * x * x))))


def _row_iota(shape):
    return lax.broadcasted_iota(I32, shape, 0)


def _cumsum_rows(x):
    n = x.shape[0]
    row = _row_iota(x.shape)
    d = 1
    while d < n:
        x = x + jnp.where(row >= d, pltpu.roll(x, d, 0), 0.0)
        d *= 2
    return x


def _shift_rows(x, tail, j):
    xr = pltpu.roll(x, j, 0)
    tr = pltpu.roll(tail, j, 0)
    row = _row_iota(tail.shape)
    top = jnp.where(row < j, tr, xr[0:SUBLANES])
    return jnp.concatenate([top, xr[SUBLANES:]], axis=0)


def _causal_conv(x, tail, w, b):
    acc = x * w[CONV_K - 1:CONV_K] + b
    for j in range(1, CONV_K):
        acc = acc + _shift_rows(x, tail, j) * w[CONV_K - 1 - j:CONV_K - j]
    return acc


def _ada_kernel(c_ref, w_ref, b_ref, o_ref):
    c = c_ref[...]
    o_ref[0] = jnp.dot(_silu(c), w_ref[0], preferred_element_type=F32,
                       precision=lax.Precision.HIGHEST) + b_ref[0]


def _ada_table(c, ada_w, ada_b):
    depth, d, e = ada_w.shape
    tn = 1024
    c8 = jnp.broadcast_to(c.reshape(1, d), (SUBLANES, d))
    out = pl.pallas_call(
        _ada_kernel,
        grid=(depth, e // tn),
        in_specs=[pl.BlockSpec((SUBLANES, d), lambda l, j: (0, 0)),
                  pl.BlockSpec((1, d, tn), lambda l, j: (l, 0, j)),
                  pl.BlockSpec((1, 1, tn), lambda l, j: (l, 0, j))],
        out_specs=pl.BlockSpec((1, SUBLANES, tn), lambda l, j: (l, 0, j)),
        out_shape=jax.ShapeDtypeStruct((depth, SUBLANES, e), F32),
        compiler_params=_params(("parallel", "parallel")),
        name="ada_table",
    )(c8, ada_w, ada_b.reshape(depth, 1, e))
    return out[:, 0, :]


def _lb_kernel(p_ref, o_ref):
    p = p_ref[...]
    depth = p.shape[0]
    rows = [p[l:l + 1] for l in range(depth)]
    m = rows[0]
    for r in rows[1:]:
        m = jnp.maximum(m, r)
    es = [jnp.exp(r - m) for r in rows]
    tot = es[0]
    for e in es[1:]:
        tot = tot + e
    soft = [e / tot for e in es]
    acc = soft[0]
    o_ref[0:1, :] = acc - soft[0]
    for l in range(1, depth):
        acc = acc + soft[l]
        o_ref[l:l + 1, :] = acc - soft[0]


def _lb_table(p):
    return pl.pallas_call(
        _lb_kernel,
        out_shape=jax.ShapeDtypeStruct(p.shape, F32),
        name="hgrn_lb_table",
    )(p)


def _modulate_kernel(x_ref, sc_ref, sh_ref, o_ref):
    o_ref[...] = (x_ref[...] * (1.0 + sc_ref[...]) + sh_ref[...]).astype(o_ref.dtype)


def _modulate(x, scale, shift):
    t, d = x.shape
    tm = ROW_TILE
    return pl.pallas_call(
        _modulate_kernel,
        grid=(t // tm,),
        in_specs=[pl.BlockSpec((tm, d), lambda i: (i, 0)),
                  pl.BlockSpec((1, d), lambda i: (0, 0)),
                  pl.BlockSpec((1, d), lambda i: (0, 0))],
        out_specs=pl.BlockSpec((tm, d), lambda i: (i, 0)),
        out_shape=jax.ShapeDtypeStruct((t, d), BF16),
        compiler_params=_params(("parallel",)),
        name="modulate",
    )(x, scale, shift)


def _proj_kernel(h_ref, w_ref, o_ref):
    o_ref[...] = _dot(h_ref[...], w_ref[...])


def _project(h, w):
    t, d = h.shape
    n = w.shape[1]
    tm = ROW_TILE
    return pl.pallas_call(
        _proj_kernel,
        grid=(t // tm,),
        in_specs=[pl.BlockSpec((tm, d), lambda i: (i, 0)),
                  pl.BlockSpec((d, n), lambda i: (0, 0))],
        out_specs=pl.BlockSpec((tm, n), lambda i: (i, 0)),
        out_shape=jax.ShapeDtypeStruct((t, n), F32),
        compiler_params=_params(("parallel",)),
        name="in_proj",
    )(h, w)


def _rglru_kernel(p_ref, cw_ref, cb_ref, wa_ref, ba_ref, wx_ref, bx_ref, lam_ref, o_ref, tail_ref, h_ref):
    @pl.when(pl.program_id(0) == 0)
    def _():
        tail_ref[...] = jnp.zeros_like(tail_ref)
        h_ref[...] = jnp.zeros_like(h_ref)

    p = p_ref[...]
    n = p.shape[0]
    ax = p[:, :MIX_W]
    ag = p[:, MIX_W:]
    xc = _causal_conv(ax, tail_ref[...], cw_ref[...], cb_ref[...])
    tail_ref[...] = ax[n - SUBLANES:]
    xb = xc.astype(BF16)
    r = _sigmoid(_dot(xb, wa_ref[...]) + ba_ref[...])
    gi = _sigmoid(_dot(xb, wx_ref[...]) + bx_ref[...])
    log_a = (-LRU_C) * r * _softplus(-lam_ref[...])
    a = jnp.exp(log_a)
    u = jnp.sqrt(-jnp.tanh(log_a) * (a * a + 1.0)) * (gi * xc)
    row = _row_iota(a.shape)
    d = 1
    while d < n:
        keep = row >= d
        a_s = pltpu.roll(a, d, 0)
        u_s = pltpu.roll(u, d, 0)
        u = jnp.where(keep, a * u_s + u, u)
        a = jnp.where(keep, a * a_s, a)
        d *= 2
    h = u + a * h_ref[0:1]
    h_ref[...] = jnp.broadcast_to(h[n - 1:n], h_ref.shape)
    o_ref[...] = (h * _gelu_tanh(ag)).astype(o_ref.dtype)


def _rglru(proj, cw, cb, wa, ba, wx, bx, lam):
    t = proj.shape[0]
    n = LRU_CHUNK
    vec = pl.BlockSpec((1, MIX_W), lambda i: (0, 0))
    sq = pl.BlockSpec((MIX_W, MIX_W), lambda i: (0, 0))
    return pl.pallas_call(
        _rglru_kernel,
        grid=(t // n,),
        in_specs=[pl.BlockSpec((n, 2 * MIX_W), lambda i: (i, 0)),
                  pl.BlockSpec((CONV_K, MIX_W), lambda i: (0, 0)), vec, sq, vec, sq, vec, vec],
        out_specs=pl.BlockSpec((n, MIX_W), lambda i: (i, 0)),
        out_shape=jax.ShapeDtypeStruct((t, MIX_W), BF16),
        scratch_shapes=[pltpu.VMEM((SUBLANES, MIX_W), F32), pltpu.VMEM((SUBLANES, MIX_W), F32)],
        compiler_params=_params(("arbitrary",)),
        name="rglru",
    )(proj, cw, cb, wa, ba, wx, bx, lam)


def _ssd_kernel(p_ref, cw_ref, cb_ref, dtbn_ref, dtbx_ref, an_ref, ax_ref, dx_ref, nw_ref, o_ref, tail_ref, st_ref):
    @pl.when(pl.program_id(0) == 0)
    def _():
        tail_ref[...] = jnp.zeros_like(tail_ref)
        st_ref[...] = jnp.zeros_like(st_ref)

    p = p_ref[...]
    n = p.shape[0]
    gw = SSD_HPG * SSD_P
    z = p[:, 0:MIX_W]
    xbc_raw = p[:, MIX_W:3 * MIX_W]
    dtn_raw = p[:, 3 * MIX_W:3 * MIX_W + LANES]
    dtx_raw = p[:, 3 * MIX_W + LANES:]
    xbc = _silu(_causal_conv(xbc_raw, tail_ref[...], cw_ref[...], cb_ref[...]))
    tail_ref[...] = xbc_raw[n - SUBLANES:]
    xv = xbc[:, :MIX_W]
    bm = xbc[:, MIX_W:MIX_W + SSD_G * SSD_N]
    cm = xbc[:, MIX_W + SSD_G * SSD_N:]
    dtx = _softplus(dtx_raw + dtbx_ref[...])
    csx = _cumsum_rows(dtx * ax_ref[...])
    dtn = _softplus(dtn_raw + dtbn_ref[...])
    csn_t = jnp.transpose(_cumsum_rows(dtn * an_ref[...]))
    xdt = xv * dtx
    tril = _row_iota((n, n)) >= lax.broadcasted_iota(I32, (n, n), 1)
    lane_head = lax.shift_right_logical(lax.broadcasted_iota(I32, (n, gw), 1), SSD_P.bit_length() - 1)
    ys = []
    for g in range(SSD_G):
        seg = slice(g * gw, (g + 1) * gw)
        cg = cm[:, g * SSD_N:(g + 1) * SSD_N].astype(BF16)
        bg = bm[:, g * SSD_N:(g + 1) * SSD_N]
        scores = _dot_nt(cg, bg.astype(BF16))
        csg = csx[:, seg]
        xdt_g = xdt[:, seg]
        xdt_gb = xdt_g.astype(BF16)
        st = st_ref[g]
        y = _dot(cg, st.astype(BF16)) * jnp.exp(csg)
        for h in range(SSD_HPG):
            hh = g * SSD_HPG + h
            col = csx[:, hh * SSD_P:hh * SSD_P + 1]
            dec = jnp.exp(jnp.where(tril, col - csn_t[hh:hh + 1, :], NEG_BIG))
            y = y + jnp.where(lane_head == h, _dot((scores * dec).astype(BF16), xdt_gb), 0.0)
        last = csg[n - 1:n]
        xw = (xdt_g * jnp.exp(last - csg)).astype(BF16)
        st_ref[g] = jnp.exp(last) * st + _dot(jnp.transpose(bg).astype(BF16), xw)
        ys.append(y)
    y = jnp.concatenate(ys, axis=1) + dx_ref[...] * xv
    yz = y * _silu(z)
    ms = jnp.mean(yz * yz, axis=1, keepdims=True)
    o_ref[...] = (yz * lax.rsqrt(ms + EPS) * nw_ref[...]).astype(o_ref.dtype)


def _ssd(proj, cw, cb, dtb_n, dtb_x, a_n, a_x, d_x, nw):
    t = proj.shape[0]
    n = SCAN_CHUNK
    cc = 2 * MIX_W
    vec = lambda w: pl.BlockSpec((1, w), lambda i: (0, 0))
    return pl.pallas_call(
        _ssd_kernel,
        grid=(t // n,),
        in_specs=[pl.BlockSpec((n, B_COLS), lambda i: (i, 0)),
                  pl.BlockSpec((CONV_K, cc), lambda i: (0, 0)), vec(cc),
                  vec(LANES), vec(MIX_W), vec(LANES), vec(MIX_W), vec(MIX_W), vec(MIX_W)],
        out_specs=pl.BlockSpec((n, MIX_W), lambda i: (i, 0)),
        out_shape=jax.ShapeDtypeStruct((t, MIX_W), BF16),
        scratch_shapes=[pltpu.VMEM((SUBLANES, cc), F32),
                        pltpu.VMEM((SSD_G, SSD_N, SSD_HPG * SSD_P), F32)],
        compiler_params=_params(("arbitrary",)),
        name="ssd",
    )(proj, cw, cb, dtb_n, dtb_x, a_n, a_x, d_x, nw)


def _mlstm_kernel(p_ref, cw_ref, cb_ref, gb_ref, nw_ref, o_ref, tail_ref, c_ref, n_ref, m_ref):
    @pl.when(pl.program_id(0) == 0)
    def _():
        tail_ref[...] = jnp.zeros_like(tail_ref)
        c_ref[...] = jnp.zeros_like(c_ref)
        n_ref[...] = jnp.zeros_like(n_ref)
        m_ref[...] = jnp.zeros_like(m_ref)

    p = p_ref[...]
    n = p.shape[0]
    dh = MLSTM_DH
    qk_raw = p[:, :2 * MIX_W]
    qk = _silu(_causal_conv(qk_raw, tail_ref[...], cw_ref[...], cb_ref[...]))
    tail_ref[...] = qk_raw[n - SUBLANES:]
    v_all = p[:, 2 * MIX_W:3 * MIX_W]
    pre = p[:, 3 * MIX_W:3 * MIX_W + LANES] + gb_ref[...]
    og = p[:, 3 * MIX_W + LANES:]
    lf = jnp.minimum(pre, 0.0) - jnp.log1p(jnp.exp(-jnp.abs(pre)))
    cs_all = _cumsum_rows(lf)
    lane = lax.broadcasted_iota(I32, pre.shape, 1)
    rt = jnp.transpose(jnp.where(lane < MLSTM_HEADS, pre, cs_all))
    tril = _row_iota((n, n)) >= lax.broadcasted_iota(I32, (n, n), 1)
    for h in range(MLSTM_HEADS):
        sl = slice(h * dh, (h + 1) * dh)
        q = qk[:, sl] * (dh ** -0.5)
        k = qk[:, MIX_W + h * dh:MIX_W + (h + 1) * dh]
        v = v_all[:, sl]
        qb = q.astype(BF16)
        vb = v.astype(BF16)
        cs = cs_all[:, MLSTM_HEADS + h:MLSTM_HEADS + h + 1]
        ig = pre[:, h:h + 1]
        rowv = rt[h:h + 1, :] - rt[MLSTM_HEADS + h:MLSTM_HEADS + h + 1, :]
        m_prev = m_ref[h:h + 1, 0:1]
        dmat = jnp.where(tril, cs + rowv, NEG_BIG)
        inter = cs + m_prev
        m_row = jnp.maximum(jnp.max(dmat, axis=1, keepdims=True), inter)
        w = jnp.exp(dmat - m_row) * _dot_nt(qb, k.astype(BF16))
        w_inter = jnp.exp(inter - m_row)
        cmem = c_ref[h]
        nmem = n_ref[h:h + 1, :]
        num = _dot(w.astype(BF16), vb) + w_inter * _dot(qb, cmem.astype(BF16))
        den = jnp.sum(w, axis=1, keepdims=True) + w_inter * jnp.sum(q * nmem, axis=1, keepdims=True)
        hs = num / jnp.maximum(jnp.abs(den), jnp.exp(-m_row))
        g_tot = cs[n - 1:n]
        s_end = g_tot - cs + ig
        m_new = jnp.maximum(g_tot + m_prev, jnp.max(s_end, axis=0, keepdims=True))
        kw = k * jnp.exp(s_end - m_new)
        decay = jnp.exp(g_tot + m_prev - m_new)
        c_ref[h] = decay * cmem + _dot(jnp.transpose(kw).astype(BF16), vb)
        n_ref[h:h + 1, :] = decay * nmem + jnp.sum(kw, axis=0, keepdims=True)
        m_ref[h:h + 1, :] = jnp.broadcast_to(m_new, (1, LANES))
        ms = jnp.mean(hs * hs, axis=1, keepdims=True)
        o_ref[:, sl] = (_sigmoid(og[:, sl]) * (hs * lax.rsqrt(ms + EPS) * nw_ref[:, sl])).astype(o_ref.dtype)


def _mlstm(proj, cw, cb, gate_bias, nw):
    t = proj.shape[0]
    n = SCAN_CHUNK
    cc = 2 * MIX_W
    vec = lambda w: pl.BlockSpec((1, w), lambda i: (0, 0))
    return pl.pallas_call(
        _mlstm_kernel,
        grid=(t // n,),
        in_specs=[pl.BlockSpec((n, C_COLS), lambda i: (i, 0)),
                  pl.BlockSpec((CONV_K, cc), lambda i: (0, 0)), vec(cc), vec(LANES), vec(MIX_W)],
        out_specs=pl.BlockSpec((n, MIX_W), lambda i: (i, 0)),
        out_shape=jax.ShapeDtypeStruct((t, MIX_W), BF16),
        scratch_shapes=[pltpu.VMEM((SUBLANES, cc), F32),
                        pltpu.VMEM((MLSTM_HEADS, MLSTM_DH, MLSTM_DH), F32),
                        pltpu.VMEM((SUBLANES, MLSTM_DH), F32),
                        pltpu.VMEM((SUBLANES, LANES), F32)],
        compiler_params=_params(("arbitrary",)),
        name="mlstm",
    )(proj, cw, cb, gate_bias, nw)


def _hgrn_kernel(p_ref, lb_ref, nw_ref, o_ref, st_ref):
    @pl.when(pl.program_id(0) == 0)
    def _():
        st_ref[...] = jnp.zeros_like(st_ref)

    p = p_ref[...]
    n = p.shape[0]
    dk = HGRN_DK
    q = _silu(p[:, :MIX_W])
    u = p[:, MIX_W:2 * MIX_W]
    v = p[:, 2 * MIX_W:3 * MIX_W]
    gt = p[:, 3 * MIX_W:]
    lb = lb_ref[...]
    f = lb + (1.0 - lb) * _sigmoid(u)
    k = (1.0 - lb) * _sigmoid(-u)
    cs = _cumsum_rows(jnp.log(jnp.maximum(f, 1e-30)))
    last = cs[n - 1:n]
    rowi = _row_iota((n, 1))
    vb = v.astype(BF16)

    scores = [None] * HGRN_HEADS
    rr = _row_iota((n, n))
    cc = lax.broadcasted_iota(I32, (n, n), 1)
    hs = n // 2
    while hs >= SUBLANES:
        blk = 2 * hs
        nb = n // blk
        ref = jnp.concatenate([jnp.broadcast_to(cs[b * blk + hs - 1:b * blk + hs], (blk, MIX_W))
                               for b in range(nb)], axis=0)
        upper = (rowi & (blk - 1)) >= hs
        qj = jnp.where(upper, q * jnp.exp(jnp.minimum(cs - ref, 0.0)), 0.0).astype(BF16)
        kj = jnp.where(upper, 0.0, k * jnp.exp(jnp.minimum(ref - cs, 0.0))).astype(BF16)
        same = (rr & -blk) == (cc & -blk)
        for h in range(HGRN_HEADS):
            sl = slice(h * dk, (h + 1) * dk)
            s = _dot_nt(qj[:, sl], kj[:, sl])
            if nb > 1:
                s = jnp.where(same, s, 0.0)
            scores[h] = s if scores[h] is None else scores[h] + s
        hs //= 2

    qe = (q * jnp.exp(cs)).astype(BF16)
    ys = []
    for h in range(HGRN_HEADS):
        sl = slice(h * dk, (h + 1) * dk)
        ys.append(_dot(scores[h].astype(BF16), vb[:, sl]) + _dot_nt(qe[:, sl], st_ref[h].astype(BF16)))

    for dlt in range(SUBLANES):
        if dlt == 0:
            t_all = q * k
            vd = v
        else:
            kd = pltpu.roll(k, dlt, 0)
            csd = pltpu.roll(cs, dlt, 0)
            vd = pltpu.roll(v, dlt, 0)
            t_all = q * kd * jnp.exp(jnp.minimum(cs - csd, 0.0))
        valid = (rowi & (SUBLANES - 1)) >= dlt
        for h in range(HGRN_HEADS):
            sl = slice(h * dk, (h + 1) * dk)
            s = jnp.where(valid, jnp.sum(t_all[:, sl], axis=1, keepdims=True), 0.0)
            ys[h] = ys[h] + s * vd[:, sl]

    kw = k * jnp.exp(last - cs)
    elast = jnp.exp(last)
    for h in range(HGRN_HEADS):
        sl = slice(h * dk, (h + 1) * dk)
        st_ref[h] = elast[:, sl] * st_ref[h] + _dot(jnp.transpose(v[:, sl]).astype(BF16), kw[:, sl].astype(BF16))
        y = ys[h]
        ms = jnp.mean(y * y, axis=1, keepdims=True)
        o_ref[:, sl] = (y * lax.rsqrt(ms + EPS) * nw_ref[:, sl] * _silu(gt[:, sl])).astype(o_ref.dtype)


def _hgrn(proj, lb, nw):
    t = proj.shape[0]
    n = SCAN_CHUNK
    vec = pl.BlockSpec((1, MIX_W), lambda i: (0, 0))
    return pl.pallas_call(
        _hgrn_kernel,
        grid=(t // n,),
        in_specs=[pl.BlockSpec((n, 4 * MIX_W), lambda i: (i, 0)), vec, vec],
        out_specs=pl.BlockSpec((n, MIX_W), lambda i: (i, 0)),
        out_shape=jax.ShapeDtypeStruct((t, MIX_W), BF16),
        scratch_shapes=[pltpu.VMEM((HGRN_HEADS, HGRN_DK, HGRN_DK), F32)],
        compiler_params=_params(("arbitrary",)),
        name="hgrn2",
    )(proj, lb, nw)


def _merge_kernel(h_ref, ba_ref, bb_ref, bc_ref, bd_ref, wg_ref, bg_ref, wu_ref, o_ref):
    h = h_ref[...]
    acc = None
    for br, b_ref in enumerate((ba_ref, bb_ref, bc_ref, bd_ref)):
        gate = _sigmoid(_dot(h, wg_ref[br]) + bg_ref[br])
        term = gate * _dot(b_ref[...], wu_ref[br])
        acc = term if acc is None else acc + term
    o_ref[...] = acc.astype(o_ref.dtype)


def _merge(h, branches, wg, bg, wu):
    t, d = h.shape
    tm, tn = 1024, 256
    nb = len(branches)
    bspec = pl.BlockSpec((tm, MIX_W), lambda i, j: (i, 0))
    return pl.pallas_call(
        _merge_kernel,
        grid=(t // tm, d // tn),
        in_specs=[pl.BlockSpec((tm, d), lambda i, j: (i, 0)), bspec, bspec, bspec, bspec,
                  pl.BlockSpec((nb, d, tn), lambda i, j: (0, 0, j)),
                  pl.BlockSpec((nb, 1, tn), lambda i, j: (0, 0, j)),
                  pl.BlockSpec((nb, MIX_W, tn), lambda i, j: (0, 0, j))],
        out_specs=pl.BlockSpec((tm, tn), lambda i, j: (i, j)),
        out_shape=jax.ShapeDtypeStruct((t, d), BF16),
        compiler_params=_params(("parallel", "parallel")),
        name="merge",
    )(h, *branches, wg, bg, wu)


def _layer_norm_rows(z, w, b):
    mu = jnp.mean(z, axis=1, keepdims=True)
    zc = z - mu
    var = jnp.mean(zc * zc, axis=1, keepdims=True)
    return zc * lax.rsqrt(var + EPS) * w + b


def _out_ln_kernel(m_ref, w_ref, x_ref, g_ref, lw_ref, lb_ref, sc_ref, sh_ref, x_out, h_out):
    y = _dot(m_ref[...], w_ref[...])
    xn = _layer_norm_rows(ALPHA * x_ref[...] + g_ref[...] * y, lw_ref[...], lb_ref[...])
    x_out[...] = xn
    h_out[...] = xn * (1.0 + sc_ref[...]) + sh_ref[...]


def _out_ln(merged, w_out, x, gate, ln_w, ln_b, scale_next, shift_next):
    t, d = x.shape
    tm = LN_TILE
    vec = pl.BlockSpec((1, d), lambda i: (0, 0))
    row = pl.BlockSpec((tm, d), lambda i: (i, 0))
    return pl.pallas_call(
        _out_ln_kernel,
        grid=(t // tm,),
        in_specs=[row, pl.BlockSpec((d, d), lambda i: (0, 0)), row, vec, vec, vec, vec, vec],
        out_specs=[row, row],
        out_shape=[jax.ShapeDtypeStruct((t, d), F32), jax.ShapeDtypeStruct((t, d), F32)],
        compiler_params=_params(("parallel",)),
        name="out_proj_ln",
    )(merged, w_out, x, gate, ln_w, ln_b, scale_next, shift_next)


def _router_kernel(h_ref, w_ref, b_ref, o_ref, cnt_ref, run_ref):
    @pl.when(pl.program_id(0) == 0)
    def _():
        run_ref[...] = jnp.zeros_like(run_ref)

    h = h_ref[...]
    n = h.shape[0]
    logits = jnp.dot(h, w_ref[...], preferred_element_type=F32, precision=lax.Precision.HIGHEST) + b_ref[...]
    lane = lax.broadcasted_iota(I32, logits.shape, 1)
    lane_f = lane.astype(F32)
    big = float(LANES)
    gl = jnp.where(lane < N_GROUPS, logits, NEG_BIG)
    ge = jnp.exp(gl - jnp.max(gl, axis=1, keepdims=True))
    gp = ge / jnp.sum(ge, axis=1, keepdims=True)
    p_grp = jnp.max(gp, axis=1, keepdims=True)
    grp = jnp.min(jnp.where(gp == p_grp, lane_f, big), axis=1, keepdims=True)
    lo = N_GROUPS + EXP_PER_GROUP * grp
    el = jnp.where((lane_f >= lo) & (lane_f < lo + EXP_PER_GROUP), logits, NEG_BIG)
    v1 = jnp.max(el, axis=1, keepdims=True)
    i1 = jnp.min(jnp.where(el == v1, lane_f, big), axis=1, keepdims=True)
    el2 = jnp.where(lane_f == i1, NEG_BIG, el)
    v2 = jnp.max(el2, axis=1, keepdims=True)
    i2 = jnp.min(jnp.where(el2 == v2, lane_f, big), axis=1, keepdims=True)
    e2 = jnp.exp(v2 - v1)
    w1 = (1.0 / (1.0 + e2)) * p_grp
    w2 = (e2 / (1.0 + e2)) * p_grp
    oh1 = jnp.where(lane_f == i1, 1.0, 0.0)
    oh2 = jnp.where(lane_f == i2, 1.0, 0.0)
    strict = (_row_iota((n, n)) > lax.broadcasted_iota(I32, (n, n), 1)).astype(BF16)
    run = run_ref[...]
    c1 = _dot(strict, oh1.astype(BF16)) + run[0:1]
    c2 = _dot(strict, oh2.astype(BF16)) + run[1:2]
    r1 = jnp.sum(oh1 * c1, axis=1, keepdims=True)
    r2 = jnp.sum(oh2 * c2, axis=1, keepdims=True)
    run_ref[0:1] = run[0:1] + jnp.sum(oh1, axis=0, keepdims=True)
    run_ref[1:2] = run[1:2] + jnp.sum(oh2, axis=0, keepdims=True)
    cnt_ref[...] = run_ref[...]
    out = jnp.where(lane == 0, i1 - N_GROUPS, 0.0)
    out = jnp.where(lane == 1, i2 - N_GROUPS, out)
    out = jnp.where(lane == 2, w1, out)
    out = jnp.where(lane == 3, w2, out)
    out = jnp.where(lane == 4, r1, out)
    out = jnp.where(lane == 5, r2, out)
    o_ref[...] = out


def _router(h, w, b):
    t, d = h.shape
    tm = LN_TILE
    return pl.pallas_call(
        _router_kernel,
        grid=(t // tm,),
        in_specs=[pl.BlockSpec((tm, d), lambda i: (i, 0)),
                  pl.BlockSpec((d, LANES), lambda i: (0, 0)),
                  pl.BlockSpec((1, LANES), lambda i: (0, 0))],
        out_specs=[pl.BlockSpec((tm, LANES), lambda i: (i, 0)),
                   pl.BlockSpec((SUBLANES, LANES), lambda i: (0, 0))],
        out_shape=[jax.ShapeDtypeStruct((t, LANES), F32), jax.ShapeDtypeStruct((SUBLANES, LANES), F32)],
        scratch_shapes=[pltpu.VMEM((SUBLANES, LANES), F32)],
        compiler_params=_params(("arbitrary",)),
        name="router",
    )(h, w, b)


def _dispatch_kernel(dest_ref, h_ref, xs_in_ref, xs_ref, sem):
    del xs_in_ref
    nt = dest_ref.shape[2] // 2
    base = pl.program_id(0) * nt

    def copy(r, j):
        return pltpu.make_async_copy(h_ref.at[pl.ds(base + r, 1)],
                                     xs_ref.at[pl.ds(dest_ref[0, 0, 2 * r + j], 1)], sem)

    def issue(r, carry):
        copy(r, 0).start()
        copy(r, 1).start()
        return carry

    def drain(r, carry):
        copy(r, 0).wait()
        copy(r, 1).wait()
        return carry

    lax.fori_loop(0, nt, issue, 0)
    lax.fori_loop(0, nt, drain, 0)


def _dispatch(h, dest, n_slots):
    t, d = h.shape
    nt = DISPATCH_TILE
    xs0 = jnp.zeros((n_slots, d), F32)
    return pl.pallas_call(
        _dispatch_kernel,
        grid=(t // nt,),
        in_specs=[pl.BlockSpec((1, 1, 2 * nt), lambda i: (i, 0, 0), memory_space=pltpu.SMEM),
                  pl.BlockSpec(memory_space=pl.ANY),
                  pl.BlockSpec(memory_space=pl.ANY)],
        out_specs=pl.BlockSpec(memory_space=pl.ANY),
        out_shape=jax.ShapeDtypeStruct((n_slots, d), F32),
        scratch_shapes=[pltpu.SemaphoreType.DMA(())],
        input_output_aliases={2: 0},
        compiler_params=pltpu.CompilerParams(dimension_semantics=("arbitrary",), has_side_effects=True),
        name="moe_dispatch",
    )(dest.reshape(t // nt, 1, 2 * nt), h, xs0)


def _expert_kernel(be_ref, nu_ref, xs_ref, wg_ref, wu_ref, wd_ref, ys_ref):
    del be_ref
    b = pl.program_id(0)

    @pl.when(b < nu_ref[0])
    def _():
        rows = xs_ref[...].astype(BF16)
        hid = _silu(_dot(rows, wg_ref[0].astype(BF16))) * _dot(rows, wu_ref[0].astype(BF16))
        ys_ref[...] = _dot(hid.astype(BF16), wd_ref[0].astype(BF16))

    @pl.when(b >= nu_ref[0])
    def _():
        ys_ref[...] = jnp.zeros_like(ys_ref)


def _experts(blk_e, n_used, xs, w_g, w_u, w_d):
    n_slots, d = xs.shape
    f = w_g.shape[2]
    bm = MOE_BM
    grid_spec = pltpu.PrefetchScalarGridSpec(
        num_scalar_prefetch=2,
        grid=(n_slots // bm,),
        in_specs=[pl.BlockSpec((bm, d), lambda b, be, nu: (b, 0)),
                  pl.BlockSpec((1, d, f), lambda b, be, nu: (be[b], 0, 0)),
                  pl.BlockSpec((1, d, f), lambda b, be, nu: (be[b], 0, 0)),
                  pl.BlockSpec((1, f, d), lambda b, be, nu: (be[b], 0, 0))],
        out_specs=pl.BlockSpec((bm, d), lambda b, be, nu: (b, 0)),
    )
    return pl.pallas_call(
        _expert_kernel,
        grid_spec=grid_spec,
        out_shape=jax.ShapeDtypeStruct((n_slots, d), F32),
        compiler_params=_params(("arbitrary",)),
        name="moe_experts",
    )(blk_e, n_used, xs, w_g, w_u, w_d)


def _combine_ln_kernel(dest_ref, ys_ref, r_ref, x_ref, g_ref, lw_ref, lb_ref, sc_ref, sh_ref,
                       x_out, h_out, buf, sem):
    nt = buf.shape[1]

    def copy(r, j):
        return pltpu.make_async_copy(ys_ref.at[pl.ds(dest_ref[0, 0, 2 * r + j], 1)],
                                     buf.at[j, pl.ds(r, 1)], sem.at[j])

    def issue(r, carry):
        copy(r, 0).start()
        copy(r, 1).start()
        return carry

    def drain(r, carry):
        copy(r, 0).wait()
        copy(r, 1).wait()
        return carry

    lax.fori_loop(0, nt, issue, 0)
    lax.fori_loop(0, nt, drain, 0)
    rt = r_ref[...]
    y = buf[0] * rt[:, 2:3] + buf[1] * rt[:, 3:4]
    xn = _layer_norm_rows(ALPHA * x_ref[...] + g_ref[...] * y, lw_ref[...], lb_ref[...])
    x_out[...] = xn
    h_out[...] = (xn * (1.0 + sc_ref[...]) + sh_ref[...]).astype(h_out.dtype)


def _combine_ln(dest, ys, route, x, gate, ln_w, ln_b, scale_next, shift_next):
    t, d = x.shape
    tm = LN_TILE
    vec = pl.BlockSpec((1, d), lambda i: (0, 0))
    row = pl.BlockSpec((tm, d), lambda i: (i, 0))
    return pl.pallas_call(
        _combine_ln_kernel,
        grid=(t // tm,),
        in_specs=[pl.BlockSpec((1, 1, 2 * tm), lambda i: (i, 0, 0), memory_space=pltpu.SMEM),
                  pl.BlockSpec(memory_space=pl.ANY),
                  pl.BlockSpec((tm, LANES), lambda i: (i, 0)),
                  row, vec, vec, vec, vec, vec],
        out_specs=[row, row],
        out_shape=[jax.ShapeDtypeStruct((t, d), F32), jax.ShapeDtypeStruct((t, d), BF16)],
        scratch_shapes=[pltpu.VMEM((2, tm, d), F32), pltpu.SemaphoreType.DMA((2,))],
        compiler_params=_params(("arbitrary",)),
        name="moe_combine_ln",
    )(dest.reshape(t // tm, 1, 2 * tm), ys, route, x, gate, ln_w, ln_b, scale_next, shift_next)


def _moe_plan(route, counts):
    bm = MOE_BM
    t = route.shape[0]
    n_blocks = (2 * t) // bm + N_EXPERTS
    c1 = counts[0, N_GROUPS:N_GROUPS + N_EXPERTS].astype(I32)
    c2 = counts[1, N_GROUPS:N_GROUPS + N_EXPERTS].astype(I32)
    pcounts = (c1 + c2 + bm - 1) // bm * bm
    pends = jnp.cumsum(pcounts)
    pstart = pends - pcounts
    e1 = route[:, 0].astype(I32)
    e2 = route[:, 1].astype(I32)
    d1 = pstart[e1] + route[:, 4].astype(I32)
    d2 = pstart[e2] + c1[e2] + route[:, 5].astype(I32)
    dest = jnp.stack([d1, d2], axis=1)
    blk_e = jnp.minimum(jnp.searchsorted(pends, jnp.arange(n_blocks, dtype=I32) * bm, side='right'),
                        N_EXPERTS - 1).astype(I32)
    n_used = (pends[-1] // bm).astype(I32).reshape(1)
    return dest, blk_e, n_used, n_blocks * bm


def _block_diag(w):
    nb, bw, _ = w.shape
    eye = jnp.eye(nb, dtype=w.dtype)
    return (eye[:, None, :, None] * w[:, :, None, :]).reshape(nb * bw, nb * bw)


def _pad_lanes(v, width=LANES):
    return jnp.pad(v, [(0, 0)] * (v.ndim - 1) + [(0, width - v.shape[-1])])


def _split_w_in(w):
    o = 0
    parts = []
    for s in (MIX_W, MIX_W, MIX_W, MIX_W, SSD_G * SSD_N, SSD_G * SSD_N, SSD_HEADS, MIX_W, MIX_W, MIX_W,
              MLSTM_HEADS, MLSTM_HEADS, MIX_W, MIX_W, MIX_W, MIX_W, MIX_W):
        parts.append(w[:, o:o + s])
        o += s
    a_x, a_g, b_z, b_x, b_b, b_c, b_dt, c_q, c_k, c_v, c_i, c_f, c_o, d_q, d_f, d_i, d_g = parts
    wa = jnp.concatenate([a_x, a_g], axis=1)
    wb = jnp.concatenate([b_z, b_x, b_b, b_c, _pad_lanes(b_dt), jnp.repeat(b_dt, SSD_P, axis=1)], axis=1)
    wc = jnp.concatenate([c_q, c_k, c_v, _pad_lanes(jnp.concatenate([c_i, c_f], axis=1)), c_o], axis=1)
    wd = jnp.concatenate([d_q, d_f, d_i, d_g], axis=1)
    return [m.astype(BF16) for m in (wa, wb, wc, wd)]


def kernel(x, c, ada_w, ada_b, w_in, lru_conv_w, lru_conv_b, lru_wa, lru_ba, lru_wx, lru_bx, lru_lambda, ssd_conv_w, ssd_conv_b, ssd_dt_bias, ssd_a_log, ssd_d, ssd_norm_w, mlstm_conv_w, mlstm_conv_b, mlstm_i_bias, mlstm_f_bias, mlstm_norm_w, hgrn_lb_param, hgrn_norm_w, w_up, w_gate, b_gate, w_out, ln1_w, ln1_b, router_group_w, router_group_b, router_expert_w, router_expert_b, exp_w_gate, exp_w_up, exp_w_down, ln2_w, ln2_b):
    bsz, seq, d = x.shape
    depth = w_in.shape[0]
    assert bsz == 1 and d == D_MODEL
    xs = x.reshape(seq, d)
    ada = _ada_table(c, ada_w, ada_b)
    lb_all = _lb_table(hgrn_lb_param)
    row = lambda v: v.reshape(1, -1)

    def mod_vec(l, k):
        return ada[l, k * d:(k + 1) * d].reshape(1, d)

    h = _modulate(xs, mod_vec(0, 1), mod_vec(0, 0))
    for l in range(depth):
        wa, wb, wc, wd = _split_w_in(w_in[l])
        br_a = _rglru(_project(h, wa), lru_conv_w[l], row(lru_conv_b[l]),
                      _block_diag(lru_wa[l]).astype(BF16), row(lru_ba[l]),
                      _block_diag(lru_wx[l]).astype(BF16), row(lru_bx[l]), row(lru_lambda[l]))
        a_heads = -jnp.exp(ssd_a_log[l])
        br_b = _ssd(_project(h, wb), ssd_conv_w[l], row(ssd_conv_b[l]),
                    row(_pad_lanes(ssd_dt_bias[l])), row(jnp.repeat(ssd_dt_bias[l], SSD_P)),
                    row(_pad_lanes(a_heads)), row(jnp.repeat(a_heads, SSD_P)),
                    row(jnp.repeat(ssd_d[l], SSD_P)), row(ssd_norm_w[l]))
        gate_bias = _pad_lanes(jnp.concatenate([mlstm_i_bias[l], mlstm_f_bias[l]]))
        br_c = _mlstm(_project(h, wc), mlstm_conv_w[l], row(mlstm_conv_b[l]), row(gate_bias),
                      row(mlstm_norm_w[l]))
        br_d = _hgrn(_project(h, wd), row(lb_all[l]), row(hgrn_norm_w[l]))
        merged = _merge(h, (br_a, br_b, br_c, br_d), w_gate[l].astype(BF16),
                        b_gate[l].reshape(4, 1, d), w_up[l].astype(BF16))
        xs, h2 = _out_ln(merged, w_out[l].astype(BF16), xs, mod_vec(l, 2), row(ln1_w[l]), row(ln1_b[l]),
                         mod_vec(l, 4), mod_vec(l, 3))
        wr = _pad_lanes(jnp.concatenate([router_group_w[l], router_expert_w[l]], axis=1))
        br = row(_pad_lanes(jnp.concatenate([router_group_b[l], router_expert_b[l]])))
        route, counts = _router(h2, wr, br)
        dest, blk_e, n_used, n_slots = _moe_plan(route, counts)
        xsort = _dispatch(h2, dest, n_slots)
        ysort = _experts(blk_e, n_used, xsort, exp_w_gate[l], exp_w_up[l], exp_w_down[l])
        nl = min(l + 1, depth - 1)
        xs, h = _combine_ln(dest, ysort, route, xs, mod_vec(l, 5), row(ln2_w[l]), row(ln2_b[l]),
                            mod_vec(nl, 1), mod_vec(nl, 0))
    return xs.reshape(bsz, seq, d)
```

```python
import functools

import jax
import jax.numpy as jnp
from jax import lax
from jax.experimental import pallas as pl
from jax.experimental.pallas import tpu as pltpu

F32 = jnp.float32
BF16 = jnp.bfloat16
I32 = jnp.int32

D_MODEL = 2048
DEPTH = 4
MIX_W = 512
CONV_K = 4
LRU_BLOCKS = 8
LRU_BW = MIX_W // LRU_BLOCKS
LRU_C = 8.0
SSD_HEADS = 8
SSD_P = 64
SSD_G = 2
SSD_HPG = 4
SSD_N = 128
MLSTM_HEADS = 4
MLSTM_DH = 128
HGRN_HEADS = 4
HGRN_DK = 128
N_GROUPS = 4
EXP_PER_GROUP = 8
N_EXPERTS = 32
D_FF = 512
ALPHA = (2.0 * DEPTH) ** 0.25
EPS = 1e-5
NEG_BIG = -1e30

LANES = 128
SUBLANES = 8
VMEM_LIMIT = 56 * 1024 * 1024

SCAN_CHUNK = 128
MIXER_ROWS = 256
LRU_CHUNK = 256
MOE_BM = 256
ROW_TILE = 512
LN_TILE = 256
DISPATCH_TILE = 512

B_COLS = 3 * MIX_W + LANES + MIX_W
C_COLS = 3 * MIX_W + LANES + MIX_W


def _params(sem, vmem=VMEM_LIMIT):
    return pltpu.CompilerParams(dimension_semantics=sem, vmem_limit_bytes=vmem)


def _dot(a, b):
    return jnp.dot(a, b, preferred_element_type=F32)


def _dot_nt(a, b):
    return lax.dot_general(a, b, (((1,), (1,)), ((), ())), preferred_element_type=F32)


def _sigmoid(x):
    return jax.nn.sigmoid(x)


def _silu(x):
    return x * jax.nn.sigmoid(x)


def _softplus(x):
    return jnp.maximum(x, 0.0) + jnp.log1p(jnp.exp(-jnp.abs(x)))


def _gelu_tanh(x):
    return 0.5 * x * (1.0 + jnp.tanh(0.7978845608028654 * (x + 0.044715 * (x * x * x))))


def _row_iota(shape):
    return lax.broadcasted_iota(I32, shape, 0)


def _cumsum_rows(x):
    n = x.shape[0]
    row = _row_iota(x.shape)
    d = 1
    while d < n:
        x = x + jnp.where(row >= d, pltpu.roll(x, d, 0), 0.0)
        d *= 2
    return x


def _shift_rows(x, tail, j):
    xr = pltpu.roll(x, j, 0)
    tr = pltpu.roll(tail, j, 0)
    row = _row_iota(tail.shape)
    top = jnp.where(row < j, tr, xr[0:SUBLANES])
    return jnp.concatenate([top, xr[SUBLANES:]], axis=0)


def _causal_conv(x, tail, w, b):
    acc = x * w[CONV_K - 1:CONV_K] + b
    for j in range(1, CONV_K):
        acc = acc + _shift_rows(x, tail, j) * w[CONV_K - 1 - j:CONV_K - j]
    return acc


def _ada_kernel(c_ref, w_ref, b_ref, o_ref):
    c = c_ref[...]
    o_ref[0] = jnp.dot(_silu(c), w_ref[0], preferred_element_type=F32,
                       precision=lax.Precision.HIGHEST) + b_ref[0]


def _ada_table(c, ada_w, ada_b):
    depth, d, e = ada_w.shape
    tn = 1024
    c8 = jnp.broadcast_to(c.reshape(1, d), (SUBLANES, d))
    out = pl.pallas_call(
        _ada_kernel,
        grid=(depth, e // tn),
        in_specs=[pl.BlockSpec((SUBLANES, d), lambda l, j: (0, 0)),
                  pl.BlockSpec((1, d, tn), lambda l, j: (l, 0, j)),
                  pl.BlockSpec((1, 1, tn), lambda l, j: (l, 0, j))],
        out_specs=pl.BlockSpec((1, SUBLANES, tn), lambda l, j: (l, 0, j)),
        out_shape=jax.ShapeDtypeStruct((depth, SUBLANES, e), F32),
        compiler_params=_params(("parallel", "parallel")),
        name="ada_table",
    )(c8, ada_w, ada_b.reshape(depth, 1, e))
    return out[:, 0, :]


def _lb_kernel(p_ref, o_ref):
    p = p_ref[...]
    depth = p.shape[0]
    rows = [p[l:l + 1] for l in range(depth)]
    m = rows[0]
    for r in rows[1:]:
        m = jnp.maximum(m, r)
    es = [jnp.exp(r - m) for r in rows]
    tot = es[0]
    for e in es[1:]:
        tot = tot + e
    soft = [e / tot for e in es]
    acc = soft[0]
    o_ref[0:1, :] = acc - soft[0]
    for l in range(1, depth):
        acc = acc + soft[l]
        o_ref[l:l + 1, :] = acc - soft[0]


def _lb_table(p):
    return pl.pallas_call(
        _lb_kernel,
        out_shape=jax.ShapeDtypeStruct(p.shape, F32),
        name="hgrn_lb_table",
    )(p)


def _modulate_kernel(x_ref, sc_ref, sh_ref, o_ref):
    o_ref[...] = (x_ref[...] * (1.0 + sc_ref[...]) + sh_ref[...]).astype(o_ref.dtype)


def _modulate(x, scale, shift):
    t, d = x.shape
    tm = ROW_TILE
    return pl.pallas_call(
        _modulate_kernel,
        grid=(t // tm,),
        in_specs=[pl.BlockSpec((tm, d), lambda i: (i, 0)),
                  pl.BlockSpec((1, d), lambda i: (0, 0)),
                  pl.BlockSpec((1, d), lambda i: (0, 0))],
        out_specs=pl.BlockSpec((tm, d), lambda i: (i, 0)),
        out_shape=jax.ShapeDtypeStruct((t, d), BF16),
        compiler_params=_params(("parallel",)),
        name="modulate",
    )(x, scale, shift)


def _proj_kernel(h_ref, w_ref, o_ref):
    o_ref[...] = _dot(h_ref[...], w_ref[...])


def _project(h, w):
    t, d = h.shape
    n = w.shape[1]
    tm = ROW_TILE
    return pl.pallas_call(
        _proj_kernel,
        grid=(t // tm,),
        in_specs=[pl.BlockSpec((tm, d), lambda i: (i, 0)),
                  pl.BlockSpec((d, n), lambda i: (0, 0))],
        out_specs=pl.BlockSpec((tm, n), lambda i: (i, 0)),
        out_shape=jax.ShapeDtypeStruct((t, n), F32),
        compiler_params=_params(("parallel",)),
        name="in_proj",
    )(h, w)


def _rglru_kernel(p_ref, cw_ref, cb_ref, wa_ref, ba_ref, wx_ref, bx_ref, lam_ref, o_ref, tail_ref, h_ref):
    @pl.when(pl.program_id(0) == 0)
    def _():
        tail_ref[...] = jnp.zeros_like(tail_ref)
        h_ref[...] = jnp.zeros_like(h_ref)

    p = p_ref[...]
    n = p.shape[0]
    ax = p[:, :MIX_W]
    ag = p[:, MIX_W:]
    xc = _causal_conv(ax, tail_ref[...], cw_ref[...], cb_ref[...])
    tail_ref[...] = ax[n - SUBLANES:]
    xb = xc.astype(BF16)
    r = _sigmoid(_dot(xb, wa_ref[...]) + ba_ref[...])
    gi = _sigmoid(_dot(xb, wx_ref[...]) + bx_ref[...])
    log_a = (-LRU_C) * r * _softplus(-lam_ref[...])
    a = jnp.exp(log_a)
    u = jnp.sqrt(-jnp.tanh(log_a) * (a * a + 1.0)) * (gi * xc)
    row = _row_iota(a.shape)
    d = 1
    while d < n:
        keep = row >= d
        a_s = pltpu.roll(a, d, 0)
        u_s = pltpu.roll(u, d, 0)
        u = jnp.where(keep, a * u_s + u, u)
        a = jnp.where(keep, a * a_s, a)
        d *= 2
    h = u + a * h_ref[0:1]
    h_ref[...] = jnp.broadcast_to(h[n - 1:n], h_ref.shape)
    o_ref[...] = (h * _gelu_tanh(ag)).astype(o_ref.dtype)


def _rglru(proj, cw, cb, wa, ba, wx, bx, lam):
    t = proj.shape[0]
    n = LRU_CHUNK
    vec = pl.BlockSpec((1, MIX_W), lambda i: (0, 0))
    sq = pl.BlockSpec((MIX_W, MIX_W), lambda i: (0, 0))
    return pl.pallas_call(
        _rglru_kernel,
        grid=(t // n,),
        in_specs=[pl.BlockSpec((n, 2 * MIX_W), lambda i: (i, 0)),
                  pl.BlockSpec((CONV_K, MIX_W), lambda i: (0, 0)), vec, sq, vec, sq, vec, vec],
        out_specs=pl.BlockSpec((n, MIX_W), lambda i: (i, 0)),
        out_shape=jax.ShapeDtypeStruct((t, MIX_W), BF16),
        scratch_shapes=[pltpu.VMEM((SUBLANES, MIX_W), F32), pltpu.VMEM((SUBLANES, MIX_W), F32)],
        compiler_params=_params(("arbitrary",)),
        name="rglru",
    )(proj, cw, cb, wa, ba, wx, bx, lam)


def _chunked(chunk_fn, state_refs, p_ref, o_ref, *refs):
    @pl.when(pl.program_id(0) == 0)
    def _():
        for s in state_refs:
            s[...] = jnp.zeros_like(s)

    for sub in range(p_ref.shape[0] // SCAN_CHUNK):
        rows = pl.ds(sub * SCAN_CHUNK, SCAN_CHUNK)
        chunk_fn(p_ref.at[rows], *refs, o_ref.at[rows], *state_refs)


def _ssd_kernel(p_ref, cw_ref, cb_ref, dtbn_ref, dtbx_ref, an_ref, ax_ref, dx_ref, nw_ref, o_ref, tail_ref, st_ref):
    _chunked(_ssd_chunk, (tail_ref, st_ref), p_ref, o_ref,
             cw_ref, cb_ref, dtbn_ref, dtbx_ref, an_ref, ax_ref, dx_ref, nw_ref)


def _ssd_chunk(p_ref, cw_ref, cb_ref, dtbn_ref, dtbx_ref, an_ref, ax_ref, dx_ref, nw_ref, o_ref, tail_ref, st_ref):
    p = p_ref[...]
    n = p.shape[0]
    gw = SSD_HPG * SSD_P
    z = p[:, 0:MIX_W]
    xbc_raw = p[:, MIX_W:3 * MIX_W]
    dtn_raw = p[:, 3 * MIX_W:3 * MIX_W + LANES]
    dtx_raw = p[:, 3 * MIX_W + LANES:]
    xbc = _silu(_causal_conv(xbc_raw, tail_ref[...], cw_ref[...], cb_ref[...]))
    tail_ref[...] = xbc_raw[n - SUBLANES:]
    xv = xbc[:, :MIX_W]
    bm = xbc[:, MIX_W:MIX_W + SSD_G * SSD_N]
    cm = xbc[:, MIX_W + SSD_G * SSD_N:]
    dtx = _softplus(dtx_raw + dtbx_ref[...])
    csx = _cumsum_rows(dtx * ax_ref[...])
    dtn = _softplus(dtn_raw + dtbn_ref[...])
    csn_t = jnp.transpose(_cumsum_rows(dtn * an_ref[...]))
    xdt = xv * dtx
    tril = _row_iota((n, n)) >= lax.broadcasted_iota(I32, (n, n), 1)
    lane_head = lax.shift_right_logical(lax.broadcasted_iota(I32, (n, gw), 1), SSD_P.bit_length() - 1)
    ys = []
    for g in range(SSD_G):
        seg = slice(g * gw, (g + 1) * gw)
        cg = cm[:, g * SSD_N:(g + 1) * SSD_N].astype(BF16)
        bg = bm[:, g * SSD_N:(g + 1) * SSD_N]
        scores = _dot_nt(cg, bg.astype(BF16))
        csg = csx[:, seg]
        xdt_g = xdt[:, seg]
        xdt_gb = xdt_g.astype(BF16)
        st = st_ref[g]
        y = _dot(cg, st.astype(BF16)) * jnp.exp(csg)
        for h in range(SSD_HPG):
            hh = g * SSD_HPG + h
            col = csx[:, hh * SSD_P:hh * SSD_P + 1]
            dec = jnp.exp(jnp.where(tril, col - csn_t[hh:hh + 1, :], NEG_BIG))
            y = y + jnp.where(lane_head == h, _dot((scores * dec).astype(BF16), xdt_gb), 0.0)
        last = csg[n - 1:n]
        xw = (xdt_g * jnp.exp(last - csg)).astype(BF16)
        st_ref[g] = jnp.exp(last) * st + _dot(jnp.transpose(bg).astype(BF16), xw)
        ys.append(y)
    y = jnp.concatenate(ys, axis=1) + dx_ref[...] * xv
    yz = y * _silu(z)
    ms = jnp.mean(yz * yz, axis=1, keepdims=True)
    o_ref[...] = (yz * lax.rsqrt(ms + EPS) * nw_ref[...]).astype(o_ref.dtype)


def _ssd(proj, cw, cb, dtb_n, dtb_x, a_n, a_x, d_x, nw):
    t = proj.shape[0]
    n = MIXER_ROWS
    cc = 2 * MIX_W
    vec = lambda w: pl.BlockSpec((1, w), lambda i: (0, 0))
    return pl.pallas_call(
        _ssd_kernel,
        grid=(t // n,),
        in_specs=[pl.BlockSpec((n, B_COLS), lambda i: (i, 0)),
                  pl.BlockSpec((CONV_K, cc), lambda i: (0, 0)), vec(cc),
                  vec(LANES), vec(MIX_W), vec(LANES), vec(MIX_W), vec(MIX_W), vec(MIX_W)],
        out_specs=pl.BlockSpec((n, MIX_W), lambda i: (i, 0)),
        out_shape=jax.ShapeDtypeStruct((t, MIX_W), BF16),
        scratch_shapes=[pltpu.VMEM((SUBLANES, cc), F32),
                        pltpu.VMEM((SSD_G, SSD_N, SSD_HPG * SSD_P), F32)],
        compiler_params=_params(("arbitrary",)),
        name="ssd",
    )(proj, cw, cb, dtb_n, dtb_x, a_n, a_x, d_x, nw)


def _mlstm_kernel(p_ref, cw_ref, cb_ref, gb_ref, nw_ref, o_ref, tail_ref, c_ref, n_ref, m_ref):
    _chunked(_mlstm_chunk, (tail_ref, c_ref, n_ref, m_ref), p_ref, o_ref, cw_ref, cb_ref, gb_ref, nw_ref)


def _mlstm_chunk(p_ref, cw_ref, cb_ref, gb_ref, nw_ref, o_ref, tail_ref, c_ref, n_ref, m_ref):
    p = p_ref[...]
    n = p.shape[0]
    dh = MLSTM_DH
    qk_raw = p[:, :2 * MIX_W]
    qk = _silu(_causal_conv(qk_raw, tail_ref[...], cw_ref[...], cb_ref[...]))
    tail_ref[...] = qk_raw[n - SUBLANES:]
    v_all = p[:, 2 * MIX_W:3 * MIX_W]
    pre = p[:, 3 * MIX_W:3 * MIX_W + LANES] + gb_ref[...]
    og = p[:, 3 * MIX_W + LANES:]
    lf = jnp.minimum(pre, 0.0) - jnp.log1p(jnp.exp(-jnp.abs(pre)))
    cs_all = _cumsum_rows(lf)
    lane = lax.broadcasted_iota(I32, pre.shape, 1)
    rt = jnp.transpose(jnp.where(lane < MLSTM_HEADS, pre, cs_all))
    tril = _row_iota((n, n)) >= lax.broadcasted_iota(I32, (n, n), 1)
    for h in range(MLSTM_HEADS):
        sl = slice(h * dh, (h + 1) * dh)
        q = qk[:, sl] * (dh ** -0.5)
        k = qk[:, MIX_W + h * dh:MIX_W + (h + 1) * dh]
        v = v_all[:, sl]
        qb = q.astype(BF16)
        vb = v.astype(BF16)
        cs = cs_all[:, MLSTM_HEADS + h:MLSTM_HEADS + h + 1]
        ig = pre[:, h:h + 1]
        rowv = rt[h:h + 1, :] - rt[MLSTM_HEADS + h:MLSTM_HEADS + h + 1, :]
        m_prev = m_ref[h:h + 1, 0:1]
        dmat = jnp.where(tril, cs + rowv, NEG_BIG)
        inter = cs + m_prev
        m_row = jnp.maximum(jnp.max(dmat, axis=1, keepdims=True), inter)
        w = jnp.exp(dmat - m_row) * _dot_nt(qb, k.astype(BF16))
        w_inter = jnp.exp(inter - m_row)
        cmem = c_ref[h]
        nmem = n_ref[h:h + 1, :]
        num = _dot(w.astype(BF16), vb) + w_inter * _dot(qb, cmem.astype(BF16))
        den = jnp.sum(w, axis=1, keepdims=True) + w_inter * jnp.sum(q * nmem, axis=1, keepdims=True)
        hs = num / jnp.maximum(jnp.abs(den), jnp.exp(-m_row))
        g_tot = cs[n - 1:n]
        s_end = g_tot - cs + ig
        m_new = jnp.maximum(g_tot + m_prev, jnp.max(s_end, axis=0, keepdims=True))
        kw = k * jnp.exp(s_end - m_new)
        decay = jnp.exp(g_tot + m_prev - m_new)
        c_ref[h] = decay * cmem + _dot(jnp.transpose(kw).astype(BF16), vb)
        n_ref[h:h + 1, :] = decay * nmem + jnp.sum(kw, axis=0, keepdims=True)
        m_ref[h:h + 1, :] = jnp.broadcast_to(m_new, (1, LANES))
        ms = jnp.mean(hs * hs, axis=1, keepdims=True)
        o_ref[:, sl] = (_sigmoid(og[:, sl]) * (hs * lax.rsqrt(ms + EPS) * nw_ref[:, sl])).astype(o_ref.dtype)


def _mlstm(proj, cw, cb, gate_bias, nw):
    t = proj.shape[0]
    n = SCAN_CHUNK
    cc = 2 * MIX_W
    vec = lambda w: pl.BlockSpec((1, w), lambda i: (0, 0))
    return pl.pallas_call(
        _mlstm_kernel,
        grid=(t // n,),
        in_specs=[pl.BlockSpec((n, C_COLS), lambda i: (i, 0)),
                  pl.BlockSpec((CONV_K, cc), lambda i: (0, 0)), vec(cc), vec(LANES), vec(MIX_W)],
        out_specs=pl.BlockSpec((n, MIX_W), lambda i: (i, 0)),
        out_shape=jax.ShapeDtypeStruct((t, MIX_W), BF16),
        scratch_shapes=[pltpu.VMEM((SUBLANES, cc), F32),
                        pltpu.VMEM((MLSTM_HEADS, MLSTM_DH, MLSTM_DH), F32),
                        pltpu.VMEM((SUBLANES, MLSTM_DH), F32),
                        pltpu.VMEM((SUBLANES, LANES), F32)],
        compiler_params=_params(("arbitrary",)),
        name="mlstm",
    )(proj, cw, cb, gate_bias, nw)


def _hgrn_kernel(p_ref, lb_ref, nw_ref, o_ref, st_ref):
    _chunked(_hgrn_chunk, (st_ref,), p_ref, o_ref, lb_ref, nw_ref)


def _hgrn_chunk(p_ref, lb_ref, nw_ref, o_ref, st_ref):
    p = p_ref[...]
    n = p.shape[0]
    dk = HGRN_DK
    q = _silu(p[:, :MIX_W])
    u = p[:, MIX_W:2 * MIX_W]
    v = p[:, 2 * MIX_W:3 * MIX_W]
    gt = p[:, 3 * MIX_W:]
    lb = lb_ref[...]
    f = lb + (1.0 - lb) * _sigmoid(u)
    k = (1.0 - lb) * _sigmoid(-u)
    cs = _cumsum_rows(jnp.log(jnp.maximum(f, 1e-30)))
    last = cs[n - 1:n]
    rowi = _row_iota((n, 1))
    vb = v.astype(BF16)

    scores = [None] * HGRN_HEADS
    rr = _row_iota((n, n))
    cc = lax.broadcasted_iota(I32, (n, n), 1)
    hs = n // 2
    while hs >= SUBLANES:
        blk = 2 * hs
        nb = n // blk
        ref = jnp.concatenate([jnp.broadcast_to(cs[b * blk + hs - 1:b * blk + hs], (blk, MIX_W))
                               for b in range(nb)], axis=0)
        upper = (rowi & (blk - 1)) >= hs
        qj = jnp.where(upper, q * jnp.exp(jnp.minimum(cs - ref, 0.0)), 0.0).astype(BF16)
        kj = jnp.where(upper, 0.0, k * jnp.exp(jnp.minimum(ref - cs, 0.0))).astype(BF16)
        same = (rr & -blk) == (cc & -blk)
        for h in range(HGRN_HEADS):
            sl = slice(h * dk, (h + 1) * dk)
            s = _dot_nt(qj[:, sl], kj[:, sl])
            if nb > 1:
                s = jnp.where(same, s, 0.0)
            scores[h] = s if scores[h] is None else scores[h] + s
        hs //= 2

    qe = (q * jnp.exp(cs)).astype(BF16)
    ys = []
    for h in range(HGRN_HEADS):
        sl = slice(h * dk, (h + 1) * dk)
        ys.append(_dot(scores[h].astype(BF16), vb[:, sl]) + _dot_nt(qe[:, sl], st_ref[h].astype(BF16)))

    for dlt in range(SUBLANES):
        if dlt == 0:
            t_all = q * k
            vd = v
        else:
            kd = pltpu.roll(k, dlt, 0)
            csd = pltpu.roll(cs, dlt, 0)
            vd = pltpu.roll(v, dlt, 0)
            t_all = q * kd * jnp.exp(jnp.minimum(cs - csd, 0.0))
        valid = (rowi & (SUBLANES - 1)) >= dlt
        for h in range(HGRN_HEADS):
            sl = slice(h * dk, (h + 1) * dk)
            s = jnp.where(valid, jnp.sum(t_all[:, sl], axis=1, keepdims=True), 0.0)
            ys[h] = ys[h] + s * vd[:, sl]

    kw = k * jnp.exp(last - cs)
    elast = jnp.exp(last)
    for h in range(HGRN_HEADS):
        sl = slice(h * dk, (h + 1) * dk)
        st_ref[h] = elast[:, sl] * st_ref[h] + _dot(jnp.transpose(v[:, sl]).astype(BF16), kw[:, sl].astype(BF16))
        y = ys[h]
        ms = jnp.mean(y * y, axis=1, keepdims=True)
        o_ref[:, sl] = (y * lax.rsqrt(ms + EPS) * nw_ref[:, sl] * _silu(gt[:, sl])).astype(o_ref.dtype)


def _hgrn(proj, lb, nw):
    t = proj.shape[0]
    n = MIXER_ROWS
    vec = pl.BlockSpec((1, MIX_W), lambda i: (0, 0))
    return pl.pallas_call(
        _hgrn_kernel,
        grid=(t // n,),
        in_specs=[pl.BlockSpec((n, 4 * MIX_W), lambda i: (i, 0)), vec, vec],
        out_specs=pl.BlockSpec((n, MIX_W), lambda i: (i, 0)),
        out_shape=jax.ShapeDtypeStruct((t, MIX_W), BF16),
        scratch_shapes=[pltpu.VMEM((HGRN_HEADS, HGRN_DK, HGRN_DK), F32)],
        compiler_params=_params(("arbitrary",)),
        name="hgrn2",
    )(proj, lb, nw)


def _merge_kernel(h_ref, ba_ref, bb_ref, bc_ref, bd_ref, wg_ref, bg_ref, wu_ref, o_ref, wg_s, wu_s):
    @pl.when(pl.program_id(1) == 0)
    def _():
        wg_s[...] = wg_ref[0].astype(BF16)
        wu_s[...] = wu_ref[0].astype(BF16)

    h = h_ref[...]
    acc = None
    for br, b_ref in enumerate((ba_ref, bb_ref, bc_ref, bd_ref)):
        gate = _sigmoid(_dot(h, wg_s[br]) + bg_ref[br])
        term = gate * _dot(b_ref[...], wu_s[br])
        acc = term if acc is None else acc + term
    o_ref[...] = acc.astype(o_ref.dtype)


def _merge(layer, h, branches, wg, bg, wu):
    t, d = h.shape
    tm, tn = 1024, 256
    nb = len(branches)
    bspec = pl.BlockSpec((tm, MIX_W), lambda j, i: (i, 0))
    return pl.pallas_call(
        _merge_kernel,
        grid=(d // tn, t // tm),
        in_specs=[pl.BlockSpec((tm, d), lambda j, i: (i, 0)), bspec, bspec, bspec, bspec,
                  pl.BlockSpec((1, nb, d, tn), lambda j, i: (layer, 0, 0, j)),
                  pl.BlockSpec((nb, 1, tn), lambda j, i: (0, 0, j)),
                  pl.BlockSpec((1, nb, MIX_W, tn), lambda j, i: (layer, 0, 0, j))],
        out_specs=pl.BlockSpec((tm, tn), lambda j, i: (i, j)),
        out_shape=jax.ShapeDtypeStruct((t, d), BF16),
        scratch_shapes=[pltpu.VMEM((nb, d, tn), BF16), pltpu.VMEM((nb, MIX_W, tn), BF16)],
        compiler_params=_params(("parallel", "arbitrary")),
        name="merge",
    )(h, *branches, wg, bg, wu)


def _layer_norm_rows(z, w, b):
    mu = jnp.mean(z, axis=1, keepdims=True)
    zc = z - mu
    var = jnp.mean(zc * zc, axis=1, keepdims=True)
    return zc * lax.rsqrt(var + EPS) * w + b


def _out_ln_kernel(m_ref, w_ref, x_ref, g_ref, lw_ref, lb_ref, sc_ref, sh_ref, x_out, h_out):
    y = _dot(m_ref[...], w_ref[...])
    xn = _layer_norm_rows(ALPHA * x_ref[...] + g_ref[...] * y, lw_ref[...], lb_ref[...])
    x_out[...] = xn
    h_out[...] = xn * (1.0 + sc_ref[...]) + sh_ref[...]


def _out_ln(merged, w_out, x, gate, ln_w, ln_b, scale_next, shift_next):
    t, d = x.shape
    tm = LN_TILE
    vec = pl.BlockSpec((1, d), lambda i: (0, 0))
    row = pl.BlockSpec((tm, d), lambda i: (i, 0))
    return pl.pallas_call(
        _out_ln_kernel,
        grid=(t // tm,),
        in_specs=[row, pl.BlockSpec((d, d), lambda i: (0, 0)), row, vec, vec, vec, vec, vec],
        out_specs=[row, row],
        out_shape=[jax.ShapeDtypeStruct((t, d), F32), jax.ShapeDtypeStruct((t, d), F32)],
        compiler_params=_params(("parallel",)),
        name="out_proj_ln",
    )(merged, w_out, x, gate, ln_w, ln_b, scale_next, shift_next)


def _router_kernel(h_ref, w_ref, b_ref, o_ref, cnt_ref, run_ref):
    @pl.when(pl.program_id(0) == 0)
    def _():
        run_ref[...] = jnp.zeros_like(run_ref)

    h = h_ref[...]
    n = h.shape[0]
    h_hi = h.astype(BF16)
    h_lo = (h - h_hi.astype(F32)).astype(BF16)
    logits = _dot(h_hi, w_ref[0]) + _dot(h_lo, w_ref[0]) + _dot(h_hi, w_ref[1]) + b_ref[...]
    lane = lax.broadcasted_iota(I32, logits.shape, 1)
    lane_f = lane.astype(F32)
    big = float(LANES)
    gl = jnp.where(lane < N_GROUPS, logits, NEG_BIG)
    ge = jnp.exp(gl - jnp.max(gl, axis=1, keepdims=True))
    gp = ge / jnp.sum(ge, axis=1, keepdims=True)
    p_grp = jnp.max(gp, axis=1, keepdims=True)
    grp = jnp.min(jnp.where(gp == p_grp, lane_f, big), axis=1, keepdims=True)
    lo = N_GROUPS + EXP_PER_GROUP * grp
    el = jnp.where((lane_f >= lo) & (lane_f < lo + EXP_PER_GROUP), logits, NEG_BIG)
    v1 = jnp.max(el, axis=1, keepdims=True)
    i1 = jnp.min(jnp.where(el == v1, lane_f, big), axis=1, keepdims=True)
    el2 = jnp.where(lane_f == i1, NEG_BIG, el)
    v2 = jnp.max(el2, axis=1, keepdims=True)
    i2 = jnp.min(jnp.where(el2 == v2, lane_f, big), axis=1, keepdims=True)
    e2 = jnp.exp(v2 - v1)
    w1 = (1.0 / (1.0 + e2)) * p_grp
    w2 = (e2 / (1.0 + e2)) * p_grp
    oh1 = jnp.where(lane_f == i1, 1.0, 0.0)
    oh2 = jnp.where(lane_f == i2, 1.0, 0.0)
    strict = (_row_iota((n, n)) > lax.broadcasted_iota(I32, (n, n), 1)).astype(BF16)
    run = run_ref[...]
    c1 = _dot(strict, oh1.astype(BF16)) + run[0:1]
    c2 = _dot(strict, oh2.astype(BF16)) + run[1:2]
    r1 = jnp.sum(oh1 * c1, axis=1, keepdims=True)
    r2 = jnp.sum(oh2 * c2, axis=1, keepdims=True)
    run_ref[0:1] = run[0:1] + jnp.sum(oh1, axis=0, keepdims=True)
    run_ref[1:2] = run[1:2] + jnp.sum(oh2, axis=0, keepdims=True)
    cnt_ref[...] = run_ref[...]
    out = jnp.where(lane == 0, i1 - N_GROUPS, 0.0)
    out = jnp.where(lane == 1, i2 - N_GROUPS, out)
    out = jnp.where(lane == 2, w1, out)
    out = jnp.where(lane == 3, w2, out)
    out = jnp.where(lane == 4, r1, out)
    out = jnp.where(lane == 5, r2, out)
    o_ref[...] = out


def _router(h, w, b):
    t, d = h.shape
    tm = LN_TILE
    w_hi = w.astype(BF16)
    return pl.pallas_call(
        _router_kernel,
        grid=(t // tm,),
        in_specs=[pl.BlockSpec((tm, d), lambda i: (i, 0)),
                  pl.BlockSpec((2, d, LANES), lambda i: (0, 0, 0)),
                  pl.BlockSpec((1, LANES), lambda i: (0, 0))],
        out_specs=[pl.BlockSpec((tm, LANES), lambda i: (i, 0)),
                   pl.BlockSpec((SUBLANES, LANES), lambda i: (0, 0))],
        out_shape=[jax.ShapeDtypeStruct((t, LANES), F32), jax.ShapeDtypeStruct((SUBLANES, LANES), F32)],
        scratch_shapes=[pltpu.VMEM((SUBLANES, LANES), F32)],
        compiler_params=_params(("arbitrary",)),
        name="router",
    )(h, jnp.stack([w_hi, (w - w_hi.astype(F32)).astype(BF16)]), b)


def _dispatch_kernel(dest_ref, h_ref, xs_in_ref, xs_ref, sem):
    del xs_in_ref
    nt = h_ref.shape[0]

    def copy(r, j):
        return pltpu.make_async_copy(h_ref.at[pl.ds(r, 1)],
                                     xs_ref.at[pl.ds(dest_ref[0, 0, 2 * r + j], 1)], sem)

    def issue(r, carry):
        copy(r, 0).start()
        copy(r, 1).start()
        return carry

    def drain(r, carry):
        copy(r, 0).wait()
        copy(r, 1).wait()
        return carry

    lax.fori_loop(0, nt, issue, 0, unroll=8)
    lax.fori_loop(0, nt, drain, 0, unroll=8)


def _dispatch(h, dest, n_slots):
    t, d = h.shape
    nt = DISPATCH_TILE
    xs0 = jnp.zeros((n_slots, d), F32)
    return pl.pallas_call(
        _dispatch_kernel,
        grid=(t // nt,),
        in_specs=[pl.BlockSpec((1, 1, 2 * nt), lambda i: (i, 0, 0), memory_space=pltpu.SMEM),
                  pl.BlockSpec((nt, d), lambda i: (i, 0)),
                  pl.BlockSpec(memory_space=pl.ANY)],
        out_specs=pl.BlockSpec(memory_space=pl.ANY),
        out_shape=jax.ShapeDtypeStruct((n_slots, d), F32),
        scratch_shapes=[pltpu.SemaphoreType.DMA(())],
        input_output_aliases={2: 0},
        compiler_params=pltpu.CompilerParams(dimension_semantics=("arbitrary",), has_side_effects=True),
        name="moe_dispatch",
    )(dest.reshape(t // nt, 1, 2 * nt), h, xs0)


def _expert_kernel(be_ref, nu_ref, xs_ref, wg_ref, wu_ref, wd_ref, ys_ref, wg_s, wu_s, wd_s):
    b = pl.program_id(0)
    used = b < nu_ref[0]
    prev = be_ref[jnp.maximum(b - 1, 0)]

    @pl.when(used & ((b == 0) | (be_ref[b] != prev)))
    def _():
        wg_s[...] = wg_ref[0, 0].astype(BF16)
        wu_s[...] = wu_ref[0, 0].astype(BF16)
        wd_s[...] = wd_ref[0, 0].astype(BF16)

    @pl.when(used)
    def _():
        rows = xs_ref[...].astype(BF16)
        hid = _silu(_dot(rows, wg_s[...])) * _dot(rows, wu_s[...])
        ys_ref[...] = _dot(hid.astype(BF16), wd_s[...])

    @pl.when(jnp.logical_not(used))
    def _():
        ys_ref[...] = jnp.zeros_like(ys_ref)


def _experts(layer, blk_e, n_used, xs, w_g, w_u, w_d):
    n_slots, d = xs.shape
    f = w_g.shape[3]
    bm = MOE_BM
    grid_spec = pltpu.PrefetchScalarGridSpec(
        num_scalar_prefetch=2,
        grid=(n_slots // bm,),
        in_specs=[pl.BlockSpec((bm, d), lambda b, be, nu: (b, 0)),
                  pl.BlockSpec((1, 1, d, f), lambda b, be, nu: (layer, be[b], 0, 0)),
                  pl.BlockSpec((1, 1, d, f), lambda b, be, nu: (layer, be[b], 0, 0)),
                  pl.BlockSpec((1, 1, f, d), lambda b, be, nu: (layer, be[b], 0, 0))],
        out_specs=pl.BlockSpec((bm, d), lambda b, be, nu: (b, 0)),
        scratch_shapes=[pltpu.VMEM((d, f), BF16), pltpu.VMEM((d, f), BF16), pltpu.VMEM((f, d), BF16)],
    )
    return pl.pallas_call(
        _expert_kernel,
        grid_spec=grid_spec,
        out_shape=jax.ShapeDtypeStruct((n_slots, d), F32),
        compiler_params=_params(("arbitrary",)),
        name="moe_experts",
    )(blk_e, n_used, xs, w_g, w_u, w_d)


def _combine_ln_kernel(dest_ref, dnext_ref, ys_ref, r_ref, x_ref, g_ref, lw_ref, lb_ref, sc_ref, sh_ref,
                       x_out, h_out, buf, sem):
    nt = buf.shape[2]
    i = pl.program_id(0)
    slot = lax.rem(i, 2)

    def copy(d_ref, s, r, j):
        return pltpu.make_async_copy(ys_ref.at[pl.ds(d_ref[0, 0, 2 * r + j], 1)],
                                     buf.at[s, j, pl.ds(r, 1)], sem.at[s])

    def gather(d_ref, s):
        def issue(r, carry):
            copy(d_ref, s, r, 0).start()
            copy(d_ref, s, r, 1).start()
            return carry
        lax.fori_loop(0, nt, issue, 0, unroll=8)

    @pl.when(i == 0)
    def _():
        gather(dest_ref, 0)

    @pl.when(i + 1 < pl.num_programs(0))
    def _():
        gather(dnext_ref, 1 - slot)

    def drain(r, carry):
        copy(dest_ref, slot, r, 0).wait()
        copy(dest_ref, slot, r, 1).wait()
        return carry

    lax.fori_loop(0, nt, drain, 0, unroll=8)
    rt = r_ref[...]
    y = buf[slot, 0] * rt[:, 2:3] + buf[slot, 1] * rt[:, 3:4]
    xn = _layer_norm_rows(ALPHA * x_ref[...] + g_ref[...] * y, lw_ref[...], lb_ref[...])
    x_out[...] = xn
    h_out[...] = (xn * (1.0 + sc_ref[...]) + sh_ref[...]).astype(h_out.dtype)


def _combine_ln(dest, ys, route, x, gate, ln_w, ln_b, scale_next, shift_next):
    t, d = x.shape
    tm = LN_TILE
    vec = pl.BlockSpec((1, d), lambda i: (0, 0))
    row = pl.BlockSpec((tm, d), lambda i: (i, 0))
    nsteps = t // tm
    dest3 = dest.reshape(nsteps, 1, 2 * tm)
    return pl.pallas_call(
        _combine_ln_kernel,
        grid=(nsteps,),
        in_specs=[pl.BlockSpec((1, 1, 2 * tm), lambda i: (i, 0, 0), memory_space=pltpu.SMEM),
                  pl.BlockSpec((1, 1, 2 * tm), lambda i: (jnp.minimum(i + 1, nsteps - 1), 0, 0),
                               memory_space=pltpu.SMEM),
                  pl.BlockSpec(memory_space=pl.ANY),
                  pl.BlockSpec((tm, LANES), lambda i: (i, 0)),
                  row, vec, vec, vec, vec, vec],
        out_specs=[row, row],
        out_shape=[jax.ShapeDtypeStruct((t, d), F32), jax.ShapeDtypeStruct((t, d), BF16)],
        scratch_shapes=[pltpu.VMEM((2, 2, tm, d), F32), pltpu.SemaphoreType.DMA((2,))],
        compiler_params=_params(("arbitrary",)),
        name="moe_combine_ln",
    )(dest3, dest3, ys, route, x, gate, ln_w, ln_b, scale_next, shift_next)


def _slot_kernel(r_ref, tab_ref, o_ref):
    rt = r_ref[...]
    lane = lax.broadcasted_iota(I32, rt.shape, 1)
    lane_f = lane.astype(F32)
    tab = tab_ref[...]
    d1 = jnp.sum(jnp.where(lane_f == rt[:, 0:1] + N_GROUPS, tab[0:1], 0.0), axis=1, keepdims=True) + rt[:, 4:5]
    d2 = jnp.sum(jnp.where(lane_f == rt[:, 1:2] + N_GROUPS, tab[1:2], 0.0), axis=1, keepdims=True) + rt[:, 5:6]
    o_ref[...] = jnp.where(lane == 0, d1, jnp.where(lane == 1, d2, 0.0)).astype(I32)


def _moe_plan(route, counts):
    bm = MOE_BM
    t = route.shape[0]
    tm = 1024
    n_blocks = (2 * t) // bm + N_EXPERTS
    c1 = counts[0].astype(I32)
    c2 = counts[1].astype(I32)
    pcounts = (c1 + c2 + bm - 1) // bm * bm
    pends = jnp.cumsum(pcounts)
    pstart = pends - pcounts
    tab = jnp.zeros((SUBLANES, LANES), F32).at[0].set(pstart.astype(F32)).at[1].set((pstart + c1).astype(F32))
    slots = pl.pallas_call(
        _slot_kernel,
        grid=(t // tm,),
        in_specs=[pl.BlockSpec((tm, LANES), lambda i: (i, 0)),
                  pl.BlockSpec((SUBLANES, LANES), lambda i: (0, 0))],
        out_specs=pl.BlockSpec((tm, LANES), lambda i: (i, 0)),
        out_shape=jax.ShapeDtypeStruct((t, LANES), I32),
        compiler_params=_params(("parallel",)),
        name="moe_slots",
    )(route, tab)
    dest = slots[:, :2]
    ends = pends[N_GROUPS:N_GROUPS + N_EXPERTS]
    starts_of_blocks = jnp.arange(n_blocks, dtype=I32) * bm
    blk_e = jnp.minimum(jnp.sum((ends[None, :] <= starts_of_blocks[:, None]).astype(I32), axis=1),
                        N_EXPERTS - 1).astype(I32)
    n_used = (pends[-1] // bm).astype(I32).reshape(1)
    return dest, blk_e, n_used, n_blocks * bm


def _block_diag(w):
    nb, bw, _ = w.shape
    eye = jnp.eye(nb, dtype=w.dtype)
    return (eye[:, None, :, None] * w[:, :, None, :]).reshape(nb * bw, nb * bw)


def _pad_lanes(v, width=LANES):
    return jnp.pad(v, [(0, 0)] * (v.ndim - 1) + [(0, width - v.shape[-1])])


def _split_w_in(w):
    o = 0
    parts = []
    for s in (MIX_W, MIX_W, MIX_W, MIX_W, SSD_G * SSD_N, SSD_G * SSD_N, SSD_HEADS, MIX_W, MIX_W, MIX_W,
              MLSTM_HEADS, MLSTM_HEADS, MIX_W, MIX_W, MIX_W, MIX_W, MIX_W):
        parts.append(w[:, o:o + s])
        o += s
    a_x, a_g, b_z, b_x, b_b, b_c, b_dt, c_q, c_k, c_v, c_i, c_f, c_o, d_q, d_f, d_i, d_g = parts
    wa = jnp.concatenate([a_x, a_g], axis=1)
    wb = jnp.concatenate([b_z, b_x, b_b, b_c, _pad_lanes(b_dt), jnp.repeat(b_dt, SSD_P, axis=1)], axis=1)
    wc = jnp.concatenate([c_q, c_k, c_v, _pad_lanes(jnp.concatenate([c_i, c_f], axis=1)), c_o], axis=1)
    wd = jnp.concatenate([d_q, d_f, d_i, d_g], axis=1)
    return [m.astype(BF16) for m in (wa, wb, wc, wd)]


def kernel(x, c, ada_w, ada_b, w_in, lru_conv_w, lru_conv_b, lru_wa, lru_ba, lru_wx, lru_bx, lru_lambda, ssd_conv_w, ssd_conv_b, ssd_dt_bias, ssd_a_log, ssd_d, ssd_norm_w, mlstm_conv_w, mlstm_conv_b, mlstm_i_bias, mlstm_f_bias, mlstm_norm_w, hgrn_lb_param, hgrn_norm_w, w_up, w_gate, b_gate, w_out, ln1_w, ln1_b, router_group_w, router_group_b, router_expert_w, router_expert_b, exp_w_gate, exp_w_up, exp_w_down, ln2_w, ln2_b):
    bsz, seq, d = x.shape
    depth = w_in.shape[0]
    assert bsz == 1 and d == D_MODEL
    xs = x.reshape(seq, d)
    ada = _ada_table(c, ada_w, ada_b)
    lb_all = _lb_table(hgrn_lb_param)
    row = lambda v: v.reshape(1, -1)

    def mod_vec(l, k):
        return ada[l, k * d:(k + 1) * d].reshape(1, d)

    h = _modulate(xs, mod_vec(0, 1), mod_vec(0, 0))
    for l in range(depth):
        wa, wb, wc, wd = _split_w_in(w_in[l])
        br_a = _rglru(_project(h, wa), lru_conv_w[l], row(lru_conv_b[l]),
                      _block_diag(lru_wa[l]).astype(BF16), row(lru_ba[l]),
                      _block_diag(lru_wx[l]).astype(BF16), row(lru_bx[l]), row(lru_lambda[l]))
        a_heads = -jnp.exp(ssd_a_log[l])
        br_b = _ssd(_project(h, wb), ssd_conv_w[l], row(ssd_conv_b[l]),
                    row(_pad_lanes(ssd_dt_bias[l])), row(jnp.repeat(ssd_dt_bias[l], SSD_P)),
                    row(_pad_lanes(a_heads)), row(jnp.repeat(a_heads, SSD_P)),
                    row(jnp.repeat(ssd_d[l], SSD_P)), row(ssd_norm_w[l]))
        gate_bias = _pad_lanes(jnp.concatenate([mlstm_i_bias[l], mlstm_f_bias[l]]))
        br_c = _mlstm(_project(h, wc), mlstm_conv_w[l], row(mlstm_conv_b[l]), row(gate_bias),
                      row(mlstm_norm_w[l]))
        br_d = _hgrn(_project(h, wd), row(lb_all[l]), row(hgrn_norm_w[l]))
        merged = _merge(l, h, (br_a, br_b, br_c, br_d), w_gate, b_gate[l].reshape(4, 1, d), w_up)
        xs, h2 = _out_ln(merged, w_out[l].astype(BF16), xs, mod_vec(l, 2), row(ln1_w[l]), row(ln1_b[l]),
                         mod_vec(l, 4), mod_vec(l, 3))
        wr = _pad_lanes(jnp.concatenate([router_group_w[l], router_expert_w[l]], axis=1))
        br = row(_pad_lanes(jnp.concatenate([router_group_b[l], router_expert_b[l]])))
        route, counts = _router(h2, wr, br)
        dest, blk_e, n_used, n_slots = _moe_plan(route, counts)
        xsort = _dispatch(h2, dest, n_slots)
        ysort = _experts(l, blk_e, n_used, xsort, exp_w_gate, exp_w_up, exp_w_down)
        nl = min(l + 1, depth - 1)
        xs, h = _combine_ln(dest, ysort, route, xs, mod_vec(l, 5), row(ln2_w[l]), row(ln2_b[l]),
                            mod_vec(nl, 1), mod_vec(nl, 0))
    return xs.reshape(bsz, seq, d)
```

```python
import functools

import jax
import jax.numpy as jnp
from jax import lax
from jax.experimental import pallas as pl
from jax.experimental.pallas import tpu as pltpu

F32 = jnp.float32
BF16 = jnp.bfloat16
I32 = jnp.int32

D_MODEL = 2048
DEPTH = 4
MIX_W = 512
CONV_K = 4
LRU_BLOCKS = 8
LRU_BW = MIX_W // LRU_BLOCKS
LRU_C = 8.0
SSD_HEADS = 8
SSD_P = 64
SSD_G = 2
SSD_HPG = 4
SSD_N = 128
MLSTM_HEADS = 4
MLSTM_DH = 128
HGRN_HEADS = 4
HGRN_DK = 128
N_GROUPS = 4
EXP_PER_GROUP = 8
N_EXPERTS = 32
D_FF = 512
ALPHA = (2.0 * DEPTH) ** 0.25
EPS = 1e-5
NEG_BIG = -1e30

LANES = 128
SUBLANES = 8
VMEM_LIMIT = 56 * 1024 * 1024

SCAN_CHUNK = 128
MIXER_ROWS = 256
HGRN_DIAG = 2
LRU_CHUNK = 256
MOE_BM = 512
ROW_TILE = 512
LN_TILE = 256
DISPATCH_TILE = 512

B_COLS = 3 * MIX_W + LANES
C_COLS = 3 * MIX_W + LANES + MIX_W


def _params(sem, vmem=VMEM_LIMIT):
    return pltpu.CompilerParams(dimension_semantics=sem, vmem_limit_bytes=vmem)


def _dot(a, b):
    return jnp.dot(a, b, preferred_element_type=F32)


def _dot_nt(a, b):
    return lax.dot_general(a, b, (((1,), (1,)), ((), ())), preferred_element_type=F32)


def _sigmoid(x):
    return jax.nn.sigmoid(x)


def _silu(x):
    return x * jax.nn.sigmoid(x)


def _log1p_exp_neg_abs(x):
    return jnp.log(1.0 + jnp.exp(-jnp.abs(x)))


def _softplus(x):
    return jnp.maximum(x, 0.0) + _log1p_exp_neg_abs(x)


def _gelu_tanh(x):
    return 0.5 * x * (1.0 + jnp.tanh(0.7978845608028654 * (x + 0.044715 * (x * x * x))))


def _row_iota(shape):
    return lax.broadcasted_iota(I32, shape, 0)


def _split3(x):
    a = x.astype(BF16)
    r = x - a.astype(F32)
    b = r.astype(BF16)
    return a, b, (r - b.astype(F32)).astype(BF16)


def _cumsum_rows(x):
    n = x.shape[0]
    row = _row_iota(x.shape)
    d = 1
    while d < n:
        x = x + jnp.where(row >= d, pltpu.roll(x, d, 0), 0.0)
        d *= 2
    return x


def _expand_lanes(x, sel):
    a, b, c = _split3(x)
    return _dot(a, sel) + _dot(b, sel) + _dot(c, sel)


def _causal_conv(x, ext_ref, w, b):
    n = x.shape[0]
    ext_ref[SUBLANES:, :] = x
    acc = x * w[CONV_K - 1:CONV_K] + b
    for j in range(1, CONV_K):
        acc = acc + ext_ref[SUBLANES - j:SUBLANES - j + n, :] * w[CONV_K - 1 - j:CONV_K - j]
    ext_ref[0:SUBLANES, :] = x[n - SUBLANES:]
    return acc


def _ada_kernel(c_ref, w_ref, b_ref, o_ref):
    c = c_ref[...]
    o_ref[0] = jnp.dot(_silu(c), w_ref[0], preferred_element_type=F32,
                       precision=lax.Precision.HIGHEST) + b_ref[0]


def _ada_table(c, ada_w, ada_b):
    depth, d, e = ada_w.shape
    tn = 1024
    c8 = jnp.broadcast_to(c.reshape(1, d), (SUBLANES, d))
    out = pl.pallas_call(
        _ada_kernel,
        grid=(depth, e // tn),
        in_specs=[pl.BlockSpec((SUBLANES, d), lambda l, j: (0, 0)),
                  pl.BlockSpec((1, d, tn), lambda l, j: (l, 0, j)),
                  pl.BlockSpec((1, 1, tn), lambda l, j: (l, 0, j))],
        out_specs=pl.BlockSpec((1, SUBLANES, tn), lambda l, j: (l, 0, j)),
        out_shape=jax.ShapeDtypeStruct((depth, SUBLANES, e), F32),
        compiler_params=_params(("parallel", "parallel")),
        name="ada_table",
    )(c8, ada_w, ada_b.reshape(depth, 1, e))
    return out[:, 0, :]


def _lb_kernel(p_ref, o_ref):
    p = p_ref[...]
    depth = p.shape[0]
    rows = [p[l:l + 1] for l in range(depth)]
    m = rows[0]
    for r in rows[1:]:
        m = jnp.maximum(m, r)
    es = [jnp.exp(r - m) for r in rows]
    tot = es[0]
    for e in es[1:]:
        tot = tot + e
    soft = [e / tot for e in es]
    acc = soft[0]
    o_ref[0:1, :] = acc - soft[0]
    for l in range(1, depth):
        acc = acc + soft[l]
        o_ref[l:l + 1, :] = acc - soft[0]


def _lb_table(p):
    return pl.pallas_call(
        _lb_kernel,
        out_shape=jax.ShapeDtypeStruct(p.shape, F32),
        name="hgrn_lb_table",
    )(p)


def _modulate_kernel(x_ref, sc_ref, sh_ref, o_ref):
    o_ref[...] = (x_ref[...] * (1.0 + sc_ref[...]) + sh_ref[...]).astype(o_ref.dtype)


def _modulate(x, scale, shift):
    t, d = x.shape
    tm = ROW_TILE
    return pl.pallas_call(
        _modulate_kernel,
        grid=(t // tm,),
        in_specs=[pl.BlockSpec((tm, d), lambda i: (i, 0)),
                  pl.BlockSpec((1, d), lambda i: (0, 0)),
                  pl.BlockSpec((1, d), lambda i: (0, 0))],
        out_specs=pl.BlockSpec((tm, d), lambda i: (i, 0)),
        out_shape=jax.ShapeDtypeStruct((t, d), BF16),
        compiler_params=_params(("parallel",)),
        name="modulate",
    )(x, scale, shift)


def _proj_kernel(h_ref, w_ref, o_ref):
    o_ref[...] = _dot(h_ref[...], w_ref[...])


def _project(h, w):
    t, d = h.shape
    n = w.shape[1]
    tm = ROW_TILE
    return pl.pallas_call(
        _proj_kernel,
        grid=(t // tm,),
        in_specs=[pl.BlockSpec((tm, d), lambda i: (i, 0)),
                  pl.BlockSpec((d, n), lambda i: (0, 0))],
        out_specs=pl.BlockSpec((tm, n), lambda i: (i, 0)),
        out_shape=jax.ShapeDtypeStruct((t, n), F32),
        compiler_params=_params(("parallel",)),
        name="in_proj",
    )(h, w)


def _rglru_kernel(p_ref, cw_ref, cb_ref, wa_ref, ba_ref, wx_ref, bx_ref, lam_ref, o_ref, tail_ref, h_ref):
    @pl.when(pl.program_id(0) == 0)
    def _():
        tail_ref[...] = jnp.zeros_like(tail_ref)
        h_ref[...] = jnp.zeros_like(h_ref)

    p = p_ref[...]
    n = p.shape[0]
    ax = p[:, :MIX_W]
    ag = p[:, MIX_W:]
    xc = _causal_conv(ax, tail_ref, cw_ref[...], cb_ref[...])
    xb = xc.astype(BF16)
    r = _sigmoid(_dot(xb, wa_ref[...]) + ba_ref[...])
    gi = _sigmoid(_dot(xb, wx_ref[...]) + bx_ref[...])
    log_a = (-LRU_C) * r * _softplus(-lam_ref[...])
    a = jnp.exp(log_a)
    u = jnp.sqrt(-jnp.tanh(log_a) * (a * a + 1.0)) * (gi * xc)
    row = _row_iota(a.shape)
    d = 1
    while d < n:
        keep = row >= d
        a_s = pltpu.roll(a, d, 0)
        u_s = pltpu.roll(u, d, 0)
        u = jnp.where(keep, a * u_s + u, u)
        a = jnp.where(keep, a * a_s, a)
        d *= 2
    h = u + a * h_ref[0:1]
    h_ref[...] = jnp.broadcast_to(h[n - 1:n], h_ref.shape)
    o_ref[...] = (h * _gelu_tanh(ag)).astype(o_ref.dtype)


def _rglru(proj, cw, cb, wa, ba, wx, bx, lam):
    t = proj.shape[0]
    n = LRU_CHUNK
    vec = pl.BlockSpec((1, MIX_W), lambda i: (0, 0))
    sq = pl.BlockSpec((MIX_W, MIX_W), lambda i: (0, 0))
    return pl.pallas_call(
        _rglru_kernel,
        grid=(t // n,),
        in_specs=[pl.BlockSpec((n, 2 * MIX_W), lambda i: (i, 0)),
                  pl.BlockSpec((CONV_K, MIX_W), lambda i: (0, 0)), vec, sq, vec, sq, vec, vec],
        out_specs=pl.BlockSpec((n, MIX_W), lambda i: (i, 0)),
        out_shape=jax.ShapeDtypeStruct((t, MIX_W), BF16),
        scratch_shapes=[pltpu.VMEM((SUBLANES + n, MIX_W), F32), pltpu.VMEM((SUBLANES, MIX_W), F32)],
        compiler_params=_params(("arbitrary",)),
        name="rglru",
    )(proj, cw, cb, wa, ba, wx, bx, lam)


def _chunked(chunk_fn, state_refs, p_ref, o_ref, *refs):
    @pl.when(pl.program_id(0) == 0)
    def _():
        for s in state_refs:
            s[...] = jnp.zeros_like(s)

    for sub in range(p_ref.shape[0] // SCAN_CHUNK):
        rows = pl.ds(sub * SCAN_CHUNK, SCAN_CHUNK)
        chunk_fn(p_ref.at[rows], *refs, o_ref.at[rows], *state_refs)


def _ssd_kernel(p_ref, cw_ref, cb_ref, dtbn_ref, an_ref, dx_ref, nw_ref, sel_ref, o_ref, tail_ref, st_ref):
    _chunked(_ssd_chunk, (tail_ref, st_ref), p_ref, o_ref,
             cw_ref, cb_ref, dtbn_ref, an_ref, dx_ref, nw_ref, sel_ref)


def _ssd_chunk(p_ref, cw_ref, cb_ref, dtbn_ref, an_ref, dx_ref, nw_ref, sel_ref, o_ref, tail_ref, st_ref):
    p = p_ref[...]
    n = p.shape[0]
    gw = SSD_HPG * SSD_P
    z = p[:, 0:MIX_W]
    xbc_raw = p[:, MIX_W:3 * MIX_W]
    dtn_raw = p[:, 3 * MIX_W:3 * MIX_W + LANES]
    xbc = _silu(_causal_conv(xbc_raw, tail_ref, cw_ref[...], cb_ref[...]))
    xv = xbc[:, :MIX_W]
    bm = xbc[:, MIX_W:MIX_W + SSD_G * SSD_N]
    cm = xbc[:, MIX_W + SSD_G * SSD_N:]
    dtn = _softplus(dtn_raw + dtbn_ref[...])
    csn = _cumsum_rows(dtn * an_ref[...])
    csn_t = jnp.transpose(csn)
    sel = sel_ref[...]
    csx = _expand_lanes(csn, sel)
    xdt = xv * _expand_lanes(dtn, sel)
    tril = _row_iota((n, n)) >= lax.broadcasted_iota(I32, (n, n), 1)
    lane_head = lax.shift_right_logical(lax.broadcasted_iota(I32, (n, gw), 1), SSD_P.bit_length() - 1)
    ys = []
    for g in range(SSD_G):
        seg = slice(g * gw, (g + 1) * gw)
        cg = cm[:, g * SSD_N:(g + 1) * SSD_N].astype(BF16)
        bg = bm[:, g * SSD_N:(g + 1) * SSD_N]
        scores = _dot_nt(cg, bg.astype(BF16))
        csg = csx[:, seg]
        xdt_g = xdt[:, seg]
        xdt_gb = xdt_g.astype(BF16)
        st = st_ref[g]
        y = _dot(cg, st.astype(BF16)) * jnp.exp(csg)
        for h in range(SSD_HPG):
            hh = g * SSD_HPG + h
            col = csx[:, hh * SSD_P:hh * SSD_P + 1]
            dec = jnp.exp(jnp.where(tril, col - csn_t[hh:hh + 1, :], NEG_BIG))
            y = y + jnp.where(lane_head == h, _dot((scores * dec).astype(BF16), xdt_gb), 0.0)
        last = csg[n - 1:n]
        xw = (xdt_g * jnp.exp(last - csg)).astype(BF16)
        st_ref[g] = jnp.exp(last) * st + _dot(jnp.transpose(bg).astype(BF16), xw)
        ys.append(y)
    y = jnp.concatenate(ys, axis=1) + dx_ref[...] * xv
    yz = y * _silu(z)
    ms = jnp.mean(yz * yz, axis=1, keepdims=True)
    o_ref[...] = (yz * lax.rsqrt(ms + EPS) * nw_ref[...]).astype(o_ref.dtype)


def _ssd(proj, cw, cb, dtb_n, a_n, d_x, nw):
    t = proj.shape[0]
    n = MIXER_ROWS
    cc = 2 * MIX_W
    vec = lambda w: pl.BlockSpec((1, w), lambda i: (0, 0))
    sel = _lane_selector(SSD_HEADS, SSD_P)
    return pl.pallas_call(
        _ssd_kernel,
        grid=(t // n,),
        in_specs=[pl.BlockSpec((n, B_COLS), lambda i: (i, 0)),
                  pl.BlockSpec((CONV_K, cc), lambda i: (0, 0)), vec(cc),
                  vec(LANES), vec(LANES), vec(MIX_W), vec(MIX_W),
                  pl.BlockSpec(sel.shape, lambda i: (0, 0))],
        out_specs=pl.BlockSpec((n, MIX_W), lambda i: (i, 0)),
        out_shape=jax.ShapeDtypeStruct((t, MIX_W), BF16),
        scratch_shapes=[pltpu.VMEM((SUBLANES + SCAN_CHUNK, cc), F32),
                        pltpu.VMEM((SSD_G, SSD_N, SSD_HPG * SSD_P), F32)],
        compiler_params=_params(("arbitrary",)),
        name="ssd",
    )(proj, cw, cb, dtb_n, a_n, d_x, nw, sel)


def _cummax_rows(x):
    n = x.shape[0]
    row = _row_iota(x.shape)
    d = 1
    while d < n:
        x = jnp.maximum(x, jnp.where(row >= d, pltpu.roll(x, d, 0), NEG_BIG))
        d *= 2
    return x


def _lane_selector(heads, width):
    k = lax.broadcasted_iota(I32, (LANES, heads * width), 0)
    j = lax.broadcasted_iota(I32, (LANES, heads * width), 1)
    return (k == j // width).astype(BF16)


def _mlstm_kernel(p_ref, cw_ref, cb_ref, gb_ref, nw_ref, sel_ref, o_ref, tail_ref, c_ref, n_ref, m_ref):
    _chunked(_mlstm_chunk, (tail_ref, c_ref, n_ref, m_ref), p_ref, o_ref,
             cw_ref, cb_ref, gb_ref, nw_ref, sel_ref)


def _mlstm_chunk(p_ref, cw_ref, cb_ref, gb_ref, nw_ref, sel_ref, o_ref, tail_ref, c_ref, n_ref, m_ref):
    p = p_ref[...]
    n = p.shape[0]
    dh = MLSTM_DH
    qk_raw = p[:, :2 * MIX_W]
    qk = _silu(_causal_conv(qk_raw, tail_ref, cw_ref[...], cb_ref[...]))
    v_all = p[:, 2 * MIX_W:3 * MIX_W]
    pre = p[:, 3 * MIX_W:3 * MIX_W + LANES] + gb_ref[...]
    og_all = _sigmoid(p[:, 3 * MIX_W + LANES:])
    nw_all = nw_ref[...]
    fpre = pltpu.roll(pre, LANES - MLSTM_HEADS, 1)
    cs_n = _cumsum_rows(jnp.minimum(fpre, 0.0) - _log1p_exp_neg_abs(fpre))
    r_n = pre - cs_n
    rt = jnp.transpose(r_n)
    sel = sel_ref[...]
    cs_b = _expand_lanes(cs_n, sel)
    ig_b = _expand_lanes(pre, sel)
    cm_b = _expand_lanes(_cummax_rows(r_n), sel)
    tril = _row_iota((n, n)) >= lax.broadcasted_iota(I32, (n, n), 1)
    m_old = [m_ref[h:h + 1, :] for h in range(MLSTM_HEADS)]
    c_old = [c_ref[h] for h in range(MLSTM_HEADS)]
    n_old = [n_ref[h:h + 1, :] for h in range(MLSTM_HEADS)]
    c_new, n_new, m_news, outs = [], [], [], []
    for h in range(MLSTM_HEADS):
        sl = slice(h * dh, (h + 1) * dh)
        q = qk[:, sl] * (dh ** -0.5)
        k = qk[:, MIX_W + h * dh:MIX_W + (h + 1) * dh]
        v = v_all[:, sl]
        qb = q.astype(BF16)
        vb = v.astype(BF16)
        cs = cs_b[:, sl]
        ig = ig_b[:, sl]
        m_prev = m_old[h]
        inter = cs + m_prev
        m_row = jnp.maximum(cs + cm_b[:, sl], inter)
        dexp = jnp.where(tril, (cs - m_row) + rt[h:h + 1, :], NEG_BIG)
        w = jnp.exp(dexp) * _dot_nt(qb, k.astype(BF16))
        w_inter = jnp.exp(inter - m_row)
        cmem = c_old[h]
        nmem = n_old[h]
        num = _dot(w.astype(BF16), vb) + w_inter * _dot(qb, cmem.astype(BF16))
        den = jnp.sum(w, axis=1, keepdims=True) + w_inter * jnp.sum(q * nmem, axis=1, keepdims=True)
        hs = num / jnp.maximum(jnp.abs(den), jnp.exp(-m_row))
        g_tot = cs[n - 1:n]
        s_end = g_tot - cs + ig
        m_new = jnp.maximum(g_tot + m_prev, jnp.max(s_end, axis=0, keepdims=True))
        kw = k * jnp.exp(s_end - m_new)
        decay = jnp.exp(g_tot + m_prev - m_new)
        c_new.append(decay * cmem + _dot(jnp.transpose(kw).astype(BF16), vb))
        n_new.append(decay * nmem + jnp.sum(kw, axis=0, keepdims=True))
        m_news.append(jnp.broadcast_to(m_new, (1, LANES)))
        ms = jnp.mean(hs * hs, axis=1, keepdims=True)
        outs.append((og_all[:, sl] * (hs * lax.rsqrt(ms + EPS) * nw_all[:, sl])).astype(o_ref.dtype))
    for h in range(MLSTM_HEADS):
        c_ref[h] = c_new[h]
        n_ref[h:h + 1, :] = n_new[h]
        m_ref[h:h + 1, :] = m_news[h]
        o_ref[:, h * dh:(h + 1) * dh] = outs[h]


def _mlstm(proj, cw, cb, gate_bias, nw):
    t = proj.shape[0]
    n = SCAN_CHUNK
    assert n == LANES
    cc = 2 * MIX_W
    vec = lambda w: pl.BlockSpec((1, w), lambda i: (0, 0))
    return pl.pallas_call(
        _mlstm_kernel,
        grid=(t // n,),
        in_specs=[pl.BlockSpec((n, C_COLS), lambda i: (i, 0)),
                  pl.BlockSpec((CONV_K, cc), lambda i: (0, 0)), vec(cc), vec(LANES), vec(MIX_W),
                  pl.BlockSpec((LANES, MLSTM_HEADS * LANES), lambda i: (0, 0))],
        out_specs=pl.BlockSpec((n, MIX_W), lambda i: (i, 0)),
        out_shape=jax.ShapeDtypeStruct((t, MIX_W), BF16),
        scratch_shapes=[pltpu.VMEM((SUBLANES + SCAN_CHUNK, cc), F32),
                        pltpu.VMEM((MLSTM_HEADS, MLSTM_DH, MLSTM_DH), F32),
                        pltpu.VMEM((SUBLANES, MLSTM_DH), F32),
                        pltpu.VMEM((SUBLANES, LANES), F32)],
        compiler_params=_params(("arbitrary",)),
        name="mlstm",
    )(proj, cw, cb, gate_bias, nw, _lane_selector(MLSTM_HEADS, LANES))


def _hgrn_kernel(p_ref, lb_ref, nw_ref, o_ref, st_ref):
    _chunked(_hgrn_chunk, (st_ref,), p_ref, o_ref, lb_ref, nw_ref)


def _hgrn_chunk(p_ref, lb_ref, nw_ref, o_ref, st_ref):
    p = p_ref[...]
    n = p.shape[0]
    dk = HGRN_DK
    q = _silu(p[:, :MIX_W])
    u = p[:, MIX_W:2 * MIX_W]
    v = p[:, 2 * MIX_W:3 * MIX_W]
    gt = p[:, 3 * MIX_W:]
    lb = lb_ref[...]
    f = lb + (1.0 - lb) * _sigmoid(u)
    k = (1.0 - lb) * _sigmoid(-u)
    cs = _cumsum_rows(jnp.log(jnp.maximum(f, 1e-30)))
    last = cs[n - 1:n]
    rowi = _row_iota((n, 1))
    vb = v.astype(BF16)

    scores = [None] * HGRN_HEADS
    rr = _row_iota((n, n))
    cc = lax.broadcasted_iota(I32, (n, n), 1)
    hs = n // 2
    while hs >= HGRN_DIAG:
        blk = 2 * hs
        nb = n // blk
        if blk > SUBLANES:
            ref = jnp.concatenate([jnp.broadcast_to(cs[b * blk + hs - 1:b * blk + hs], (blk, MIX_W))
                                   for b in range(nb)], axis=0)
        else:
            c3 = cs.reshape(n // SUBLANES, SUBLANES, MIX_W)
            sub = lax.broadcasted_iota(I32, c3.shape, 1)
            ref = None
            for j in reversed(range(SUBLANES // blk)):
                rj = jnp.broadcast_to(c3[:, j * blk + hs - 1:j * blk + hs, :], c3.shape)
                ref = rj if ref is None else jnp.where(sub < (j + 1) * blk, rj, ref)
            ref = ref.reshape(n, MIX_W)
        upper = (rowi & (blk - 1)) >= hs
        qj = jnp.where(upper, q * jnp.exp(jnp.minimum(cs - ref, 0.0)), 0.0).astype(BF16)
        kj = jnp.where(upper, 0.0, k * jnp.exp(jnp.minimum(ref - cs, 0.0))).astype(BF16)
        same = (rr & -blk) == (cc & -blk)
        for h in range(HGRN_HEADS):
            sl = slice(h * dk, (h + 1) * dk)
            s = _dot_nt(qj[:, sl], kj[:, sl])
            if nb > 1:
                s = jnp.where(same, s, 0.0)
            scores[h] = s if scores[h] is None else scores[h] + s
        hs //= 2

    band = rr - cc
    for dlt in range(HGRN_DIAG):
        if dlt == 0:
            t_all = q * k
        else:
            kd = pltpu.roll(k, dlt, 0)
            csd = pltpu.roll(cs, dlt, 0)
            t_all = q * kd * jnp.exp(jnp.minimum(cs - csd, 0.0))
        valid = (rowi & (HGRN_DIAG - 1)) >= dlt
        for h in range(HGRN_HEADS):
            sl = slice(h * dk, (h + 1) * dk)
            s = jnp.where(valid, jnp.sum(t_all[:, sl], axis=1, keepdims=True), 0.0)
            scores[h] = scores[h] + jnp.where(band == dlt, s, 0.0)

    qe = (q * jnp.exp(cs)).astype(BF16)
    ys = []
    for h in range(HGRN_HEADS):
        sl = slice(h * dk, (h + 1) * dk)
        ys.append(_dot(scores[h].astype(BF16), vb[:, sl]) + _dot_nt(qe[:, sl], st_ref[h].astype(BF16)))

    kw = k * jnp.exp(last - cs)
    elast = jnp.exp(last)
    for h in range(HGRN_HEADS):
        sl = slice(h * dk, (h + 1) * dk)
        st_ref[h] = elast[:, sl] * st_ref[h] + _dot(jnp.transpose(v[:, sl]).astype(BF16), kw[:, sl].astype(BF16))
        y = ys[h]
        ms = jnp.mean(y * y, axis=1, keepdims=True)
        o_ref[:, sl] = (y * lax.rsqrt(ms + EPS) * nw_ref[:, sl] * _silu(gt[:, sl])).astype(o_ref.dtype)


def _hgrn(proj, lb, nw):
    t = proj.shape[0]
    n = MIXER_ROWS
    vec = pl.BlockSpec((1, MIX_W), lambda i: (0, 0))
    return pl.pallas_call(
        _hgrn_kernel,
        grid=(t // n,),
        in_specs=[pl.BlockSpec((n, 4 * MIX_W), lambda i: (i, 0)), vec, vec],
        out_specs=pl.BlockSpec((n, MIX_W), lambda i: (i, 0)),
        out_shape=jax.ShapeDtypeStruct((t, MIX_W), BF16),
        scratch_shapes=[pltpu.VMEM((HGRN_HEADS, HGRN_DK, HGRN_DK), F32)],
        compiler_params=_params(("arbitrary",)),
        name="hgrn2",
    )(proj, lb, nw)


def _merge_kernel(h_ref, ba_ref, bb_ref, bc_ref, bd_ref, wg_ref, bg_ref, wu_ref, o_ref, wg_s, wu_s):
    @pl.when(pl.program_id(1) == 0)
    def _():
        wg_s[...] = wg_ref[0].astype(BF16)
        wu_s[...] = wu_ref[0].astype(BF16)

    h = h_ref[...]
    acc = None
    for br, b_ref in enumerate((ba_ref, bb_ref, bc_ref, bd_ref)):
        gate = _sigmoid(_dot(h, wg_s[br]) + bg_ref[br])
        term = gate * _dot(b_ref[...], wu_s[br])
        acc = term if acc is None else acc + term
    o_ref[...] = acc.astype(o_ref.dtype)


def _merge(layer, h, branches, wg, bg, wu):
    t, d = h.shape
    tm, tn = 1024, 256
    nb = len(branches)
    bspec = pl.BlockSpec((tm, MIX_W), lambda j, i: (i, 0))
    return pl.pallas_call(
        _merge_kernel,
        grid=(d // tn, t // tm),
        in_specs=[pl.BlockSpec((tm, d), lambda j, i: (i, 0)), bspec, bspec, bspec, bspec,
                  pl.BlockSpec((1, nb, d, tn), lambda j, i: (layer, 0, 0, j)),
                  pl.BlockSpec((nb, 1, tn), lambda j, i: (0, 0, j)),
                  pl.BlockSpec((1, nb, MIX_W, tn), lambda j, i: (layer, 0, 0, j))],
        out_specs=pl.BlockSpec((tm, tn), lambda j, i: (i, j)),
        out_shape=jax.ShapeDtypeStruct((t, d), BF16),
        scratch_shapes=[pltpu.VMEM((nb, d, tn), BF16), pltpu.VMEM((nb, MIX_W, tn), BF16)],
        compiler_params=_params(("parallel", "arbitrary")),
        name="merge",
    )(h, *branches, wg, bg, wu)


def _layer_norm_rows(z, w, b):
    mu = jnp.mean(z, axis=1, keepdims=True)
    zc = z - mu
    var = jnp.mean(zc * zc, axis=1, keepdims=True)
    return zc * lax.rsqrt(var + EPS) * w + b


def _out_ln_kernel(m_ref, w_ref, x_ref, g_ref, lw_ref, lb_ref, sc_ref, sh_ref, x_out, h_out):
    y = _dot(m_ref[...], w_ref[...])
    xn = _layer_norm_rows(ALPHA * x_ref[...] + g_ref[...] * y, lw_ref[...], lb_ref[...])
    x_out[...] = xn
    h_out[...] = xn * (1.0 + sc_ref[...]) + sh_ref[...]


def _out_ln(merged, w_out, x, gate, ln_w, ln_b, scale_next, shift_next):
    t, d = x.shape
    tm = ROW_TILE
    vec = pl.BlockSpec((1, d), lambda i: (0, 0))
    row = pl.BlockSpec((tm, d), lambda i: (i, 0))
    return pl.pallas_call(
        _out_ln_kernel,
        grid=(t // tm,),
        in_specs=[row, pl.BlockSpec((d, d), lambda i: (0, 0)), row, vec, vec, vec, vec, vec],
        out_specs=[row, row],
        out_shape=[jax.ShapeDtypeStruct((t, d), F32), jax.ShapeDtypeStruct((t, d), F32)],
        compiler_params=_params(("parallel",)),
        name="out_proj_ln",
    )(merged, w_out, x, gate, ln_w, ln_b, scale_next, shift_next)


def _router_kernel(h_ref, w_ref, b_ref, o_ref, cnt_ref, run_ref):
    @pl.when(pl.program_id(0) == 0)
    def _():
        run_ref[...] = jnp.zeros_like(run_ref)

    h = h_ref[...]
    n = h.shape[0]
    h_hi = h.astype(BF16)
    h_lo = (h - h_hi.astype(F32)).astype(BF16)
    logits = _dot(h_hi, w_ref[0]) + _dot(h_lo, w_ref[0]) + _dot(h_hi, w_ref[1]) + b_ref[...]
    lane = lax.broadcasted_iota(I32, logits.shape, 1)
    lane_f = lane.astype(F32)
    big = float(LANES)
    gl = jnp.where(lane < N_GROUPS, logits, NEG_BIG)
    ge = jnp.exp(gl - jnp.max(gl, axis=1, keepdims=True))
    gp = ge / jnp.sum(ge, axis=1, keepdims=True)
    p_grp = jnp.max(gp, axis=1, keepdims=True)
    grp = jnp.min(jnp.where(gp == p_grp, lane_f, big), axis=1, keepdims=True)
    lo = N_GROUPS + EXP_PER_GROUP * grp
    el = jnp.where((lane_f >= lo) & (lane_f < lo + EXP_PER_GROUP), logits, NEG_BIG)
    v1 = jnp.max(el, axis=1, keepdims=True)
    i1 = jnp.min(jnp.where(el == v1, lane_f, big), axis=1, keepdims=True)
    el2 = jnp.where(lane_f == i1, NEG_BIG, el)
    v2 = jnp.max(el2, axis=1, keepdims=True)
    i2 = jnp.min(jnp.where(el2 == v2, lane_f, big), axis=1, keepdims=True)
    e2 = jnp.exp(v2 - v1)
    w1 = (1.0 / (1.0 + e2)) * p_grp
    w2 = (e2 / (1.0 + e2)) * p_grp
    oh1 = jnp.where(lane_f == i1, 1.0, 0.0)
    oh2 = jnp.where(lane_f == i2, 1.0, 0.0)
    strict = (_row_iota((n, n)) > lax.broadcasted_iota(I32, (n, n), 1)).astype(BF16)
    run = run_ref[...]
    c1 = _dot(strict, oh1.astype(BF16)) + run[0:1]
    c2 = _dot(strict, oh2.astype(BF16)) + run[1:2]
    r1 = jnp.sum(oh1 * c1, axis=1, keepdims=True)
    r2 = jnp.sum(oh2 * c2, axis=1, keepdims=True)
    run_ref[0:1] = run[0:1] + jnp.sum(oh1, axis=0, keepdims=True)
    run_ref[1:2] = run[1:2] + jnp.sum(oh2, axis=0, keepdims=True)
    cnt_ref[...] = run_ref[...]
    out = jnp.where(lane == 0, i1 - N_GROUPS, 0.0)
    out = jnp.where(lane == 1, i2 - N_GROUPS, out)
    out = jnp.where(lane == 2, w1, out)
    out = jnp.where(lane == 3, w2, out)
    out = jnp.where(lane == 4, r1, out)
    out = jnp.where(lane == 5, r2, out)
    o_ref[...] = out


def _router(h, w, b):
    t, d = h.shape
    tm = LN_TILE
    w_hi = w.astype(BF16)
    return pl.pallas_call(
        _router_kernel,
        grid=(t // tm,),
        in_specs=[pl.BlockSpec((tm, d), lambda i: (i, 0)),
                  pl.BlockSpec((2, d, LANES), lambda i: (0, 0, 0)),
                  pl.BlockSpec((1, LANES), lambda i: (0, 0))],
        out_specs=[pl.BlockSpec((tm, LANES), lambda i: (i, 0)),
                   pl.BlockSpec((SUBLANES, LANES), lambda i: (0, 0))],
        out_shape=[jax.ShapeDtypeStruct((t, LANES), F32), jax.ShapeDtypeStruct((SUBLANES, LANES), F32)],
        scratch_shapes=[pltpu.VMEM((SUBLANES, LANES), F32)],
        compiler_params=_params(("arbitrary",)),
        name="router",
    )(h, jnp.stack([w_hi, (w - w_hi.astype(F32)).astype(BF16)]), b)


def _dispatch_kernel(dest_ref, h_ref, xs_in_ref, xs_ref, sem):
    del xs_in_ref
    nt = h_ref.shape[0]

    def copy(r, j):
        return pltpu.make_async_copy(h_ref.at[pl.ds(r, 1)],
                                     xs_ref.at[pl.ds(dest_ref[0, 0, 2 * r + j], 1)], sem)

    def issue(r, carry):
        copy(r, 0).start()
        copy(r, 1).start()
        return carry

    def drain(r, carry):
        copy(r, 0).wait()
        copy(r, 1).wait()
        return carry

    lax.fori_loop(0, nt, issue, 0, unroll=8)
    lax.fori_loop(0, nt, drain, 0, unroll=8)


def _dispatch(h, dest, n_slots):
    t, d = h.shape
    nt = DISPATCH_TILE
    xs0 = jnp.zeros((n_slots, d), F32)
    return pl.pallas_call(
        _dispatch_kernel,
        grid=(t // nt,),
        in_specs=[pl.BlockSpec((1, 1, 2 * nt), lambda i: (i, 0, 0), memory_space=pltpu.SMEM),
                  pl.BlockSpec((nt, d), lambda i: (i, 0)),
                  pl.BlockSpec(memory_space=pl.ANY)],
        out_specs=pl.BlockSpec(memory_space=pl.ANY),
        out_shape=jax.ShapeDtypeStruct((n_slots, d), F32),
        scratch_shapes=[pltpu.SemaphoreType.DMA(())],
        input_output_aliases={2: 0},
        compiler_params=pltpu.CompilerParams(dimension_semantics=("arbitrary",), has_side_effects=True),
        name="moe_dispatch",
    )(dest.reshape(t // nt, 1, 2 * nt), h, xs0)


def _expert_kernel(be_ref, nu_ref, xs_ref, wg_ref, wu_ref, wd_ref, ys_ref, wg_s, wu_s, wd_s):
    b = pl.program_id(0)
    used = b < nu_ref[0]
    prev = be_ref[jnp.maximum(b - 1, 0)]

    @pl.when(used & ((b == 0) | (be_ref[b] != prev)))
    def _():
        wg_s[...] = wg_ref[0, 0].astype(BF16)
        wu_s[...] = wu_ref[0, 0].astype(BF16)
        wd_s[...] = wd_ref[0, 0].astype(BF16)

    @pl.when(used)
    def _():
        rows = xs_ref[...].astype(BF16)
        hid = _silu(_dot(rows, wg_s[...])) * _dot(rows, wu_s[...])
        ys_ref[...] = _dot(hid.astype(BF16), wd_s[...])

    @pl.when(jnp.logical_not(used))
    def _():
        ys_ref[...] = jnp.zeros_like(ys_ref)


def _experts(layer, blk_e, n_used, xs, w_g, w_u, w_d):
    n_slots, d = xs.shape
    f = w_g.shape[3]
    bm = MOE_BM
    grid_spec = pltpu.PrefetchScalarGridSpec(
        num_scalar_prefetch=2,
        grid=(n_slots // bm,),
        in_specs=[pl.BlockSpec((bm, d), lambda b, be, nu: (b, 0)),
                  pl.BlockSpec((1, 1, d, f), lambda b, be, nu: (layer, be[b], 0, 0)),
                  pl.BlockSpec((1, 1, d, f), lambda b, be, nu: (layer, be[b], 0, 0)),
                  pl.BlockSpec((1, 1, f, d), lambda b, be, nu: (layer, be[b], 0, 0))],
        out_specs=pl.BlockSpec((bm, d), lambda b, be, nu: (b, 0)),
        scratch_shapes=[pltpu.VMEM((d, f), BF16), pltpu.VMEM((d, f), BF16), pltpu.VMEM((f, d), BF16)],
    )
    return pl.pallas_call(
        _expert_kernel,
        grid_spec=grid_spec,
        out_shape=jax.ShapeDtypeStruct((n_slots, d), F32),
        compiler_params=_params(("arbitrary",)),
        name="moe_experts",
    )(blk_e, n_used, xs, w_g, w_u, w_d)


def _combine_ln_kernel(dest_ref, dnext_ref, ys_ref, r_ref, x_ref, g_ref, lw_ref, lb_ref, sc_ref, sh_ref,
                       x_out, h_out, buf, sem):
    nt = buf.shape[2]
    i = pl.program_id(0)
    slot = lax.rem(i, 2)

    def copy(d_ref, s, r, j):
        return pltpu.make_async_copy(ys_ref.at[pl.ds(d_ref[0, 0, 2 * r + j], 1)],
                                     buf.at[s, j, pl.ds(r, 1)], sem.at[s])

    def gather(d_ref, s):
        def issue(r, carry):
            copy(d_ref, s, r, 0).start()
            copy(d_ref, s, r, 1).start()
            return carry
        lax.fori_loop(0, nt, issue, 0, unroll=8)

    @pl.when(i == 0)
    def _():
        gather(dest_ref, 0)

    @pl.when(i + 1 < pl.num_programs(0))
    def _():
        gather(dnext_ref, 1 - slot)

    def drain(r, carry):
        copy(dest_ref, slot, r, 0).wait()
        copy(dest_ref, slot, r, 1).wait()
        return carry

    lax.fori_loop(0, nt, drain, 0, unroll=8)
    rt = r_ref[...]
    y = buf[slot, 0] * rt[:, 2:3] + buf[slot, 1] * rt[:, 3:4]
    xn = _layer_norm_rows(ALPHA * x_ref[...] + g_ref[...] * y, lw_ref[...], lb_ref[...])
    x_out[...] = xn
    h_out[...] = (xn * (1.0 + sc_ref[...]) + sh_ref[...]).astype(h_out.dtype)


def _combine_ln(dest, ys, route, x, gate, ln_w, ln_b, scale_next, shift_next):
    t, d = x.shape
    tm = LN_TILE
    vec = pl.BlockSpec((1, d), lambda i: (0, 0))
    row = pl.BlockSpec((tm, d), lambda i: (i, 0))
    nsteps = t // tm
    dest3 = dest.reshape(nsteps, 1, 2 * tm)
    return pl.pallas_call(
        _combine_ln_kernel,
        grid=(nsteps,),
        in_specs=[pl.BlockSpec((1, 1, 2 * tm), lambda i: (i, 0, 0), memory_space=pltpu.SMEM),
                  pl.BlockSpec((1, 1, 2 * tm), lambda i: (jnp.minimum(i + 1, nsteps - 1), 0, 0),
                               memory_space=pltpu.SMEM),
                  pl.BlockSpec(memory_space=pl.ANY),
                  pl.BlockSpec((tm, LANES), lambda i: (i, 0)),
                  row, vec, vec, vec, vec, vec],
        out_specs=[row, row],
        out_shape=[jax.ShapeDtypeStruct((t, d), F32), jax.ShapeDtypeStruct((t, d), BF16)],
        scratch_shapes=[pltpu.VMEM((2, 2, tm, d), F32), pltpu.SemaphoreType.DMA((2,))],
        compiler_params=_params(("arbitrary",)),
        name="moe_combine_ln",
    )(dest3, dest3, ys, route, x, gate, ln_w, ln_b, scale_next, shift_next)


def _slot_kernel(r_ref, tab_ref, o_ref):
    rt = r_ref[...]
    lane = lax.broadcasted_iota(I32, rt.shape, 1)
    lane_f = lane.astype(F32)
    tab = tab_ref[...]
    d1 = jnp.sum(jnp.where(lane_f == rt[:, 0:1] + N_GROUPS, tab[0:1], 0.0), axis=1, keepdims=True) + rt[:, 4:5]
    d2 = jnp.sum(jnp.where(lane_f == rt[:, 1:2] + N_GROUPS, tab[1:2], 0.0), axis=1, keepdims=True) + rt[:, 5:6]
    o_ref[...] = jnp.where(lane == 0, d1, jnp.where(lane == 1, d2, 0.0)).astype(I32)


def _moe_plan(route, counts):
    bm = MOE_BM
    t = route.shape[0]
    tm = 1024
    n_blocks = (2 * t) // bm + N_EXPERTS
    c1 = counts[0].astype(I32)
    c2 = counts[1].astype(I32)
    pcounts = (c1 + c2 + bm - 1) // bm * bm
    pends = jnp.cumsum(pcounts)
    pstart = pends - pcounts
    tab = jnp.zeros((SUBLANES, LANES), F32).at[0].set(pstart.astype(F32)).at[1].set((pstart + c1).astype(F32))
    slots = pl.pallas_call(
        _slot_kernel,
        grid=(t // tm,),
        in_specs=[pl.BlockSpec((tm, LANES), lambda i: (i, 0)),
                  pl.BlockSpec((SUBLANES, LANES), lambda i: (0, 0))],
        out_specs=pl.BlockSpec((tm, LANES), lambda i: (i, 0)),
        out_shape=jax.ShapeDtypeStruct((t, LANES), I32),
        compiler_params=_params(("parallel",)),
        name="moe_slots",
    )(route, tab)
    dest = slots[:, :2]
    ends = pends[N_GROUPS:N_GROUPS + N_EXPERTS]
    starts_of_blocks = jnp.arange(n_blocks, dtype=I32) * bm
    blk_e = jnp.minimum(jnp.sum((ends[None, :] <= starts_of_blocks[:, None]).astype(I32), axis=1),
                        N_EXPERTS - 1).astype(I32)
    n_used = (pends[-1] // bm).astype(I32).reshape(1)
    return dest, blk_e, n_used, n_blocks * bm


def _block_diag(w):
    nb, bw, _ = w.shape
    eye = jnp.eye(nb, dtype=w.dtype)
    return (eye[:, None, :, None] * w[:, :, None, :]).reshape(nb * bw, nb * bw)


def _pad_lanes(v, width=LANES):
    return jnp.pad(v, [(0, 0)] * (v.ndim - 1) + [(0, width - v.shape[-1])])


def _split_w_in(w):
    o = 0
    parts = []
    for s in (MIX_W, MIX_W, MIX_W, MIX_W, SSD_G * SSD_N, SSD_G * SSD_N, SSD_HEADS, MIX_W, MIX_W, MIX_W,
              MLSTM_HEADS, MLSTM_HEADS, MIX_W, MIX_W, MIX_W, MIX_W, MIX_W):
        parts.append(w[:, o:o + s])
        o += s
    a_x, a_g, b_z, b_x, b_b, b_c, b_dt, c_q, c_k, c_v, c_i, c_f, c_o, d_q, d_f, d_i, d_g = parts
    wa = jnp.concatenate([a_x, a_g], axis=1)
    wb = jnp.concatenate([b_z, b_x, b_b, b_c, _pad_lanes(b_dt)], axis=1)
    wc = jnp.concatenate([c_q, c_k, c_v, _pad_lanes(jnp.concatenate([c_i, c_f], axis=1)), c_o], axis=1)
    wd = jnp.concatenate([d_q, d_f, d_i, d_g], axis=1)
    return [m.astype(BF16) for m in (wa, wb, wc, wd)]


def kernel(x, c, ada_w, ada_b, w_in, lru_conv_w, lru_conv_b, lru_wa, lru_ba, lru_wx, lru_bx, lru_lambda, ssd_conv_w, ssd_conv_b, ssd_dt_bias, ssd_a_log, ssd_d, ssd_norm_w, mlstm_conv_w, mlstm_conv_b, mlstm_i_bias, mlstm_f_bias, mlstm_norm_w, hgrn_lb_param, hgrn_norm_w, w_up, w_gate, b_gate, w_out, ln1_w, ln1_b, router_group_w, router_group_b, router_expert_w, router_expert_b, exp_w_gate, exp_w_up, exp_w_down, ln2_w, ln2_b):
    bsz, seq, d = x.shape
    depth = w_in.shape[0]
    assert bsz == 1 and d == D_MODEL
    xs = x.reshape(seq, d)
    ada = _ada_table(c, ada_w, ada_b)
    lb_all = _lb_table(hgrn_lb_param)
    row = lambda v: v.reshape(1, -1)

    def mod_vec(l, k):
        return ada[l, k * d:(k + 1) * d].reshape(1, d)

    h = _modulate(xs, mod_vec(0, 1), mod_vec(0, 0))
    for l in range(depth):
        wa, wb, wc, wd = _split_w_in(w_in[l])
        br_a = _rglru(_project(h, wa), lru_conv_w[l], row(lru_conv_b[l]),
                      _block_diag(lru_wa[l]).astype(BF16), row(lru_ba[l]),
                      _block_diag(lru_wx[l]).astype(BF16), row(lru_bx[l]), row(lru_lambda[l]))
        a_heads = -jnp.exp(ssd_a_log[l])
        br_b = _ssd(_project(h, wb), ssd_conv_w[l], row(ssd_conv_b[l]),
                    row(_pad_lanes(ssd_dt_bias[l])), row(_pad_lanes(a_heads)),
                    row(jnp.repeat(ssd_d[l], SSD_P)), row(ssd_norm_w[l]))
        gate_bias = _pad_lanes(jnp.concatenate([mlstm_i_bias[l], mlstm_f_bias[l]]))
        br_c = _mlstm(_project(h, wc), mlstm_conv_w[l], row(mlstm_conv_b[l]), row(gate_bias),
                      row(mlstm_norm_w[l]))
        br_d = _hgrn(_project(h, wd), row(lb_all[l]), row(hgrn_norm_w[l]))
        merged = _merge(l, h, (br_a, br_b, br_c, br_d), w_gate, b_gate[l].reshape(4, 1, d), w_up)
        xs, h2 = _out_ln(merged, w_out[l].astype(BF16), xs, mod_vec(l, 2), row(ln1_w[l]), row(ln1_b[l]),
                         mod_vec(l, 4), mod_vec(l, 3))
        wr = _pad_lanes(jnp.concatenate([router_group_w[l], router_expert_w[l]], axis=1))
        br = row(_pad_lanes(jnp.concatenate([router_group_b[l], router_expert_b[l]])))
        route, counts = _router(h2, wr, br)
        dest, blk_e, n_used, n_slots = _moe_plan(route, counts)
        xsort = _dispatch(h2, dest, n_slots)
        ysort = _experts(l, blk_e, n_used, xsort, exp_w_gate, exp_w_up, exp_w_down)
        nl = min(l + 1, depth - 1)
        xs, h = _combine_ln(dest, ysort, route, xs, mod_vec(l, 5), row(ln2_w[l]), row(ln2_b[l]),
                            mod_vec(nl, 1), mod_vec(nl, 0))
    return xs.reshape(bsz, seq, d)
```

```python
import functools

import jax
import jax.numpy as jnp
from jax import lax
from jax.experimental import pallas as pl
from jax.experimental.pallas import tpu as pltpu

F32 = jnp.float32
BF16 = jnp.bfloat16
I32 = jnp.int32

D_MODEL = 2048
DEPTH = 4
MIX_W = 512
CONV_K = 4
LRU_BLOCKS = 8
LRU_BW = MIX_W // LRU_BLOCKS
LRU_C = 8.0
SSD_HEADS = 8
SSD_P = 64
SSD_G = 2
SSD_HPG = 4
SSD_N = 128
MLSTM_HEADS = 4
MLSTM_DH = 128
HGRN_HEADS = 4
HGRN_DK = 128
N_GROUPS = 4
EXP_PER_GROUP = 8
N_EXPERTS = 32
D_FF = 512
ALPHA = (2.0 * DEPTH) ** 0.25
EPS = 1e-5
NEG_BIG = -1e30

LANES = 128
SUBLANES = 8
VMEM_LIMIT = 56 * 1024 * 1024

SCAN_CHUNK = 128
MIXER_ROWS = 256
HGRN_DIAG = 2
LRU_CHUNK = 256
MOE_BM = 256
ROW_TILE = 512
LN_TILE = 256
DISPATCH_TILE = 512

B_COLS = 3 * MIX_W + LANES
C_COLS = 3 * MIX_W + LANES + MIX_W


def _params(sem, vmem=VMEM_LIMIT):
    return pltpu.CompilerParams(dimension_semantics=sem, vmem_limit_bytes=vmem)


def _dot(a, b):
    return jnp.dot(a, b, preferred_element_type=F32)


def _dot_nt(a, b):
    return lax.dot_general(a, b, (((1,), (1,)), ((), ())), preferred_element_type=F32)


def _sigmoid(x):
    return jax.nn.sigmoid(x)


def _silu(x):
    return x * jax.nn.sigmoid(x)


def _log1p_exp_neg_abs(x):
    return jnp.log(1.0 + jnp.exp(-jnp.abs(x)))


def _softplus(x):
    return jnp.maximum(x, 0.0) + _log1p_exp_neg_abs(x)


def _gelu_tanh(x):
    return 0.5 * x * (1.0 + jnp.tanh(0.7978845608028654 * (x + 0.044715 * (x * x * x))))


def _row_iota(shape):
    return lax.broadcasted_iota(I32, shape, 0)


def _split3(x):
    a = x.astype(BF16)
    r = x - a.astype(F32)
    b = r.astype(BF16)
    return a, b, (r - b.astype(F32)).astype(BF16)


def _cumsum_rows(x):
    n = x.shape[0]
    row = _row_iota(x.shape)
    d = 1
    while d < n:
        x = x + jnp.where(row >= d, pltpu.roll(x, d, 0), 0.0)
        d *= 2
    return x


def _expand_lanes(x, sel):
    a, b, c = _split3(x)
    return _dot(a, sel) + _dot(b, sel) + _dot(c, sel)


def _causal_conv(x, ext_ref, w, b):
    n = x.shape[0]
    ext_ref[SUBLANES:, :] = x
    acc = x * w[CONV_K - 1:CONV_K] + b
    for j in range(1, CONV_K):
        acc = acc + ext_ref[SUBLANES - j:SUBLANES - j + n, :] * w[CONV_K - 1 - j:CONV_K - j]
    ext_ref[0:SUBLANES, :] = x[n - SUBLANES:]
    return acc


def _ada_kernel(c_ref, w_ref, b_ref, o_ref):
    c = c_ref[...]
    o_ref[0] = jnp.dot(_silu(c), w_ref[0], preferred_element_type=F32,
                       precision=lax.Precision.HIGHEST) + b_ref[0]


def _ada_table(c, ada_w, ada_b):
    depth, d, e = ada_w.shape
    tn = 2048
    c8 = jnp.broadcast_to(c.reshape(1, d), (SUBLANES, d))
    out = pl.pallas_call(
        _ada_kernel,
        grid=(depth, e // tn),
        in_specs=[pl.BlockSpec((SUBLANES, d), lambda l, j: (0, 0)),
                  pl.BlockSpec((1, d, tn), lambda l, j: (l, 0, j)),
                  pl.BlockSpec((1, 1, tn), lambda l, j: (l, 0, j))],
        out_specs=pl.BlockSpec((1, SUBLANES, tn), lambda l, j: (l, 0, j)),
        out_shape=jax.ShapeDtypeStruct((depth, SUBLANES, e), F32),
        compiler_params=_params(("parallel", "parallel")),
        name="ada_table",
    )(c8, ada_w, ada_b.reshape(depth, 1, e))
    return out[:, 0, :]


def _lb_kernel(p_ref, o_ref):
    p = p_ref[...]
    depth = p.shape[0]
    rows = [p[l:l + 1] for l in range(depth)]
    m = rows[0]
    for r in rows[1:]:
        m = jnp.maximum(m, r)
    es = [jnp.exp(r - m) for r in rows]
    tot = es[0]
    for e in es[1:]:
        tot = tot + e
    soft = [e / tot for e in es]
    acc = soft[0]
    o_ref[0:1, :] = acc - soft[0]
    for l in range(1, depth):
        acc = acc + soft[l]
        o_ref[l:l + 1, :] = acc - soft[0]


def _lb_table(p):
    return pl.pallas_call(
        _lb_kernel,
        out_shape=jax.ShapeDtypeStruct(p.shape, F32),
        name="hgrn_lb_table",
    )(p)


def _modulate_kernel(x_ref, sc_ref, sh_ref, o_ref):
    o_ref[...] = (x_ref[...] * (1.0 + sc_ref[...]) + sh_ref[...]).astype(o_ref.dtype)


def _modulate(x, scale, shift):
    t, d = x.shape
    tm = ROW_TILE
    return pl.pallas_call(
        _modulate_kernel,
        grid=(t // tm,),
        in_specs=[pl.BlockSpec((tm, d), lambda i: (i, 0)),
                  pl.BlockSpec((1, d), lambda i: (0, 0)),
                  pl.BlockSpec((1, d), lambda i: (0, 0))],
        out_specs=pl.BlockSpec((tm, d), lambda i: (i, 0)),
        out_shape=jax.ShapeDtypeStruct((t, d), BF16),
        compiler_params=_params(("parallel",)),
        name="modulate",
    )(x, scale, shift)


def _proj_kernel(h_ref, w_ref, o_ref):
    o_ref[...] = _dot(h_ref[...], w_ref[...])


def _project(h, w):
    t, d = h.shape
    n = w.shape[1]
    tm = ROW_TILE
    return pl.pallas_call(
        _proj_kernel,
        grid=(t // tm,),
        in_specs=[pl.BlockSpec((tm, d), lambda i: (i, 0)),
                  pl.BlockSpec((d, n), lambda i: (0, 0))],
        out_specs=pl.BlockSpec((tm, n), lambda i: (i, 0)),
        out_shape=jax.ShapeDtypeStruct((t, n), F32),
        compiler_params=_params(("parallel",)),
        name="in_proj",
    )(h, w)


def _rglru_kernel(p_ref, cw_ref, cb_ref, wa_ref, ba_ref, wx_ref, bx_ref, lam_ref, o_ref, tail_ref, h_ref):
    @pl.when(pl.program_id(0) == 0)
    def _():
        tail_ref[...] = jnp.zeros_like(tail_ref)
        h_ref[...] = jnp.zeros_like(h_ref)

    p = p_ref[...]
    n = p.shape[0]
    ax = p[:, :MIX_W]
    ag = p[:, MIX_W:]
    xc = _causal_conv(ax, tail_ref, cw_ref[...], cb_ref[...])
    xb = xc.astype(BF16)
    r = _sigmoid(_dot(xb, wa_ref[...]) + ba_ref[...])
    gi = _sigmoid(_dot(xb, wx_ref[...]) + bx_ref[...])
    log_a = (-LRU_C) * r * _softplus(-lam_ref[...])
    a = jnp.exp(log_a)
    u = jnp.sqrt(-jnp.tanh(log_a) * (a * a + 1.0)) * (gi * xc)
    row = _row_iota(a.shape)
    d = 1
    while d < n:
        keep = row >= d
        a_s = pltpu.roll(a, d, 0)
        u_s = pltpu.roll(u, d, 0)
        u = jnp.where(keep, a * u_s + u, u)
        a = jnp.where(keep, a * a_s, a)
        d *= 2
    h = u + a * h_ref[0:1]
    h_ref[...] = jnp.broadcast_to(h[n - 1:n], h_ref.shape)
    o_ref[...] = (h * _gelu_tanh(ag)).astype(o_ref.dtype)


def _rglru(proj, cw, cb, wa, ba, wx, bx, lam):
    t = proj.shape[0]
    n = LRU_CHUNK
    vec = pl.BlockSpec((1, MIX_W), lambda i: (0, 0))
    sq = pl.BlockSpec((MIX_W, MIX_W), lambda i: (0, 0))
    return pl.pallas_call(
        _rglru_kernel,
        grid=(t // n,),
        in_specs=[pl.BlockSpec((n, 2 * MIX_W), lambda i: (i, 0)),
                  pl.BlockSpec((CONV_K, MIX_W), lambda i: (0, 0)), vec, sq, vec, sq, vec, vec],
        out_specs=pl.BlockSpec((n, MIX_W), lambda i: (i, 0)),
        out_shape=jax.ShapeDtypeStruct((t, MIX_W), BF16),
        scratch_shapes=[pltpu.VMEM((SUBLANES + n, MIX_W), F32), pltpu.VMEM((SUBLANES, MIX_W), F32)],
        compiler_params=_params(("arbitrary",)),
        name="rglru",
    )(proj, cw, cb, wa, ba, wx, bx, lam)


def _chunked(chunk_fn, state_refs, p_ref, o_ref, *refs):
    @pl.when(pl.program_id(0) == 0)
    def _():
        for s in state_refs:
            s[...] = jnp.zeros_like(s)

    for sub in range(p_ref.shape[0] // SCAN_CHUNK):
        rows = pl.ds(sub * SCAN_CHUNK, SCAN_CHUNK)
        chunk_fn(p_ref.at[rows], *refs, o_ref.at[rows], *state_refs)


def _ssd_kernel(p_ref, cw_ref, cb_ref, dtbn_ref, an_ref, dx_ref, nw_ref, sel_ref, o_ref, tail_ref, st_ref):
    _chunked(_ssd_chunk, (tail_ref, st_ref), p_ref, o_ref,
             cw_ref, cb_ref, dtbn_ref, an_ref, dx_ref, nw_ref, sel_ref)


def _ssd_chunk(p_ref, cw_ref, cb_ref, dtbn_ref, an_ref, dx_ref, nw_ref, sel_ref, o_ref, tail_ref, st_ref):
    p = p_ref[...]
    n = p.shape[0]
    gw = SSD_HPG * SSD_P
    z = p[:, 0:MIX_W]
    xbc_raw = p[:, MIX_W:3 * MIX_W]
    dtn_raw = p[:, 3 * MIX_W:3 * MIX_W + LANES]
    xbc = _silu(_causal_conv(xbc_raw, tail_ref, cw_ref[...], cb_ref[...]))
    xv = xbc[:, :MIX_W]
    bm = xbc[:, MIX_W:MIX_W + SSD_G * SSD_N]
    cm = xbc[:, MIX_W + SSD_G * SSD_N:]
    dtn = _softplus(dtn_raw + dtbn_ref[...])
    csn = _cumsum_rows(dtn * an_ref[...])
    csn_t = jnp.transpose(csn)
    sel = sel_ref[...]
    csx = _expand_lanes(csn, sel)
    xdt = xv * _expand_lanes(dtn, sel)
    tril = _row_iota((n, n)) >= lax.broadcasted_iota(I32, (n, n), 1)
    lane_head = lax.shift_right_logical(lax.broadcasted_iota(I32, (n, gw), 1), SSD_P.bit_length() - 1)
    ys = []
    for g in range(SSD_G):
        seg = slice(g * gw, (g + 1) * gw)
        cg = cm[:, g * SSD_N:(g + 1) * SSD_N].astype(BF16)
        bg = bm[:, g * SSD_N:(g + 1) * SSD_N]
        scores = _dot_nt(cg, bg.astype(BF16))
        csg = csx[:, seg]
        xdt_g = xdt[:, seg]
        xdt_gb = xdt_g.astype(BF16)
        st = st_ref[g]
        y = _dot(cg, st.astype(BF16)) * jnp.exp(csg)
        for h in range(SSD_HPG):
            hh = g * SSD_HPG + h
            col = csx[:, hh * SSD_P:hh * SSD_P + 1]
            dec = jnp.exp(jnp.where(tril, col - csn_t[hh:hh + 1, :], NEG_BIG))
            y = y + jnp.where(lane_head == h, _dot((scores * dec).astype(BF16), xdt_gb), 0.0)
        last = csg[n - 1:n]
        xw = (xdt_g * jnp.exp(last - csg)).astype(BF16)
        st_ref[g] = jnp.exp(last) * st + _dot(jnp.transpose(bg).astype(BF16), xw)
        ys.append(y)
    y = jnp.concatenate(ys, axis=1) + dx_ref[...] * xv
    yz = y * _silu(z)
    ms = jnp.mean(yz * yz, axis=1, keepdims=True)
    o_ref[...] = (yz * lax.rsqrt(ms + EPS) * nw_ref[...]).astype(o_ref.dtype)


def _ssd(proj, cw, cb, dtb_n, a_n, d_x, nw):
    t = proj.shape[0]
    n = MIXER_ROWS
    cc = 2 * MIX_W
    vec = lambda w: pl.BlockSpec((1, w), lambda i: (0, 0))
    sel = _lane_selector(SSD_HEADS, SSD_P)
    return pl.pallas_call(
        _ssd_kernel,
        grid=(t // n,),
        in_specs=[pl.BlockSpec((n, B_COLS), lambda i: (i, 0)),
                  pl.BlockSpec((CONV_K, cc), lambda i: (0, 0)), vec(cc),
                  vec(LANES), vec(LANES), vec(MIX_W), vec(MIX_W),
                  pl.BlockSpec(sel.shape, lambda i: (0, 0))],
        out_specs=pl.BlockSpec((n, MIX_W), lambda i: (i, 0)),
        out_shape=jax.ShapeDtypeStruct((t, MIX_W), BF16),
        scratch_shapes=[pltpu.VMEM((SUBLANES + SCAN_CHUNK, cc), F32),
                        pltpu.VMEM((SSD_G, SSD_N, SSD_HPG * SSD_P), F32)],
        compiler_params=_params(("arbitrary",)),
        name="ssd",
    )(proj, cw, cb, dtb_n, a_n, d_x, nw, sel)


def _cummax_rows(x):
    n = x.shape[0]
    row = _row_iota(x.shape)
    d = 1
    while d < n:
        x = jnp.maximum(x, jnp.where(row >= d, pltpu.roll(x, d, 0), NEG_BIG))
        d *= 2
    return x


def _lane_selector(heads, width):
    k = lax.broadcasted_iota(I32, (LANES, heads * width), 0)
    j = lax.broadcasted_iota(I32, (LANES, heads * width), 1)
    return (k == j // width).astype(BF16)


def _mlstm_kernel(p_ref, cw_ref, cb_ref, gb_ref, nw_ref, sel_ref, o_ref, tail_ref, c_ref, n_ref, m_ref):
    _chunked(_mlstm_chunk, (tail_ref, c_ref, n_ref, m_ref), p_ref, o_ref,
             cw_ref, cb_ref, gb_ref, nw_ref, sel_ref)


def _mlstm_chunk(p_ref, cw_ref, cb_ref, gb_ref, nw_ref, sel_ref, o_ref, tail_ref, c_ref, n_ref, m_ref):
    p = p_ref[...]
    n = p.shape[0]
    dh = MLSTM_DH
    qk_raw = p[:, :2 * MIX_W]
    qk = _silu(_causal_conv(qk_raw, tail_ref, cw_ref[...], cb_ref[...]))
    v_all = p[:, 2 * MIX_W:3 * MIX_W]
    pre = p[:, 3 * MIX_W:3 * MIX_W + LANES] + gb_ref[...]
    og_all = _sigmoid(p[:, 3 * MIX_W + LANES:])
    nw_all = nw_ref[...]
    fpre = pltpu.roll(pre, LANES - MLSTM_HEADS, 1)
    cs_n = _cumsum_rows(jnp.minimum(fpre, 0.0) - _log1p_exp_neg_abs(fpre))
    r_n = pre - cs_n
    rt = jnp.transpose(r_n)
    sel = sel_ref[...]
    cs_b = _expand_lanes(cs_n, sel)
    ig_b = _expand_lanes(pre, sel)
    cm_b = _expand_lanes(_cummax_rows(r_n), sel)
    tril = _row_iota((n, n)) >= lax.broadcasted_iota(I32, (n, n), 1)
    m_old = [m_ref[h:h + 1, :] for h in range(MLSTM_HEADS)]
    c_old = [c_ref[h] for h in range(MLSTM_HEADS)]
    n_old = [n_ref[h:h + 1, :] for h in range(MLSTM_HEADS)]
    c_new, n_new, m_news, outs = [], [], [], []
    for h in range(MLSTM_HEADS):
        sl = slice(h * dh, (h + 1) * dh)
        q = qk[:, sl] * (dh ** -0.5)
        k = qk[:, MIX_W + h * dh:MIX_W + (h + 1) * dh]
        v = v_all[:, sl]
        qb = q.astype(BF16)
        vb = v.astype(BF16)
        cs = cs_b[:, sl]
        ig = ig_b[:, sl]
        m_prev = m_old[h]
        inter = cs + m_prev
        m_row = jnp.maximum(cs + cm_b[:, sl], inter)
        dexp = jnp.where(tril, (cs - m_row) + rt[h:h + 1, :], NEG_BIG)
        w = jnp.exp(dexp) * _dot_nt(qb, k.astype(BF16))
        w_inter = jnp.exp(inter - m_row)
        cmem = c_old[h]
        nmem = n_old[h]
        num = _dot(w.astype(BF16), vb) + w_inter * _dot(qb, cmem.astype(BF16))
        den = jnp.sum(w, axis=1, keepdims=True) + w_inter * jnp.sum(q * nmem, axis=1, keepdims=True)
        hs = num / jnp.maximum(jnp.abs(den), jnp.exp(-m_row))
        g_tot = cs[n - 1:n]
        s_end = g_tot - cs + ig
        m_new = jnp.maximum(g_tot + m_prev, jnp.max(s_end, axis=0, keepdims=True))
        kw = k * jnp.exp(s_end - m_new)
        decay = jnp.exp(g_tot + m_prev - m_new)
        c_new.append(decay * cmem + _dot(jnp.transpose(kw).astype(BF16), vb))
        n_new.append(decay * nmem + jnp.sum(kw, axis=0, keepdims=True))
        m_news.append(jnp.broadcast_to(m_new, (1, LANES)))
        ms = jnp.mean(hs * hs, axis=1, keepdims=True)
        outs.append((og_all[:, sl] * (hs * lax.rsqrt(ms + EPS) * nw_all[:, sl])).astype(o_ref.dtype))
    for h in range(MLSTM_HEADS):
        c_ref[h] = c_new[h]
        n_ref[h:h + 1, :] = n_new[h]
        m_ref[h:h + 1, :] = m_news[h]
        o_ref[:, h * dh:(h + 1) * dh] = outs[h]


def _mlstm(proj, cw, cb, gate_bias, nw):
    t = proj.shape[0]
    n = SCAN_CHUNK
    assert n == LANES
    cc = 2 * MIX_W
    vec = lambda w: pl.BlockSpec((1, w), lambda i: (0, 0))
    return pl.pallas_call(
        _mlstm_kernel,
        grid=(t // n,),
        in_specs=[pl.BlockSpec((n, C_COLS), lambda i: (i, 0)),
                  pl.BlockSpec((CONV_K, cc), lambda i: (0, 0)), vec(cc), vec(LANES), vec(MIX_W),
                  pl.BlockSpec((LANES, MLSTM_HEADS * LANES), lambda i: (0, 0))],
        out_specs=pl.BlockSpec((n, MIX_W), lambda i: (i, 0)),
        out_shape=jax.ShapeDtypeStruct((t, MIX_W), BF16),
        scratch_shapes=[pltpu.VMEM((SUBLANES + SCAN_CHUNK, cc), F32),
                        pltpu.VMEM((MLSTM_HEADS, MLSTM_DH, MLSTM_DH), F32),
                        pltpu.VMEM((SUBLANES, MLSTM_DH), F32),
                        pltpu.VMEM((SUBLANES, LANES), F32)],
        compiler_params=_params(("arbitrary",)),
        name="mlstm",
    )(proj, cw, cb, gate_bias, nw, _lane_selector(MLSTM_HEADS, LANES))


def _hgrn_kernel(p_ref, lb_ref, nw_ref, o_ref, st_ref):
    _chunked(_hgrn_chunk, (st_ref,), p_ref, o_ref, lb_ref, nw_ref)


def _hgrn_chunk(p_ref, lb_ref, nw_ref, o_ref, st_ref):
    p = p_ref[...]
    n = p.shape[0]
    dk = HGRN_DK
    q = _silu(p[:, :MIX_W])
    u = p[:, MIX_W:2 * MIX_W]
    v = p[:, 2 * MIX_W:3 * MIX_W]
    gt = p[:, 3 * MIX_W:]
    lb = lb_ref[...]
    f = lb + (1.0 - lb) * _sigmoid(u)
    k = (1.0 - lb) * _sigmoid(-u)
    cs = _cumsum_rows(jnp.log(jnp.maximum(f, 1e-30)))
    last = cs[n - 1:n]
    rowi = _row_iota((n, 1))
    vb = v.astype(BF16)

    scores = [None] * HGRN_HEADS
    rr = _row_iota((n, n))
    cc = lax.broadcasted_iota(I32, (n, n), 1)
    hs = n // 2
    while hs >= HGRN_DIAG:
        blk = 2 * hs
        nb = n // blk
        if blk > SUBLANES:
            ref = jnp.concatenate([jnp.broadcast_to(cs[b * blk + hs - 1:b * blk + hs], (blk, MIX_W))
                                   for b in range(nb)], axis=0)
        else:
            c3 = cs.reshape(n // SUBLANES, SUBLANES, MIX_W)
            sub = lax.broadcasted_iota(I32, c3.shape, 1)
            ref = None
            for j in reversed(range(SUBLANES // blk)):
                rj = jnp.broadcast_to(c3[:, j * blk + hs - 1:j * blk + hs, :], c3.shape)
                ref = rj if ref is None else jnp.where(sub < (j + 1) * blk, rj, ref)
            ref = ref.reshape(n, MIX_W)
        upper = (rowi & (blk - 1)) >= hs
        qj = jnp.where(upper, q * jnp.exp(jnp.minimum(cs - ref, 0.0)), 0.0).astype(BF16)
        kj = jnp.where(upper, 0.0, k * jnp.exp(jnp.minimum(ref - cs, 0.0))).astype(BF16)
        same = (rr & -blk) == (cc & -blk)
        for h in range(HGRN_HEADS):
            sl = slice(h * dk, (h + 1) * dk)
            s = _dot_nt(qj[:, sl], kj[:, sl])
            if nb > 1:
                s = jnp.where(same, s, 0.0)
            scores[h] = s if scores[h] is None else scores[h] + s
        hs //= 2

    band = rr - cc
    for dlt in range(HGRN_DIAG):
        if dlt == 0:
            t_all = q * k
        else:
            kd = pltpu.roll(k, dlt, 0)
            csd = pltpu.roll(cs, dlt, 0)
            t_all = q * kd * jnp.exp(jnp.minimum(cs - csd, 0.0))
        valid = (rowi & (HGRN_DIAG - 1)) >= dlt
        for h in range(HGRN_HEADS):
            sl = slice(h * dk, (h + 1) * dk)
            s = jnp.where(valid, jnp.sum(t_all[:, sl], axis=1, keepdims=True), 0.0)
            scores[h] = scores[h] + jnp.where(band == dlt, s, 0.0)

    qe = (q * jnp.exp(cs)).astype(BF16)
    ys = []
    for h in range(HGRN_HEADS):
        sl = slice(h * dk, (h + 1) * dk)
        ys.append(_dot(scores[h].astype(BF16), vb[:, sl]) + _dot_nt(qe[:, sl], st_ref[h].astype(BF16)))

    kw = k * jnp.exp(last - cs)
    elast = jnp.exp(last)
    for h in range(HGRN_HEADS):
        sl = slice(h * dk, (h + 1) * dk)
        st_ref[h] = elast[:, sl] * st_ref[h] + _dot(jnp.transpose(v[:, sl]).astype(BF16), kw[:, sl].astype(BF16))
        y = ys[h]
        ms = jnp.mean(y * y, axis=1, keepdims=True)
        o_ref[:, sl] = (y * lax.rsqrt(ms + EPS) * nw_ref[:, sl] * _silu(gt[:, sl])).astype(o_ref.dtype)


def _hgrn(proj, lb, nw):
    t = proj.shape[0]
    n = MIXER_ROWS
    vec = pl.BlockSpec((1, MIX_W), lambda i: (0, 0))
    return pl.pallas_call(
        _hgrn_kernel,
        grid=(t // n,),
        in_specs=[pl.BlockSpec((n, 4 * MIX_W), lambda i: (i, 0)), vec, vec],
        out_specs=pl.BlockSpec((n, MIX_W), lambda i: (i, 0)),
        out_shape=jax.ShapeDtypeStruct((t, MIX_W), BF16),
        scratch_shapes=[pltpu.VMEM((HGRN_HEADS, HGRN_DK, HGRN_DK), F32)],
        compiler_params=_params(("arbitrary",)),
        name="hgrn2",
    )(proj, lb, nw)


def _merge_kernel(h_ref, ba_ref, bb_ref, bc_ref, bd_ref, wg_ref, bg_ref, wu_ref, o_ref, wg_s, wu_s):
    @pl.when(pl.program_id(1) == 0)
    def _():
        wg_s[...] = wg_ref[0].astype(BF16)
        wu_s[...] = wu_ref[0].astype(BF16)

    h = h_ref[...]
    acc = None
    for br, b_ref in enumerate((ba_ref, bb_ref, bc_ref, bd_ref)):
        gate = _sigmoid(_dot(h, wg_s[br]) + bg_ref[br])
        term = gate * _dot(b_ref[...], wu_s[br])
        acc = term if acc is None else acc + term
    o_ref[...] = acc.astype(o_ref.dtype)


def _merge(layer, h, branches, wg, bg, wu):
    t, d = h.shape
    tm, tn = 1024, 256
    nb = len(branches)
    bspec = pl.BlockSpec((tm, MIX_W), lambda j, i: (i, 0))
    return pl.pallas_call(
        _merge_kernel,
        grid=(d // tn, t // tm),
        in_specs=[pl.BlockSpec((tm, d), lambda j, i: (i, 0)), bspec, bspec, bspec, bspec,
                  pl.BlockSpec((1, nb, d, tn), lambda j, i: (layer, 0, 0, j)),
                  pl.BlockSpec((nb, 1, tn), lambda j, i: (0, 0, j)),
                  pl.BlockSpec((1, nb, MIX_W, tn), lambda j, i: (layer, 0, 0, j))],
        out_specs=pl.BlockSpec((tm, tn), lambda j, i: (i, j)),
        out_shape=jax.ShapeDtypeStruct((t, d), BF16),
        scratch_shapes=[pltpu.VMEM((nb, d, tn), BF16), pltpu.VMEM((nb, MIX_W, tn), BF16)],
        compiler_params=_params(("parallel", "arbitrary")),
        name="merge",
    )(h, *branches, wg, bg, wu)


def _layer_norm_rows(z, w, b):
    mu = jnp.mean(z, axis=1, keepdims=True)
    zc = z - mu
    var = jnp.mean(zc * zc, axis=1, keepdims=True)
    return zc * lax.rsqrt(var + EPS) * w + b


def _pack_bf16_pairs(h):
    half = h.shape[1] // 2
    lo = lax.bitcast_convert_type(h[:, :half].astype(BF16).astype(F32), jnp.uint32)
    hi = lax.bitcast_convert_type(h[:, half:].astype(BF16).astype(F32), jnp.uint32)
    return lax.shift_right_logical(lo, jnp.uint32(16)) | (hi & jnp.uint32(0xFFFF0000))


def _unpack_bf16_pairs(w):
    lo = lax.bitcast_convert_type(lax.shift_left(w, jnp.uint32(16)), F32)
    hi = lax.bitcast_convert_type(w & jnp.uint32(0xFFFF0000), F32)
    return jnp.concatenate([lo, hi], axis=1).astype(BF16)


def _out_ln_route_kernel(m_ref, w_ref, x_ref, g_ref, lw_ref, lb_ref, sc_ref, sh_ref, rw_ref, rb_ref,
                         x_out, hp_out, r_out, cnt_out, run_ref):
    @pl.when(pl.program_id(0) == 0)
    def _():
        run_ref[...] = jnp.zeros_like(run_ref)

    y = _dot(m_ref[...], w_ref[...])
    xn = _layer_norm_rows(ALPHA * x_ref[...] + g_ref[...] * y, lw_ref[...], lb_ref[...])
    x_out[...] = xn
    h = xn * (1.0 + sc_ref[...]) + sh_ref[...]
    hp_out[...] = _pack_bf16_pairs(h)
    route, new1, new2 = _route_rows(h, rw_ref, rb_ref, run_ref[...])
    r_out[...] = route
    run_ref[0:1] = new1
    run_ref[1:2] = new2
    cnt_out[...] = run_ref[...]


def _out_ln_route(merged, w_out, x, gate, ln_w, ln_b, scale_next, shift_next, rw, rb):
    t, d = x.shape
    tm = LN_TILE
    vec = pl.BlockSpec((1, d), lambda i: (0, 0))
    row = pl.BlockSpec((tm, d), lambda i: (i, 0))
    rw_hi = rw.astype(BF16)
    return pl.pallas_call(
        _out_ln_route_kernel,
        grid=(t // tm,),
        in_specs=[row, pl.BlockSpec((d, d), lambda i: (0, 0)), row, vec, vec, vec, vec, vec,
                  pl.BlockSpec((2, d, LANES), lambda i: (0, 0, 0)),
                  pl.BlockSpec((1, LANES), lambda i: (0, 0))],
        out_specs=[row, pl.BlockSpec((tm, d // 2), lambda i: (i, 0)),
                   pl.BlockSpec((tm, LANES), lambda i: (i, 0)),
                   pl.BlockSpec((SUBLANES, LANES), lambda i: (0, 0))],
        out_shape=[jax.ShapeDtypeStruct((t, d), F32), jax.ShapeDtypeStruct((t, d // 2), jnp.uint32),
                   jax.ShapeDtypeStruct((t, LANES), F32), jax.ShapeDtypeStruct((SUBLANES, LANES), F32)],
        scratch_shapes=[pltpu.VMEM((SUBLANES, LANES), F32)],
        compiler_params=_params(("arbitrary",)),
        name="out_proj_ln_route",
    )(merged, w_out, x, gate, ln_w, ln_b, scale_next, shift_next,
      jnp.stack([rw_hi, (rw - rw_hi.astype(F32)).astype(BF16)]), rb)


def _route_rows(h, w_ref, b_ref, run):
    n = h.shape[0]
    h_hi = h.astype(BF16)
    h_lo = (h - h_hi.astype(F32)).astype(BF16)
    logits = _dot(h_hi, w_ref[0]) + _dot(h_lo, w_ref[0]) + _dot(h_hi, w_ref[1]) + b_ref[...]
    lane = lax.broadcasted_iota(I32, logits.shape, 1)
    lane_f = lane.astype(F32)
    big = float(LANES)
    gl = jnp.where(lane < N_GROUPS, logits, NEG_BIG)
    ge = jnp.exp(gl - jnp.max(gl, axis=1, keepdims=True))
    gp = ge / jnp.sum(ge, axis=1, keepdims=True)
    p_grp = jnp.max(gp, axis=1, keepdims=True)
    grp = jnp.min(jnp.where(gp == p_grp, lane_f, big), axis=1, keepdims=True)
    lo = N_GROUPS + EXP_PER_GROUP * grp
    el = jnp.where((lane_f >= lo) & (lane_f < lo + EXP_PER_GROUP), logits, NEG_BIG)
    v1 = jnp.max(el, axis=1, keepdims=True)
    i1 = jnp.min(jnp.where(el == v1, lane_f, big), axis=1, keepdims=True)
    el2 = jnp.where(lane_f == i1, NEG_BIG, el)
    v2 = jnp.max(el2, axis=1, keepdims=True)
    i2 = jnp.min(jnp.where(el2 == v2, lane_f, big), axis=1, keepdims=True)
    e2 = jnp.exp(v2 - v1)
    w1 = (1.0 / (1.0 + e2)) * p_grp
    w2 = (e2 / (1.0 + e2)) * p_grp
    oh1 = jnp.where(lane_f == i1, 1.0, 0.0)
    oh2 = jnp.where(lane_f == i2, 1.0, 0.0)
    strict = (_row_iota((n, n)) > lax.broadcasted_iota(I32, (n, n), 1)).astype(BF16)
    c1 = _dot(strict, oh1.astype(BF16)) + run[0:1]
    c2 = _dot(strict, oh2.astype(BF16)) + run[1:2]
    r1 = jnp.sum(oh1 * c1, axis=1, keepdims=True)
    r2 = jnp.sum(oh2 * c2, axis=1, keepdims=True)
    new1 = run[0:1] + jnp.sum(oh1, axis=0, keepdims=True)
    new2 = run[1:2] + jnp.sum(oh2, axis=0, keepdims=True)
    out = jnp.where(lane == 0, i1 - N_GROUPS, 0.0)
    out = jnp.where(lane == 1, i2 - N_GROUPS, out)
    out = jnp.where(lane == 2, w1, out)
    out = jnp.where(lane == 3, w2, out)
    out = jnp.where(lane == 4, r1, out)
    out = jnp.where(lane == 5, r2, out)
    return out, new1, new2


def _dispatch_kernel(poff_ref, plen_ref, nu_ref, dest_ref, h_ref, xs_ref, zbuf, sem, zsem):
    nt = h_ref.shape[0]
    bm = zbuf.shape[0]

    @pl.when(pl.program_id(0) == 0)
    def _():
        zbuf[...] = jnp.zeros_like(zbuf)

        def zcopy(e, r):
            return pltpu.make_async_copy(zbuf.at[pl.ds(0, 1)], xs_ref.at[pl.ds(poff_ref[e] + r, 1)], zsem)

        def zblock(b):
            return pltpu.make_async_copy(zbuf, xs_ref.at[pl.ds(pl.multiple_of(b * bm, bm), bm)], zsem)

        def start_block(b, carry):
            zblock(b).start()
            return carry

        def wait_block(b, carry):
            zblock(b).wait()
            return carry

        n_blocks = xs_ref.shape[0] // bm
        lax.fori_loop(nu_ref[0], n_blocks, start_block, 0)
        lax.fori_loop(nu_ref[0], n_blocks, wait_block, 0)

        def start_rows(e, carry):
            def one(r, c):
                zcopy(e, r).start()
                return c
            return lax.fori_loop(0, plen_ref[e], one, carry)

        def wait_rows(e, carry):
            def one(r, c):
                zcopy(e, r).wait()
                return c
            return lax.fori_loop(0, plen_ref[e], one, carry)

        lax.fori_loop(0, N_EXPERTS, start_rows, 0)
        lax.fori_loop(0, N_EXPERTS, wait_rows, 0)

    def copy(r, j):
        return pltpu.make_async_copy(h_ref.at[pl.ds(r, 1)],
                                     xs_ref.at[pl.ds(dest_ref[0, 0, 2 * r + j], 1)], sem)

    def issue(r, carry):
        copy(r, 0).start()
        copy(r, 1).start()
        return carry

    def drain(r, carry):
        copy(r, 0).wait()
        copy(r, 1).wait()
        return carry

    lax.fori_loop(0, nt, issue, 0, unroll=8)
    lax.fori_loop(0, nt, drain, 0, unroll=8)


def _dispatch(h, dest, pad_off, pad_len, n_used, n_slots):
    t, w = h.shape
    nt = DISPATCH_TILE
    grid_spec = pltpu.PrefetchScalarGridSpec(
        num_scalar_prefetch=3,
        grid=(t // nt,),
        in_specs=[pl.BlockSpec((1, 1, 2 * nt), lambda i, po, pn, nu: (i, 0, 0), memory_space=pltpu.SMEM),
                  pl.BlockSpec((nt, w), lambda i, po, pn, nu: (i, 0))],
        out_specs=pl.BlockSpec(memory_space=pl.ANY),
        scratch_shapes=[pltpu.VMEM((MOE_BM, w), h.dtype), pltpu.SemaphoreType.DMA(()),
                        pltpu.SemaphoreType.DMA(())],
    )
    return pl.pallas_call(
        _dispatch_kernel,
        grid_spec=grid_spec,
        out_shape=jax.ShapeDtypeStruct((n_slots, w), h.dtype),
        compiler_params=pltpu.CompilerParams(dimension_semantics=("arbitrary",)),
        name="moe_dispatch",
    )(pad_off, pad_len, n_used, dest.reshape(t // nt, 1, 2 * nt), h)


def _expert_kernel(be_ref, nu_ref, xs_ref, wg_ref, wu_ref, wd_ref, ys_ref, wg_s, wu_s, wd_s):
    b = pl.program_id(0)
    used = b < nu_ref[0]
    prev = be_ref[jnp.maximum(b - 1, 0)]

    @pl.when(used & ((b == 0) | (be_ref[b] != prev)))
    def _():
        wg_s[...] = wg_ref[0, 0].astype(BF16)
        wu_s[...] = wu_ref[0, 0].astype(BF16)
        wd_s[...] = wd_ref[0, 0].astype(BF16)

    @pl.when(used)
    def _():
        rows = _unpack_bf16_pairs(xs_ref[...])
        hid = _silu(_dot(rows, wg_s[...])) * _dot(rows, wu_s[...])
        ys_ref[...] = _dot(hid.astype(BF16), wd_s[...])

    @pl.when(jnp.logical_not(used))
    def _():
        ys_ref[...] = jnp.zeros_like(ys_ref)


def _experts(layer, blk_e, n_used, xs, w_g, w_u, w_d):
    n_slots = xs.shape[0]
    d, f = w_g.shape[2], w_g.shape[3]
    bm = MOE_BM
    grid_spec = pltpu.PrefetchScalarGridSpec(
        num_scalar_prefetch=2,
        grid=(n_slots // bm,),
        in_specs=[pl.BlockSpec((bm, d // 2), lambda b, be, nu: (jnp.minimum(b, nu[0] - 1), 0)),
                  pl.BlockSpec((1, 1, d, f), lambda b, be, nu: (layer, be[b], 0, 0)),
                  pl.BlockSpec((1, 1, d, f), lambda b, be, nu: (layer, be[b], 0, 0)),
                  pl.BlockSpec((1, 1, f, d), lambda b, be, nu: (layer, be[b], 0, 0))],
        out_specs=pl.BlockSpec((bm, d), lambda b, be, nu: (b, 0)),
        scratch_shapes=[pltpu.VMEM((d, f), BF16), pltpu.VMEM((d, f), BF16), pltpu.VMEM((f, d), BF16)],
    )
    return pl.pallas_call(
        _expert_kernel,
        grid_spec=grid_spec,
        out_shape=jax.ShapeDtypeStruct((n_slots, d), F32),
        compiler_params=_params(("arbitrary",)),
        name="moe_experts",
    )(blk_e, n_used, xs, w_g, w_u, w_d)


def _combine_ln_kernel(dest_ref, dnext_ref, ys_ref, r_ref, x_ref, g_ref, lw_ref, lb_ref, sc_ref, sh_ref,
                       x_out, h_out, buf, sem):
    nt = buf.shape[2]
    i = pl.program_id(0)
    slot = lax.rem(i, 2)

    def copy(d_ref, s, r, j):
        return pltpu.make_async_copy(ys_ref.at[pl.ds(d_ref[0, 0, 2 * r + j], 1)],
                                     buf.at[s, j, pl.ds(r, 1)], sem.at[s])

    def gather(d_ref, s):
        def issue(r, carry):
            copy(d_ref, s, r, 0).start()
            copy(d_ref, s, r, 1).start()
            return carry
        lax.fori_loop(0, nt, issue, 0, unroll=8)

    @pl.when(i == 0)
    def _():
        gather(dest_ref, 0)

    @pl.when(i + 1 < pl.num_programs(0))
    def _():
        gather(dnext_ref, 1 - slot)

    def drain(r, carry):
        copy(dest_ref, slot, r, 0).wait()
        copy(dest_ref, slot, r, 1).wait()
        return carry

    lax.fori_loop(0, nt, drain, 0, unroll=8)
    rt = r_ref[...]
    y = buf[slot, 0] * rt[:, 2:3] + buf[slot, 1] * rt[:, 3:4]
    xn = _layer_norm_rows(ALPHA * x_ref[...] + g_ref[...] * y, lw_ref[...], lb_ref[...])
    x_out[...] = xn
    h_out[...] = (xn * (1.0 + sc_ref[...]) + sh_ref[...]).astype(h_out.dtype)


def _combine_ln(dest, ys, route, x, gate, ln_w, ln_b, scale_next, shift_next):
    t, d = x.shape
    tm = LN_TILE
    vec = pl.BlockSpec((1, d), lambda i: (0, 0))
    row = pl.BlockSpec((tm, d), lambda i: (i, 0))
    nsteps = t // tm
    dest3 = dest.reshape(nsteps, 1, 2 * tm)
    return pl.pallas_call(
        _combine_ln_kernel,
        grid=(nsteps,),
        in_specs=[pl.BlockSpec((1, 1, 2 * tm), lambda i: (i, 0, 0), memory_space=pltpu.SMEM),
                  pl.BlockSpec((1, 1, 2 * tm), lambda i: (jnp.minimum(i + 1, nsteps - 1), 0, 0),
                               memory_space=pltpu.SMEM),
                  pl.BlockSpec(memory_space=pl.ANY),
                  pl.BlockSpec((tm, LANES), lambda i: (i, 0)),
                  row, vec, vec, vec, vec, vec],
        out_specs=[row, row],
        out_shape=[jax.ShapeDtypeStruct((t, d), F32), jax.ShapeDtypeStruct((t, d), BF16)],
        scratch_shapes=[pltpu.VMEM((2, 2, tm, d), F32), pltpu.SemaphoreType.DMA((2,))],
        compiler_params=_params(("arbitrary",)),
        name="moe_combine_ln",
    )(dest3, dest3, ys, route, x, gate, ln_w, ln_b, scale_next, shift_next)


def _slot_kernel(r_ref, tab_ref, o_ref):
    rt = r_ref[...]
    lane = lax.broadcasted_iota(I32, rt.shape, 1)
    lane_f = lane.astype(F32)
    tab = tab_ref[...]
    d1 = jnp.sum(jnp.where(lane_f == rt[:, 0:1] + N_GROUPS, tab[0:1], 0.0), axis=1, keepdims=True) + rt[:, 4:5]
    d2 = jnp.sum(jnp.where(lane_f == rt[:, 1:2] + N_GROUPS, tab[1:2], 0.0), axis=1, keepdims=True) + rt[:, 5:6]
    o_ref[...] = jnp.where(lane == 0, d1, jnp.where(lane == 1, d2, 0.0)).astype(I32)


def _moe_plan(route, counts):
    bm = MOE_BM
    t = route.shape[0]
    tm = 1024
    n_blocks = (2 * t) // bm + N_EXPERTS
    c1 = counts[0].astype(I32)
    c2 = counts[1].astype(I32)
    pcounts = (c1 + c2 + bm - 1) // bm * bm
    pends = jnp.cumsum(pcounts)
    pstart = pends - pcounts
    tab = jnp.zeros((SUBLANES, LANES), F32).at[0].set(pstart.astype(F32)).at[1].set((pstart + c1).astype(F32))
    slots = pl.pallas_call(
        _slot_kernel,
        grid=(t // tm,),
        in_specs=[pl.BlockSpec((tm, LANES), lambda i: (i, 0)),
                  pl.BlockSpec((SUBLANES, LANES), lambda i: (0, 0))],
        out_specs=pl.BlockSpec((tm, LANES), lambda i: (i, 0)),
        out_shape=jax.ShapeDtypeStruct((t, LANES), I32),
        compiler_params=_params(("parallel",)),
        name="moe_slots",
    )(route, tab)
    dest = slots[:, :2]
    ends = pends[N_GROUPS:N_GROUPS + N_EXPERTS]
    starts_of_blocks = jnp.arange(n_blocks, dtype=I32) * bm
    blk_e = jnp.minimum(jnp.sum((ends[None, :] <= starts_of_blocks[:, None]).astype(I32), axis=1),
                        N_EXPERTS - 1).astype(I32)
    n_used = (pends[-1] // bm).astype(I32).reshape(1)
    pad_off = (pstart + c1 + c2)[N_GROUPS:N_GROUPS + N_EXPERTS]
    pad_len = (pcounts - c1 - c2)[N_GROUPS:N_GROUPS + N_EXPERTS]
    return dest, blk_e, n_used, pad_off, pad_len, n_blocks * bm


def _block_diag(w):
    nb, bw, _ = w.shape
    eye = jnp.eye(nb, dtype=w.dtype)
    return (eye[:, None, :, None] * w[:, :, None, :]).reshape(nb * bw, nb * bw)


def _pad_lanes(v, width=LANES):
    return jnp.pad(v, [(0, 0)] * (v.ndim - 1) + [(0, width - v.shape[-1])])


def _split_w_in(w):
    o = 0
    parts = []
    for s in (MIX_W, MIX_W, MIX_W, MIX_W, SSD_G * SSD_N, SSD_G * SSD_N, SSD_HEADS, MIX_W, MIX_W, MIX_W,
              MLSTM_HEADS, MLSTM_HEADS, MIX_W, MIX_W, MIX_W, MIX_W, MIX_W):
        parts.append(w[:, o:o + s])
        o += s
    a_x, a_g, b_z, b_x, b_b, b_c, b_dt, c_q, c_k, c_v, c_i, c_f, c_o, d_q, d_f, d_i, d_g = parts
    wa = jnp.concatenate([a_x, a_g], axis=1)
    wb = jnp.concatenate([b_z, b_x, b_b, b_c, _pad_lanes(b_dt)], axis=1)
    wc = jnp.concatenate([c_q, c_k, c_v, _pad_lanes(jnp.concatenate([c_i, c_f], axis=1)), c_o], axis=1)
    wd = jnp.concatenate([d_q, d_f, d_i, d_g], axis=1)
    return [m.astype(BF16) for m in (wa, wb, wc, wd)]


def kernel(x, c, ada_w, ada_b, w_in, lru_conv_w, lru_conv_b, lru_wa, lru_ba, lru_wx, lru_bx, lru_lambda, ssd_conv_w, ssd_conv_b, ssd_dt_bias, ssd_a_log, ssd_d, ssd_norm_w, mlstm_conv_w, mlstm_conv_b, mlstm_i_bias, mlstm_f_bias, mlstm_norm_w, hgrn_lb_param, hgrn_norm_w, w_up, w_gate, b_gate, w_out, ln1_w, ln1_b, router_group_w, router_group_b, router_expert_w, router_expert_b, exp_w_gate, exp_w_up, exp_w_down, ln2_w, ln2_b):
    bsz, seq, d = x.shape
    depth = w_in.shape[0]
    assert bsz == 1 and d == D_MODEL
    xs = x.reshape(seq, d)
    ada = _ada_table(c, ada_w, ada_b)
    lb_all = _lb_table(hgrn_lb_param)
    row = lambda v: v.reshape(1, -1)

    def mod_vec(l, k):
        return ada[l, k * d:(k + 1) * d].reshape(1, d)

    h = _modulate(xs, mod_vec(0, 1), mod_vec(0, 0))
    for l in range(depth):
        wa, wb, wc, wd = _split_w_in(w_in[l])
        br_a = _rglru(_project(h, wa), lru_conv_w[l], row(lru_conv_b[l]),
                      _block_diag(lru_wa[l]).astype(BF16), row(lru_ba[l]),
                      _block_diag(lru_wx[l]).astype(BF16), row(lru_bx[l]), row(lru_lambda[l]))
        a_heads = -jnp.exp(ssd_a_log[l])
        br_b = _ssd(_project(h, wb), ssd_conv_w[l], row(ssd_conv_b[l]),
                    row(_pad_lanes(ssd_dt_bias[l])), row(_pad_lanes(a_heads)),
                    row(jnp.repeat(ssd_d[l], SSD_P)), row(ssd_norm_w[l]))
        gate_bias = _pad_lanes(jnp.concatenate([mlstm_i_bias[l], mlstm_f_bias[l]]))
        br_c = _mlstm(_project(h, wc), mlstm_conv_w[l], row(mlstm_conv_b[l]), row(gate_bias),
                      row(mlstm_norm_w[l]))
        br_d = _hgrn(_project(h, wd), row(lb_all[l]), row(hgrn_norm_w[l]))
        merged = _merge(l, h, (br_a, br_b, br_c, br_d), w_gate, b_gate[l].reshape(4, 1, d), w_up)
        wr = _pad_lanes(jnp.concatenate([router_group_w[l], router_expert_w[l]], axis=1))
        br = row(_pad_lanes(jnp.concatenate([router_group_b[l], router_expert_b[l]])))
        xs, h2p, route, counts = _out_ln_route(merged, w_out[l].astype(BF16), xs, mod_vec(l, 2), row(ln1_w[l]),
                                               row(ln1_b[l]), mod_vec(l, 4), mod_vec(l, 3), wr, br)
        dest, blk_e, n_used, pad_off, pad_len, n_slots = _moe_plan(route, counts)
        xsort = _dispatch(h2p, dest, pad_off, pad_len, n_used, n_slots)
        ysort = _experts(l, blk_e, n_used, xsort, exp_w_gate, exp_w_up, exp_w_down)
        nl = min(l + 1, depth - 1)
        xs, h = _combine_ln(dest, ysort, route, xs, mod_vec(l, 5), row(ln2_w[l]), row(ln2_b[l]),
                            mod_vec(nl, 1), mod_vec(nl, 0))
    return xs.reshape(bsz, seq, d)
```

```python
import functools

import jax
import jax.numpy as jnp
from jax import lax
from jax.experimental import pallas as pl
from jax.experimental.pallas import tpu as pltpu

F32 = jnp.float32
BF16 = jnp.bfloat16
I32 = jnp.int32

D_MODEL = 2048
DEPTH = 4
MIX_W = 512
CONV_K = 4
LRU_BLOCKS = 8
LRU_BW = MIX_W // LRU_BLOCKS
LRU_C = 8.0
SSD_HEADS = 8
SSD_P = 64
SSD_G = 2
SSD_HPG = 4
SSD_N = 128
MLSTM_HEADS = 4
MLSTM_DH = 128
HGRN_HEADS = 4
HGRN_DK = 128
N_GROUPS = 4
EXP_PER_GROUP = 8
N_EXPERTS = 32
D_FF = 512
ALPHA = (2.0 * DEPTH) ** 0.25
EPS = 1e-5
NEG_BIG = -1e30

LANES = 128
SUBLANES = 8
VMEM_LIMIT = 56 * 1024 * 1024

SCAN_CHUNK = 128
MIXER_ROWS = 256
HGRN_DIAG = 2
LRU_CHUNK = 256
MOE_BM = 256
ROW_TILE = 512
LN_TILE = 256
DISPATCH_TILE = 512

B_COLS = 3 * MIX_W + LANES
C_COLS = 3 * MIX_W + LANES + MIX_W


def _params(sem, vmem=VMEM_LIMIT):
    return pltpu.CompilerParams(dimension_semantics=sem, vmem_limit_bytes=vmem)


def _dot(a, b):
    return jnp.dot(a, b, preferred_element_type=F32)


def _dot_nt(a, b):
    return lax.dot_general(a, b, (((1,), (1,)), ((), ())), preferred_element_type=F32)


def _sigmoid(x):
    return jax.nn.sigmoid(x)


def _silu(x):
    return x * jax.nn.sigmoid(x)


def _log1p_exp_neg_abs(x):
    return jnp.log(1.0 + jnp.exp(-jnp.abs(x)))


def _softplus(x):
    return jnp.maximum(x, 0.0) + _log1p_exp_neg_abs(x)


def _gelu_tanh(x):
    return 0.5 * x * (1.0 + jnp.tanh(0.7978845608028654 * (x + 0.044715 * (x * x * x))))


def _row_iota(shape):
    return lax.broadcasted_iota(I32, shape, 0)


def _split3(x):
    a = x.astype(BF16)
    r = x - a.astype(F32)
    b = r.astype(BF16)
    return a, b, (r - b.astype(F32)).astype(BF16)


def _cumsum_rows(x):
    n = x.shape[0]
    row = _row_iota(x.shape)
    d = 1
    while d < n:
        x = x + jnp.where(row >= d, pltpu.roll(x, d, 0), 0.0)
        d *= 2
    return x


def _expand_lanes(x, sel):
    a, b, c = _split3(x)
    return _dot(a, sel) + _dot(b, sel) + _dot(c, sel)


def _causal_conv(x, ext_ref, w, b):
    n = x.shape[0]
    ext_ref[SUBLANES:, :] = x
    acc = x * w[CONV_K - 1:CONV_K] + b
    for j in range(1, CONV_K):
        acc = acc + ext_ref[SUBLANES - j:SUBLANES - j + n, :] * w[CONV_K - 1 - j:CONV_K - j]
    ext_ref[0:SUBLANES, :] = x[n - SUBLANES:]
    return acc


def _ada_kernel(c_ref, w_ref, b_ref, o_ref):
    c = c_ref[...]
    o_ref[0] = jnp.dot(_silu(c), w_ref[0], preferred_element_type=F32,
                       precision=lax.Precision.HIGHEST) + b_ref[0]


def _ada_table(c, ada_w, ada_b):
    depth, d, e = ada_w.shape
    tn = 2048
    c8 = jnp.broadcast_to(c.reshape(1, d), (SUBLANES, d))
    out = pl.pallas_call(
        _ada_kernel,
        grid=(depth, e // tn),
        in_specs=[pl.BlockSpec((SUBLANES, d), lambda l, j: (0, 0)),
                  pl.BlockSpec((1, d, tn), lambda l, j: (l, 0, j)),
                  pl.BlockSpec((1, 1, tn), lambda l, j: (l, 0, j))],
        out_specs=pl.BlockSpec((1, SUBLANES, tn), lambda l, j: (l, 0, j)),
        out_shape=jax.ShapeDtypeStruct((depth, SUBLANES, e), F32),
        compiler_params=_params(("parallel", "parallel")),
        name="ada_table",
    )(c8, ada_w, ada_b.reshape(depth, 1, e))
    return out[:, 0, :]


def _lb_kernel(p_ref, o_ref):
    p = p_ref[...]
    depth = p.shape[0]
    rows = [p[l:l + 1] for l in range(depth)]
    m = rows[0]
    for r in rows[1:]:
        m = jnp.maximum(m, r)
    es = [jnp.exp(r - m) for r in rows]
    tot = es[0]
    for e in es[1:]:
        tot = tot + e
    soft = [e / tot for e in es]
    acc = soft[0]
    o_ref[0:1, :] = acc - soft[0]
    for l in range(1, depth):
        acc = acc + soft[l]
        o_ref[l:l + 1, :] = acc - soft[0]


def _lb_table(p):
    return pl.pallas_call(
        _lb_kernel,
        out_shape=jax.ShapeDtypeStruct(p.shape, F32),
        name="hgrn_lb_table",
    )(p)


def _modulate_kernel(x_ref, sc_ref, sh_ref, o_ref):
    o_ref[...] = (x_ref[...] * (1.0 + sc_ref[...]) + sh_ref[...]).astype(o_ref.dtype)


def _modulate(x, scale, shift):
    t, d = x.shape
    tm = ROW_TILE
    return pl.pallas_call(
        _modulate_kernel,
        grid=(t // tm,),
        in_specs=[pl.BlockSpec((tm, d), lambda i: (i, 0)),
                  pl.BlockSpec((1, d), lambda i: (0, 0)),
                  pl.BlockSpec((1, d), lambda i: (0, 0))],
        out_specs=pl.BlockSpec((tm, d), lambda i: (i, 0)),
        out_shape=jax.ShapeDtypeStruct((t, d), BF16),
        compiler_params=_params(("parallel",)),
        name="modulate",
    )(x, scale, shift)


def _proj_kernel(h_ref, w_ref, o_ref):
    o_ref[...] = _dot(h_ref[...], w_ref[...])


def _project(h, w):
    t, d = h.shape
    n = w.shape[1]
    tm = 2 * ROW_TILE
    return pl.pallas_call(
        _proj_kernel,
        grid=(t // tm,),
        in_specs=[pl.BlockSpec((tm, d), lambda i: (i, 0)),
                  pl.BlockSpec((d, n), lambda i: (0, 0))],
        out_specs=pl.BlockSpec((tm, n), lambda i: (i, 0)),
        out_shape=jax.ShapeDtypeStruct((t, n), F32),
        compiler_params=_params(("parallel",)),
        name="in_proj",
    )(h, w)


def _rglru_kernel(p_ref, cw_ref, cb_ref, wa_ref, ba_ref, wx_ref, bx_ref, lam_ref, o_ref, tail_ref, h_ref):
    @pl.when(pl.program_id(0) == 0)
    def _():
        tail_ref[...] = jnp.zeros_like(tail_ref)
        h_ref[...] = jnp.zeros_like(h_ref)

    p = p_ref[...]
    n = p.shape[0]
    ax = p[:, :MIX_W]
    ag = p[:, MIX_W:]
    xc = _causal_conv(ax, tail_ref, cw_ref[...], cb_ref[...])
    xb = xc.astype(BF16)
    r = _sigmoid(_dot(xb, wa_ref[...]) + ba_ref[...])
    gi = _sigmoid(_dot(xb, wx_ref[...]) + bx_ref[...])
    log_a = (-LRU_C) * r * _softplus(-lam_ref[...])
    a = jnp.exp(log_a)
    u = jnp.sqrt(-jnp.tanh(log_a) * (a * a + 1.0)) * (gi * xc)
    row = _row_iota(a.shape)
    d = 1
    while d < n:
        keep = row >= d
        a_s = pltpu.roll(a, d, 0)
        u_s = pltpu.roll(u, d, 0)
        u = jnp.where(keep, a * u_s + u, u)
        a = jnp.where(keep, a * a_s, a)
        d *= 2
    h = u + a * h_ref[0:1]
    h_ref[...] = jnp.broadcast_to(h[n - 1:n], h_ref.shape)
    o_ref[...] = (h * _gelu_tanh(ag)).astype(o_ref.dtype)


def _rglru(proj, cw, cb, wa, ba, wx, bx, lam):
    t = proj.shape[0]
    n = LRU_CHUNK
    vec = pl.BlockSpec((1, MIX_W), lambda i: (0, 0))
    sq = pl.BlockSpec((MIX_W, MIX_W), lambda i: (0, 0))
    return pl.pallas_call(
        _rglru_kernel,
        grid=(t // n,),
        in_specs=[pl.BlockSpec((n, 2 * MIX_W), lambda i: (i, 0)),
                  pl.BlockSpec((CONV_K, MIX_W), lambda i: (0, 0)), vec, sq, vec, sq, vec, vec],
        out_specs=pl.BlockSpec((n, MIX_W), lambda i: (i, 0)),
        out_shape=jax.ShapeDtypeStruct((t, MIX_W), BF16),
        scratch_shapes=[pltpu.VMEM((SUBLANES + n, MIX_W), F32), pltpu.VMEM((SUBLANES, MIX_W), F32)],
        compiler_params=_params(("arbitrary",)),
        name="rglru",
    )(proj, cw, cb, wa, ba, wx, bx, lam)


def _chunked(chunk_fn, state_refs, p_ref, o_ref, *refs):
    @pl.when(pl.program_id(0) == 0)
    def _():
        for s in state_refs:
            s[...] = jnp.zeros_like(s)

    for sub in range(p_ref.shape[0] // SCAN_CHUNK):
        rows = pl.ds(sub * SCAN_CHUNK, SCAN_CHUNK)
        chunk_fn(p_ref.at[rows], *refs, o_ref.at[rows], *state_refs)


def _ssd_kernel(p_ref, cw_ref, cb_ref, dtbn_ref, an_ref, dx_ref, nw_ref, sel_ref, o_ref, tail_ref, st_ref):
    _chunked(_ssd_chunk, (tail_ref, st_ref), p_ref, o_ref,
             cw_ref, cb_ref, dtbn_ref, an_ref, dx_ref, nw_ref, sel_ref)


def _ssd_chunk(p_ref, cw_ref, cb_ref, dtbn_ref, an_ref, dx_ref, nw_ref, sel_ref, o_ref, tail_ref, st_ref):
    p = p_ref[...]
    n = p.shape[0]
    gw = SSD_HPG * SSD_P
    z = p[:, 0:MIX_W]
    xbc_raw = p[:, MIX_W:3 * MIX_W]
    dtn_raw = p[:, 3 * MIX_W:3 * MIX_W + LANES]
    xbc = _silu(_causal_conv(xbc_raw, tail_ref, cw_ref[...], cb_ref[...]))
    xv = xbc[:, :MIX_W]
    bm = xbc[:, MIX_W:MIX_W + SSD_G * SSD_N]
    cm = xbc[:, MIX_W + SSD_G * SSD_N:]
    dtn = _softplus(dtn_raw + dtbn_ref[...])
    csn = _cumsum_rows(dtn * an_ref[...])
    csn_t = jnp.transpose(csn)
    sel = sel_ref[...]
    csx = _expand_lanes(csn, sel)
    xdt = xv * _expand_lanes(dtn, sel)
    tril = _row_iota((n, n)) >= lax.broadcasted_iota(I32, (n, n), 1)
    lane_head = lax.shift_right_logical(lax.broadcasted_iota(I32, (n, gw), 1), SSD_P.bit_length() - 1)
    ys = []
    for g in range(SSD_G):
        seg = slice(g * gw, (g + 1) * gw)
        cg = cm[:, g * SSD_N:(g + 1) * SSD_N].astype(BF16)
        bg = bm[:, g * SSD_N:(g + 1) * SSD_N]
        scores = _dot_nt(cg, bg.astype(BF16))
        csg = csx[:, seg]
        xdt_g = xdt[:, seg]
        xdt_gb = xdt_g.astype(BF16)
        st = st_ref[g]
        y = _dot(cg, st.astype(BF16)) * jnp.exp(csg)
        for h in range(SSD_HPG):
            hh = g * SSD_HPG + h
            col = csx[:, hh * SSD_P:hh * SSD_P + 1]
            dec = jnp.exp(jnp.where(tril, col - csn_t[hh:hh + 1, :], NEG_BIG))
            y = y + jnp.where(lane_head == h, _dot((scores * dec).astype(BF16), xdt_gb), 0.0)
        last = csg[n - 1:n]
        xw = (xdt_g * jnp.exp(last - csg)).astype(BF16)
        st_ref[g] = jnp.exp(last) * st + _dot(jnp.transpose(bg).astype(BF16), xw)
        ys.append(y)
    y = jnp.concatenate(ys, axis=1) + dx_ref[...] * xv
    yz = y * _silu(z)
    ms = jnp.mean(yz * yz, axis=1, keepdims=True)
    o_ref[...] = (yz * lax.rsqrt(ms + EPS) * nw_ref[...]).astype(o_ref.dtype)


def _ssd(proj, cw, cb, dtb_n, a_n, d_x, nw):
    t = proj.shape[0]
    n = MIXER_ROWS
    cc = 2 * MIX_W
    vec = lambda w: pl.BlockSpec((1, w), lambda i: (0, 0))
    sel = _lane_selector(SSD_HEADS, SSD_P)
    return pl.pallas_call(
        _ssd_kernel,
        grid=(t // n,),
        in_specs=[pl.BlockSpec((n, B_COLS), lambda i: (i, 0)),
                  pl.BlockSpec((CONV_K, cc), lambda i: (0, 0)), vec(cc),
                  vec(LANES), vec(LANES), vec(MIX_W), vec(MIX_W),
                  pl.BlockSpec(sel.shape, lambda i: (0, 0))],
        out_specs=pl.BlockSpec((n, MIX_W), lambda i: (i, 0)),
        out_shape=jax.ShapeDtypeStruct((t, MIX_W), BF16),
        scratch_shapes=[pltpu.VMEM((SUBLANES + SCAN_CHUNK, cc), F32),
                        pltpu.VMEM((SSD_G, SSD_N, SSD_HPG * SSD_P), F32)],
        compiler_params=_params(("arbitrary",)),
        name="ssd",
    )(proj, cw, cb, dtb_n, a_n, d_x, nw, sel)


def _cummax_rows(x):
    n = x.shape[0]
    row = _row_iota(x.shape)
    d = 1
    while d < n:
        x = jnp.maximum(x, jnp.where(row >= d, pltpu.roll(x, d, 0), NEG_BIG))
        d *= 2
    return x


def _lane_selector(heads, width):
    k = lax.broadcasted_iota(I32, (LANES, heads * width), 0)
    j = lax.broadcasted_iota(I32, (LANES, heads * width), 1)
    return (k == j // width).astype(BF16)


def _mlstm_kernel(p_ref, cw_ref, cb_ref, gb_ref, nw_ref, sel_ref, o_ref, tail_ref, c_ref, n_ref, m_ref):
    _chunked(_mlstm_chunk, (tail_ref, c_ref, n_ref, m_ref), p_ref, o_ref,
             cw_ref, cb_ref, gb_ref, nw_ref, sel_ref)


def _mlstm_chunk(p_ref, cw_ref, cb_ref, gb_ref, nw_ref, sel_ref, o_ref, tail_ref, c_ref, n_ref, m_ref):
    p = p_ref[...]
    n = p.shape[0]
    dh = MLSTM_DH
    qk_raw = p[:, :2 * MIX_W]
    qk = _silu(_causal_conv(qk_raw, tail_ref, cw_ref[...], cb_ref[...]))
    v_all = p[:, 2 * MIX_W:3 * MIX_W]
    pre = p[:, 3 * MIX_W:3 * MIX_W + LANES] + gb_ref[...]
    og_all = _sigmoid(p[:, 3 * MIX_W + LANES:])
    nw_all = nw_ref[...]
    fpre = pltpu.roll(pre, LANES - MLSTM_HEADS, 1)
    cs_n = _cumsum_rows(jnp.minimum(fpre, 0.0) - _log1p_exp_neg_abs(fpre))
    r_n = pre - cs_n
    rt = jnp.transpose(r_n)
    sel = sel_ref[...]
    cs_b = _expand_lanes(cs_n, sel)
    ig_b = _expand_lanes(pre, sel)
    cm_b = _expand_lanes(_cummax_rows(r_n), sel)
    tril = _row_iota((n, n)) >= lax.broadcasted_iota(I32, (n, n), 1)
    m_old = [m_ref[h:h + 1, :] for h in range(MLSTM_HEADS)]
    c_old = [c_ref[h] for h in range(MLSTM_HEADS)]
    n_old = [n_ref[h:h + 1, :] for h in range(MLSTM_HEADS)]
    c_new, n_new, m_news, outs = [], [], [], []
    for h in range(MLSTM_HEADS):
        sl = slice(h * dh, (h + 1) * dh)
        q = qk[:, sl] * (dh ** -0.5)
        k = qk[:, MIX_W + h * dh:MIX_W + (h + 1) * dh]
        v = v_all[:, sl]
        qb = q.astype(BF16)
        vb = v.astype(BF16)
        cs = cs_b[:, sl]
        ig = ig_b[:, sl]
        m_prev = m_old[h]
        inter = cs + m_prev
        m_row = jnp.maximum(cs + cm_b[:, sl], inter)
        dexp = jnp.where(tril, (cs - m_row) + rt[h:h + 1, :], NEG_BIG)
        w = jnp.exp(dexp) * _dot_nt(qb, k.astype(BF16))
        w_inter = jnp.exp(inter - m_row)
        cmem = c_old[h]
        nmem = n_old[h]
        num = _dot(w.astype(BF16), vb) + w_inter * _dot(qb, cmem.astype(BF16))
        den = jnp.sum(w, axis=1, keepdims=True) + w_inter * jnp.sum(q * nmem, axis=1, keepdims=True)
        hs = num / jnp.maximum(jnp.abs(den), jnp.exp(-m_row))
        g_tot = cs[n - 1:n]
        s_end = g_tot - cs + ig
        m_new = jnp.maximum(g_tot + m_prev, jnp.max(s_end, axis=0, keepdims=True))
        kw = k * jnp.exp(s_end - m_new)
        decay = jnp.exp(g_tot + m_prev - m_new)
        c_new.append(decay * cmem + _dot(jnp.transpose(kw).astype(BF16), vb))
        n_new.append(decay * nmem + jnp.sum(kw, axis=0, keepdims=True))
        m_news.append(jnp.broadcast_to(m_new, (1, LANES)))
        ms = jnp.mean(hs * hs, axis=1, keepdims=True)
        outs.append((og_all[:, sl] * (hs * lax.rsqrt(ms + EPS) * nw_all[:, sl])).astype(o_ref.dtype))
    for h in range(MLSTM_HEADS):
        c_ref[h] = c_new[h]
        n_ref[h:h + 1, :] = n_new[h]
        m_ref[h:h + 1, :] = m_news[h]
        o_ref[:, h * dh:(h + 1) * dh] = outs[h]


def _mlstm(proj, cw, cb, gate_bias, nw):
    t = proj.shape[0]
    n = SCAN_CHUNK
    assert n == LANES
    cc = 2 * MIX_W
    vec = lambda w: pl.BlockSpec((1, w), lambda i: (0, 0))
    return pl.pallas_call(
        _mlstm_kernel,
        grid=(t // n,),
        in_specs=[pl.BlockSpec((n, C_COLS), lambda i: (i, 0)),
                  pl.BlockSpec((CONV_K, cc), lambda i: (0, 0)), vec(cc), vec(LANES), vec(MIX_W),
                  pl.BlockSpec((LANES, MLSTM_HEADS * LANES), lambda i: (0, 0))],
        out_specs=pl.BlockSpec((n, MIX_W), lambda i: (i, 0)),
        out_shape=jax.ShapeDtypeStruct((t, MIX_W), BF16),
        scratch_shapes=[pltpu.VMEM((SUBLANES + SCAN_CHUNK, cc), F32),
                        pltpu.VMEM((MLSTM_HEADS, MLSTM_DH, MLSTM_DH), F32),
                        pltpu.VMEM((SUBLANES, MLSTM_DH), F32),
                        pltpu.VMEM((SUBLANES, LANES), F32)],
        compiler_params=_params(("arbitrary",)),
        name="mlstm",
    )(proj, cw, cb, gate_bias, nw, _lane_selector(MLSTM_HEADS, LANES))


def _hgrn_kernel(p_ref, lb_ref, nw_ref, o_ref, st_ref):
    _chunked(_hgrn_chunk, (st_ref,), p_ref, o_ref, lb_ref, nw_ref)


def _hgrn_chunk(p_ref, lb_ref, nw_ref, o_ref, st_ref):
    p = p_ref[...]
    n = p.shape[0]
    dk = HGRN_DK
    q = _silu(p[:, :MIX_W])
    u = p[:, MIX_W:2 * MIX_W]
    v = p[:, 2 * MIX_W:3 * MIX_W]
    gt = p[:, 3 * MIX_W:]
    lb = lb_ref[...]
    f = lb + (1.0 - lb) * _sigmoid(u)
    k = (1.0 - lb) * _sigmoid(-u)
    cs = _cumsum_rows(jnp.log(jnp.maximum(f, 1e-30)))
    last = cs[n - 1:n]
    rowi = _row_iota((n, 1))
    vb = v.astype(BF16)

    scores = [None] * HGRN_HEADS
    rr = _row_iota((n, n))
    cc = lax.broadcasted_iota(I32, (n, n), 1)
    hs = n // 2
    while hs >= HGRN_DIAG:
        blk = 2 * hs
        nb = n // blk
        if blk > SUBLANES:
            ref = jnp.concatenate([jnp.broadcast_to(cs[b * blk + hs - 1:b * blk + hs], (blk, MIX_W))
                                   for b in range(nb)], axis=0)
        else:
            c3 = cs.reshape(n // SUBLANES, SUBLANES, MIX_W)
            sub = lax.broadcasted_iota(I32, c3.shape, 1)
            ref = None
            for j in reversed(range(SUBLANES // blk)):
                rj = jnp.broadcast_to(c3[:, j * blk + hs - 1:j * blk + hs, :], c3.shape)
                ref = rj if ref is None else jnp.where(sub < (j + 1) * blk, rj, ref)
            ref = ref.reshape(n, MIX_W)
        upper = (rowi & (blk - 1)) >= hs
        qj = jnp.where(upper, q * jnp.exp(jnp.minimum(cs - ref, 0.0)), 0.0).astype(BF16)
        kj = jnp.where(upper, 0.0, k * jnp.exp(jnp.minimum(ref - cs, 0.0))).astype(BF16)
        same = (rr & -blk) == (cc & -blk)
        for h in range(HGRN_HEADS):
            sl = slice(h * dk, (h + 1) * dk)
            s = _dot_nt(qj[:, sl], kj[:, sl])
            if nb > 1:
                s = jnp.where(same, s, 0.0)
            scores[h] = s if scores[h] is None else scores[h] + s
        hs //= 2

    band = rr - cc
    for dlt in range(HGRN_DIAG):
        if dlt == 0:
            t_all = q * k
        else:
            kd = pltpu.roll(k, dlt, 0)
            csd = pltpu.roll(cs, dlt, 0)
            t_all = q * kd * jnp.exp(jnp.minimum(cs - csd, 0.0))
        valid = (rowi & (HGRN_DIAG - 1)) >= dlt
        for h in range(HGRN_HEADS):
            sl = slice(h * dk, (h + 1) * dk)
            s = jnp.where(valid, jnp.sum(t_all[:, sl], axis=1, keepdims=True), 0.0)
            scores[h] = scores[h] + jnp.where(band == dlt, s, 0.0)

    qe = (q * jnp.exp(cs)).astype(BF16)
    ys = []
    for h in range(HGRN_HEADS):
        sl = slice(h * dk, (h + 1) * dk)
        ys.append(_dot(scores[h].astype(BF16), vb[:, sl]) + _dot_nt(qe[:, sl], st_ref[h].astype(BF16)))

    kw = k * jnp.exp(last - cs)
    elast = jnp.exp(last)
    for h in range(HGRN_HEADS):
        sl = slice(h * dk, (h + 1) * dk)
        st_ref[h] = elast[:, sl] * st_ref[h] + _dot(jnp.transpose(v[:, sl]).astype(BF16), kw[:, sl].astype(BF16))
        y = ys[h]
        ms = jnp.mean(y * y, axis=1, keepdims=True)
        o_ref[:, sl] = (y * lax.rsqrt(ms + EPS) * nw_ref[:, sl] * _silu(gt[:, sl])).astype(o_ref.dtype)


def _hgrn(proj, lb, nw):
    t = proj.shape[0]
    n = MIXER_ROWS
    vec = pl.BlockSpec((1, MIX_W), lambda i: (0, 0))
    return pl.pallas_call(
        _hgrn_kernel,
        grid=(t // n,),
        in_specs=[pl.BlockSpec((n, 4 * MIX_W), lambda i: (i, 0)), vec, vec],
        out_specs=pl.BlockSpec((n, MIX_W), lambda i: (i, 0)),
        out_shape=jax.ShapeDtypeStruct((t, MIX_W), BF16),
        scratch_shapes=[pltpu.VMEM((HGRN_HEADS, HGRN_DK, HGRN_DK), F32)],
        compiler_params=_params(("arbitrary",)),
        name="hgrn2",
    )(proj, lb, nw)


def _merge_kernel(h_ref, ba_ref, bb_ref, bc_ref, bd_ref, wg_ref, bg_ref, wu_ref, o_ref, wg_s, wu_s):
    @pl.when(pl.program_id(1) == 0)
    def _():
        wg_s[...] = wg_ref[0].astype(BF16)
        wu_s[...] = wu_ref[0].astype(BF16)

    h = h_ref[...]
    acc = None
    for br, b_ref in enumerate((ba_ref, bb_ref, bc_ref, bd_ref)):
        gate = _sigmoid(_dot(h, wg_s[br]) + bg_ref[br])
        term = gate * _dot(b_ref[...], wu_s[br])
        acc = term if acc is None else acc + term
    o_ref[...] = acc.astype(o_ref.dtype)


def _merge(layer, h, branches, wg, bg, wu):
    t, d = h.shape
    tm, tn = 1024, 256
    nb = len(branches)
    bspec = pl.BlockSpec((tm, MIX_W), lambda j, i: (i, 0))
    return pl.pallas_call(
        _merge_kernel,
        grid=(d // tn, t // tm),
        in_specs=[pl.BlockSpec((tm, d), lambda j, i: (i, 0)), bspec, bspec, bspec, bspec,
                  pl.BlockSpec((1, nb, d, tn), lambda j, i: (layer, 0, 0, j)),
                  pl.BlockSpec((nb, 1, tn), lambda j, i: (0, 0, j)),
                  pl.BlockSpec((1, nb, MIX_W, tn), lambda j, i: (layer, 0, 0, j))],
        out_specs=pl.BlockSpec((tm, tn), lambda j, i: (i, j)),
        out_shape=jax.ShapeDtypeStruct((t, d), BF16),
        scratch_shapes=[pltpu.VMEM((nb, d, tn), BF16), pltpu.VMEM((nb, MIX_W, tn), BF16)],
        compiler_params=_params(("parallel", "arbitrary")),
        name="merge",
    )(h, *branches, wg, bg, wu)


def _layer_norm_rows(z, w, b):
    mu = jnp.mean(z, axis=1, keepdims=True)
    zc = z - mu
    var = jnp.mean(zc * zc, axis=1, keepdims=True)
    return zc * lax.rsqrt(var + EPS) * w + b


def _pack_bf16_pairs(h):
    half = h.shape[1] // 2
    lo = lax.bitcast_convert_type(h[:, :half].astype(BF16).astype(F32), jnp.uint32)
    hi = lax.bitcast_convert_type(h[:, half:].astype(BF16).astype(F32), jnp.uint32)
    return lax.shift_right_logical(lo, jnp.uint32(16)) | (hi & jnp.uint32(0xFFFF0000))


def _unpack_bf16_pairs(w):
    lo = lax.bitcast_convert_type(lax.shift_left(w, jnp.uint32(16)), F32)
    hi = lax.bitcast_convert_type(w & jnp.uint32(0xFFFF0000), F32)
    return jnp.concatenate([lo, hi], axis=1).astype(BF16)


def _out_ln_route_kernel(m_ref, w_ref, x_ref, g_ref, lw_ref, lb_ref, sc_ref, sh_ref, rw_ref, rb_ref,
                         x_out, hp_out, r_out, cnt_out, run_ref):
    @pl.when(pl.program_id(0) == 0)
    def _():
        run_ref[...] = jnp.zeros_like(run_ref)

    y = _dot(m_ref[...], w_ref[...])
    xn = _layer_norm_rows(ALPHA * x_ref[...] + g_ref[...] * y, lw_ref[...], lb_ref[...])
    x_out[...] = xn
    h = xn * (1.0 + sc_ref[...]) + sh_ref[...]
    hp_out[...] = _pack_bf16_pairs(h)
    route, new1, new2 = _route_rows(h, rw_ref, rb_ref, run_ref[...])
    r_out[...] = route
    run_ref[0:1] = new1
    run_ref[1:2] = new2
    cnt_out[...] = run_ref[...]


def _out_ln_route(merged, w_out, x, gate, ln_w, ln_b, scale_next, shift_next, rw, rb):
    t, d = x.shape
    tm = LN_TILE
    vec = pl.BlockSpec((1, d), lambda i: (0, 0))
    row = pl.BlockSpec((tm, d), lambda i: (i, 0))
    rw_hi = rw.astype(BF16)
    return pl.pallas_call(
        _out_ln_route_kernel,
        grid=(t // tm,),
        in_specs=[row, pl.BlockSpec((d, d), lambda i: (0, 0)), row, vec, vec, vec, vec, vec,
                  pl.BlockSpec((2, d, LANES), lambda i: (0, 0, 0)),
                  pl.BlockSpec((1, LANES), lambda i: (0, 0))],
        out_specs=[row, pl.BlockSpec((tm, d // 2), lambda i: (i, 0)),
                   pl.BlockSpec((tm, LANES), lambda i: (i, 0)),
                   pl.BlockSpec((SUBLANES, LANES), lambda i: (0, 0))],
        out_shape=[jax.ShapeDtypeStruct((t, d), F32), jax.ShapeDtypeStruct((t, d // 2), jnp.uint32),
                   jax.ShapeDtypeStruct((t, LANES), F32), jax.ShapeDtypeStruct((SUBLANES, LANES), F32)],
        scratch_shapes=[pltpu.VMEM((SUBLANES, LANES), F32)],
        compiler_params=_params(("arbitrary",)),
        name="out_proj_ln_route",
    )(merged, w_out, x, gate, ln_w, ln_b, scale_next, shift_next,
      jnp.stack([rw_hi, (rw - rw_hi.astype(F32)).astype(BF16)]), rb)


def _route_rows(h, w_ref, b_ref, run):
    n = h.shape[0]
    h_hi = h.astype(BF16)
    h_lo = (h - h_hi.astype(F32)).astype(BF16)
    logits = _dot(h_hi, w_ref[0]) + _dot(h_lo, w_ref[0]) + _dot(h_hi, w_ref[1]) + b_ref[...]
    lane = lax.broadcasted_iota(I32, logits.shape, 1)
    lane_f = lane.astype(F32)
    big = float(LANES)
    gl = jnp.where(lane < N_GROUPS, logits, NEG_BIG)
    ge = jnp.exp(gl - jnp.max(gl, axis=1, keepdims=True))
    gp = ge / jnp.sum(ge, axis=1, keepdims=True)
    p_grp = jnp.max(gp, axis=1, keepdims=True)
    grp = jnp.min(jnp.where(gp == p_grp, lane_f, big), axis=1, keepdims=True)
    lo = N_GROUPS + EXP_PER_GROUP * grp
    el = jnp.where((lane_f >= lo) & (lane_f < lo + EXP_PER_GROUP), logits, NEG_BIG)
    v1 = jnp.max(el, axis=1, keepdims=True)
    i1 = jnp.min(jnp.where(el == v1, lane_f, big), axis=1, keepdims=True)
    el2 = jnp.where(lane_f == i1, NEG_BIG, el)
    v2 = jnp.max(el2, axis=1, keepdims=True)
    i2 = jnp.min(jnp.where(el2 == v2, lane_f, big), axis=1, keepdims=True)
    e2 = jnp.exp(v2 - v1)
    w1 = (1.0 / (1.0 + e2)) * p_grp
    w2 = (e2 / (1.0 + e2)) * p_grp
    oh1 = jnp.where(lane_f == i1, 1.0, 0.0)
    oh2 = jnp.where(lane_f == i2, 1.0, 0.0)
    strict = (_row_iota((n, n)) > lax.broadcasted_iota(I32, (n, n), 1)).astype(BF16)
    c1 = _dot(strict, oh1.astype(BF16)) + run[0:1]
    c2 = _dot(strict, oh2.astype(BF16)) + run[1:2]
    r1 = jnp.sum(oh1 * c1, axis=1, keepdims=True)
    r2 = jnp.sum(oh2 * c2, axis=1, keepdims=True)
    new1 = run[0:1] + jnp.sum(oh1, axis=0, keepdims=True)
    new2 = run[1:2] + jnp.sum(oh2, axis=0, keepdims=True)
    out = jnp.where(lane == 0, i1 - N_GROUPS, 0.0)
    out = jnp.where(lane == 1, i2 - N_GROUPS, out)
    out = jnp.where(lane == 2, w1, out)
    out = jnp.where(lane == 3, w2, out)
    out = jnp.where(lane == 4, r1, out)
    out = jnp.where(lane == 5, r2, out)
    return out, new1, new2


def _dispatch_kernel(zflag_ref, dest_ref, h_ref, xs_ref, zbuf, sem, zsem):
    nt = h_ref.shape[0]
    bm = zbuf.shape[0]

    @pl.when(pl.program_id(0) == 0)
    def _():
        zbuf[...] = jnp.zeros_like(zbuf)

        def zblock(b):
            return pltpu.make_async_copy(zbuf, xs_ref.at[pl.ds(pl.multiple_of(b * bm, bm), bm)], zsem)

        def start_block(b, carry):
            @pl.when(zflag_ref[b] != 0)
            def _():
                zblock(b).start()
            return carry

        def wait_block(b, carry):
            @pl.when(zflag_ref[b] != 0)
            def _():
                zblock(b).wait()
            return carry

        n_blocks = xs_ref.shape[0] // bm
        lax.fori_loop(0, n_blocks, start_block, 0)
        lax.fori_loop(0, n_blocks, wait_block, 0)

    def copy(r, j):
        return pltpu.make_async_copy(h_ref.at[pl.ds(r, 1)],
                                     xs_ref.at[pl.ds(dest_ref[0, 0, 2 * r + j], 1)], sem)

    def issue(r, carry):
        copy(r, 0).start()
        copy(r, 1).start()
        return carry

    def drain(r, carry):
        copy(r, 0).wait()
        copy(r, 1).wait()
        return carry

    lax.fori_loop(0, nt, issue, 0, unroll=8)
    lax.fori_loop(0, nt, drain, 0, unroll=8)


def _dispatch(h, dest, zflag, n_slots):
    t, w = h.shape
    nt = DISPATCH_TILE
    grid_spec = pltpu.PrefetchScalarGridSpec(
        num_scalar_prefetch=1,
        grid=(t // nt,),
        in_specs=[pl.BlockSpec((1, 1, 2 * nt), lambda i, zf: (i, 0, 0), memory_space=pltpu.SMEM),
                  pl.BlockSpec((nt, w), lambda i, zf: (i, 0))],
        out_specs=pl.BlockSpec(memory_space=pl.ANY),
        scratch_shapes=[pltpu.VMEM((MOE_BM, w), h.dtype), pltpu.SemaphoreType.DMA(()),
                        pltpu.SemaphoreType.DMA(())],
    )
    return pl.pallas_call(
        _dispatch_kernel,
        grid_spec=grid_spec,
        out_shape=jax.ShapeDtypeStruct((n_slots, w), h.dtype),
        compiler_params=pltpu.CompilerParams(dimension_semantics=("arbitrary",)),
        name="moe_dispatch",
    )(zflag, dest.reshape(t // nt, 1, 2 * nt), h)


def _expert_kernel(be_ref, nu_ref, xs_ref, wg_ref, wu_ref, wd_ref, ys_ref, wg_s, wu_s, wd_s):
    b = pl.program_id(0)
    used = b < nu_ref[0]
    prev = be_ref[jnp.maximum(b - 1, 0)]

    @pl.when(used & ((b == 0) | (be_ref[b] != prev)))
    def _():
        wg_s[...] = wg_ref[0, 0].astype(BF16)
        wu_s[...] = wu_ref[0, 0].astype(BF16)
        wd_s[...] = wd_ref[0, 0].astype(BF16)

    @pl.when(used)
    def _():
        rows = _unpack_bf16_pairs(xs_ref[...])
        hid = _silu(_dot(rows, wg_s[...])) * _dot(rows, wu_s[...])
        ys_ref[...] = _dot(hid.astype(BF16), wd_s[...])

    @pl.when(jnp.logical_not(used))
    def _():
        ys_ref[...] = jnp.zeros_like(ys_ref)


def _experts(layer, blk_e, n_used, xs, w_g, w_u, w_d):
    n_slots = xs.shape[0]
    d, f = w_g.shape[2], w_g.shape[3]
    bm = MOE_BM
    grid_spec = pltpu.PrefetchScalarGridSpec(
        num_scalar_prefetch=2,
        grid=(n_slots // bm,),
        in_specs=[pl.BlockSpec((bm, d // 2), lambda b, be, nu: (jnp.minimum(b, nu[0] - 1), 0)),
                  pl.BlockSpec((1, 1, d, f), lambda b, be, nu: (layer, be[b], 0, 0)),
                  pl.BlockSpec((1, 1, d, f), lambda b, be, nu: (layer, be[b], 0, 0)),
                  pl.BlockSpec((1, 1, f, d), lambda b, be, nu: (layer, be[b], 0, 0))],
        out_specs=pl.BlockSpec((bm, d), lambda b, be, nu: (b, 0)),
        scratch_shapes=[pltpu.VMEM((d, f), BF16), pltpu.VMEM((d, f), BF16), pltpu.VMEM((f, d), BF16)],
    )
    return pl.pallas_call(
        _expert_kernel,
        grid_spec=grid_spec,
        out_shape=jax.ShapeDtypeStruct((n_slots, d), F32),
        compiler_params=_params(("arbitrary",)),
        name="moe_experts",
    )(blk_e, n_used, xs, w_g, w_u, w_d)


def _combine_ln_kernel(dest_ref, dnext_ref, ys_ref, r_ref, x_ref, g_ref, lw_ref, lb_ref, sc_ref, sh_ref,
                       x_out, h_out, buf, sem):
    nt = buf.shape[2]
    i = pl.program_id(0)
    slot = lax.rem(i, 2)

    def copy(d_ref, s, r, j):
        return pltpu.make_async_copy(ys_ref.at[pl.ds(d_ref[0, 0, 2 * r + j], 1)],
                                     buf.at[s, j, pl.ds(r, 1)], sem.at[s])

    def gather(d_ref, s):
        def issue(r, carry):
            copy(d_ref, s, r, 0).start()
            copy(d_ref, s, r, 1).start()
            return carry
        lax.fori_loop(0, nt, issue, 0, unroll=8)

    @pl.when(i == 0)
    def _():
        gather(dest_ref, 0)

    @pl.when(i + 1 < pl.num_programs(0))
    def _():
        gather(dnext_ref, 1 - slot)

    def drain(r, carry):
        copy(dest_ref, slot, r, 0).wait()
        copy(dest_ref, slot, r, 1).wait()
        return carry

    lax.fori_loop(0, nt, drain, 0, unroll=8)
    rt = r_ref[...]
    y = buf[slot, 0] * rt[:, 2:3] + buf[slot, 1] * rt[:, 3:4]
    xn = _layer_norm_rows(ALPHA * x_ref[...] + g_ref[...] * y, lw_ref[...], lb_ref[...])
    x_out[...] = xn
    h_out[...] = (xn * (1.0 + sc_ref[...]) + sh_ref[...]).astype(h_out.dtype)


def _combine_ln(dest, ys, route, x, gate, ln_w, ln_b, scale_next, shift_next):
    t, d = x.shape
    tm = LN_TILE
    vec = pl.BlockSpec((1, d), lambda i: (0, 0))
    row = pl.BlockSpec((tm, d), lambda i: (i, 0))
    nsteps = t // tm
    dest3 = dest.reshape(nsteps, 1, 2 * tm)
    return pl.pallas_call(
        _combine_ln_kernel,
        grid=(nsteps,),
        in_specs=[pl.BlockSpec((1, 1, 2 * tm), lambda i: (i, 0, 0), memory_space=pltpu.SMEM),
                  pl.BlockSpec((1, 1, 2 * tm), lambda i: (jnp.minimum(i + 1, nsteps - 1), 0, 0),
                               memory_space=pltpu.SMEM),
                  pl.BlockSpec(memory_space=pl.ANY),
                  pl.BlockSpec((tm, LANES), lambda i: (i, 0)),
                  row, vec, vec, vec, vec, vec],
        out_specs=[row, row],
        out_shape=[jax.ShapeDtypeStruct((t, d), F32), jax.ShapeDtypeStruct((t, d), BF16)],
        scratch_shapes=[pltpu.VMEM((2, 2, tm, d), F32), pltpu.SemaphoreType.DMA((2,))],
        compiler_params=_params(("arbitrary",)),
        name="moe_combine_ln",
    )(dest3, dest3, ys, route, x, gate, ln_w, ln_b, scale_next, shift_next)


def _slot_kernel(r_ref, tab_ref, o_ref):
    rt = r_ref[...]
    lane = lax.broadcasted_iota(I32, rt.shape, 1)
    lane_f = lane.astype(F32)
    tab = tab_ref[...]
    d1 = jnp.sum(jnp.where(lane_f == rt[:, 0:1] + N_GROUPS, tab[0:1], 0.0), axis=1, keepdims=True) + rt[:, 4:5]
    d2 = jnp.sum(jnp.where(lane_f == rt[:, 1:2] + N_GROUPS, tab[1:2], 0.0), axis=1, keepdims=True) + rt[:, 5:6]
    o_ref[...] = jnp.where(lane == 0, d1, jnp.where(lane == 1, d2, 0.0)).astype(I32)


def _moe_plan(route, counts):
    bm = MOE_BM
    t = route.shape[0]
    tm = 1024
    n_blocks = (2 * t) // bm + N_EXPERTS
    c1 = counts[0].astype(I32)
    c2 = counts[1].astype(I32)
    pcounts = (c1 + c2 + bm - 1) // bm * bm
    pends = jnp.cumsum(pcounts)
    pstart = pends - pcounts
    tab = jnp.zeros((SUBLANES, LANES), F32).at[0].set(pstart.astype(F32)).at[1].set((pstart + c1).astype(F32))
    slots = pl.pallas_call(
        _slot_kernel,
        grid=(t // tm,),
        in_specs=[pl.BlockSpec((tm, LANES), lambda i: (i, 0)),
                  pl.BlockSpec((SUBLANES, LANES), lambda i: (0, 0))],
        out_specs=pl.BlockSpec((tm, LANES), lambda i: (i, 0)),
        out_shape=jax.ShapeDtypeStruct((t, LANES), I32),
        compiler_params=_params(("parallel",)),
        name="moe_slots",
    )(route, tab)
    dest = slots[:, :2]
    ends = pends[N_GROUPS:N_GROUPS + N_EXPERTS]
    starts_of_blocks = jnp.arange(n_blocks, dtype=I32) * bm
    blk_e = jnp.minimum(jnp.sum((ends[None, :] <= starts_of_blocks[:, None]).astype(I32), axis=1),
                        N_EXPERTS - 1).astype(I32)
    n_used = (pends[-1] // bm).astype(I32).reshape(1)
    blk = jnp.arange(n_blocks, dtype=I32)
    next_e = jnp.concatenate([blk_e[1:], blk_e[-1:]])
    zflag = ((blk >= n_used[0] - 1) | (next_e != blk_e)).astype(I32)
    return dest, blk_e, n_used, zflag, n_blocks * bm


def _block_diag(w):
    nb, bw, _ = w.shape
    eye = jnp.eye(nb, dtype=w.dtype)
    return (eye[:, None, :, None] * w[:, :, None, :]).reshape(nb * bw, nb * bw)


def _pad_lanes(v, width=LANES):
    return jnp.pad(v, [(0, 0)] * (v.ndim - 1) + [(0, width - v.shape[-1])])


def _split_w_in(w):
    o = 0
    parts = []
    for s in (MIX_W, MIX_W, MIX_W, MIX_W, SSD_G * SSD_N, SSD_G * SSD_N, SSD_HEADS, MIX_W, MIX_W, MIX_W,
              MLSTM_HEADS, MLSTM_HEADS, MIX_W, MIX_W, MIX_W, MIX_W, MIX_W):
        parts.append(w[:, o:o + s])
        o += s
    a_x, a_g, b_z, b_x, b_b, b_c, b_dt, c_q, c_k, c_v, c_i, c_f, c_o, d_q, d_f, d_i, d_g = parts
    wa = jnp.concatenate([a_x, a_g], axis=1)
    wb = jnp.concatenate([b_z, b_x, b_b, b_c, _pad_lanes(b_dt)], axis=1)
    wc = jnp.concatenate([c_q, c_k, c_v, _pad_lanes(jnp.concatenate([c_i, c_f], axis=1)), c_o], axis=1)
    wd = jnp.concatenate([d_q, d_f, d_i, d_g], axis=1)
    return [m.astype(BF16) for m in (wa, wb, wc, wd)]


def kernel(x, c, ada_w, ada_b, w_in, lru_conv_w, lru_conv_b, lru_wa, lru_ba, lru_wx, lru_bx, lru_lambda, ssd_conv_w, ssd_conv_b, ssd_dt_bias, ssd_a_log, ssd_d, ssd_norm_w, mlstm_conv_w, mlstm_conv_b, mlstm_i_bias, mlstm_f_bias, mlstm_norm_w, hgrn_lb_param, hgrn_norm_w, w_up, w_gate, b_gate, w_out, ln1_w, ln1_b, router_group_w, router_group_b, router_expert_w, router_expert_b, exp_w_gate, exp_w_up, exp_w_down, ln2_w, ln2_b):
    bsz, seq, d = x.shape
    depth = w_in.shape[0]
    assert bsz == 1 and d == D_MODEL
    xs = x.reshape(seq, d)
    ada = _ada_table(c, ada_w, ada_b)
    lb_all = _lb_table(hgrn_lb_param)
    row = lambda v: v.reshape(1, -1)

    def mod_vec(l, k):
        return ada[l, k * d:(k + 1) * d].reshape(1, d)

    h = _modulate(xs, mod_vec(0, 1), mod_vec(0, 0))
    for l in range(depth):
        wa, wb, wc, wd = _split_w_in(w_in[l])
        br_a = _rglru(_project(h, wa), lru_conv_w[l], row(lru_conv_b[l]),
                      _block_diag(lru_wa[l]).astype(BF16), row(lru_ba[l]),
                      _block_diag(lru_wx[l]).astype(BF16), row(lru_bx[l]), row(lru_lambda[l]))
        a_heads = -jnp.exp(ssd_a_log[l])
        br_b = _ssd(_project(h, wb), ssd_conv_w[l], row(ssd_conv_b[l]),
                    row(_pad_lanes(ssd_dt_bias[l])), row(_pad_lanes(a_heads)),
                    row(jnp.repeat(ssd_d[l], SSD_P)), row(ssd_norm_w[l]))
        gate_bias = _pad_lanes(jnp.concatenate([mlstm_i_bias[l], mlstm_f_bias[l]]))
        br_c = _mlstm(_project(h, wc), mlstm_conv_w[l], row(mlstm_conv_b[l]), row(gate_bias),
                      row(mlstm_norm_w[l]))
        br_d = _hgrn(_project(h, wd), row(lb_all[l]), row(hgrn_norm_w[l]))
        merged = _merge(l, h, (br_a, br_b, br_c, br_d), w_gate, b_gate[l].reshape(4, 1, d), w_up)
        wr = _pad_lanes(jnp.concatenate([router_group_w[l], router_expert_w[l]], axis=1))
        br = row(_pad_lanes(jnp.concatenate([router_group_b[l], router_expert_b[l]])))
        xs, h2p, route, counts = _out_ln_route(merged, w_out[l].astype(BF16), xs, mod_vec(l, 2), row(ln1_w[l]),
                                               row(ln1_b[l]), mod_vec(l, 4), mod_vec(l, 3), wr, br)
        dest, blk_e, n_used, zflag, n_slots = _moe_plan(route, counts)
        xsort = _dispatch(h2p, dest, zflag, n_slots)
        ysort = _experts(l, blk_e, n_used, xsort, exp_w_gate, exp_w_up, exp_w_down)
        nl = min(l + 1, depth - 1)
        xs, h = _combine_ln(dest, ysort, route, xs, mod_vec(l, 5), row(ln2_w[l]), row(ln2_b[l]),
                            mod_vec(nl, 1), mod_vec(nl, 0))
    return xs.reshape(bsz, seq, d)
```

```python
import functools

import jax
import jax.numpy as jnp
from jax import lax
from jax.experimental import pallas as pl
from jax.experimental.pallas import tpu as pltpu

F32 = jnp.float32
BF16 = jnp.bfloat16
I32 = jnp.int32

D_MODEL = 2048
DEPTH = 4
MIX_W = 512
CONV_K = 4
LRU_BLOCKS = 8
LRU_BW = MIX_W // LRU_BLOCKS
LRU_C = 8.0
SSD_HEADS = 8
SSD_P = 64
SSD_G = 2
SSD_HPG = 4
SSD_N = 128
MLSTM_HEADS = 4
MLSTM_DH = 128
HGRN_HEADS = 4
HGRN_DK = 128
N_GROUPS = 4
EXP_PER_GROUP = 8
N_EXPERTS = 32
D_FF = 512
ALPHA = (2.0 * DEPTH) ** 0.25
EPS = 1e-5
NEG_BIG = -1e30

LANES = 128
SUBLANES = 8
VMEM_LIMIT = 56 * 1024 * 1024

SCAN_CHUNK = 128
MIXER_ROWS = 256
HGRN_DIAG = 2
LRU_CHUNK = 256
MOE_BM = 256
ROW_TILE = 512
LN_TILE = 256
DISPATCH_TILE = 512

B_COLS = 3 * MIX_W + LANES
C_COLS = 3 * MIX_W + LANES + MIX_W


def _params(sem, vmem=VMEM_LIMIT):
    return pltpu.CompilerParams(dimension_semantics=sem, vmem_limit_bytes=vmem)


def _dot(a, b):
    return jnp.dot(a, b, preferred_element_type=F32)


def _dot_nt(a, b):
    return lax.dot_general(a, b, (((1,), (1,)), ((), ())), preferred_element_type=F32)


def _sigmoid(x):
    return jax.nn.sigmoid(x)


def _silu(x):
    return x * jax.nn.sigmoid(x)


def _log1p_exp_neg_abs(x):
    return jnp.log(1.0 + jnp.exp(-jnp.abs(x)))


def _softplus(x):
    return jnp.maximum(x, 0.0) + _log1p_exp_neg_abs(x)


def _gelu_tanh(x):
    return 0.5 * x * (1.0 + jnp.tanh(0.7978845608028654 * (x + 0.044715 * (x * x * x))))


def _row_iota(shape):
    return lax.broadcasted_iota(I32, shape, 0)


def _split3(x):
    a = x.astype(BF16)
    r = x - a.astype(F32)
    b = r.astype(BF16)
    return a, b, (r - b.astype(F32)).astype(BF16)


def _cumsum_rows(x):
    n = x.shape[0]
    row = _row_iota(x.shape)
    d = 1
    while d < n:
        x = x + jnp.where(row >= d, pltpu.roll(x, d, 0), 0.0)
        d *= 2
    return x


def _expand_lanes(x, sel):
    a, b, c = _split3(x)
    return _dot(a, sel) + _dot(b, sel) + _dot(c, sel)


def _causal_conv(x, ext_ref, w, b):
    n = x.shape[0]
    ext_ref[SUBLANES:, :] = x
    acc = x * w[CONV_K - 1:CONV_K] + b
    for j in range(1, CONV_K):
        acc = acc + ext_ref[SUBLANES - j:SUBLANES - j + n, :] * w[CONV_K - 1 - j:CONV_K - j]
    ext_ref[0:SUBLANES, :] = x[n - SUBLANES:]
    return acc


def _ada_kernel(c_ref, w_ref, b_ref, o_ref):
    c = c_ref[...]
    o_ref[0] = jnp.dot(_silu(c), w_ref[0], preferred_element_type=F32,
                       precision=lax.Precision.HIGHEST) + b_ref[0]


def _ada_table(c, ada_w, ada_b):
    depth, d, e = ada_w.shape
    tn = 2048
    c8 = jnp.broadcast_to(c.reshape(1, d), (SUBLANES, d))
    out = pl.pallas_call(
        _ada_kernel,
        grid=(depth, e // tn),
        in_specs=[pl.BlockSpec((SUBLANES, d), lambda l, j: (0, 0)),
                  pl.BlockSpec((1, d, tn), lambda l, j: (l, 0, j)),
                  pl.BlockSpec((1, 1, tn), lambda l, j: (l, 0, j))],
        out_specs=pl.BlockSpec((1, SUBLANES, tn), lambda l, j: (l, 0, j)),
        out_shape=jax.ShapeDtypeStruct((depth, SUBLANES, e), F32),
        compiler_params=_params(("parallel", "parallel")),
        name="ada_table",
    )(c8, ada_w, ada_b.reshape(depth, 1, e))
    return out[:, 0, :]


def _lb_kernel(p_ref, o_ref):
    p = p_ref[...]
    depth = p.shape[0]
    rows = [p[l:l + 1] for l in range(depth)]
    m = rows[0]
    for r in rows[1:]:
        m = jnp.maximum(m, r)
    es = [jnp.exp(r - m) for r in rows]
    tot = es[0]
    for e in es[1:]:
        tot = tot + e
    soft = [e / tot for e in es]
    acc = soft[0]
    o_ref[0:1, :] = acc - soft[0]
    for l in range(1, depth):
        acc = acc + soft[l]
        o_ref[l:l + 1, :] = acc - soft[0]


def _lb_table(p):
    return pl.pallas_call(
        _lb_kernel,
        out_shape=jax.ShapeDtypeStruct(p.shape, F32),
        name="hgrn_lb_table",
    )(p)


def _modulate_kernel(x_ref, sc_ref, sh_ref, o_ref):
    o_ref[...] = (x_ref[...] * (1.0 + sc_ref[...]) + sh_ref[...]).astype(o_ref.dtype)


def _modulate(x, scale, shift):
    t, d = x.shape
    tm = ROW_TILE
    return pl.pallas_call(
        _modulate_kernel,
        grid=(t // tm,),
        in_specs=[pl.BlockSpec((tm, d), lambda i: (i, 0)),
                  pl.BlockSpec((1, d), lambda i: (0, 0)),
                  pl.BlockSpec((1, d), lambda i: (0, 0))],
        out_specs=pl.BlockSpec((tm, d), lambda i: (i, 0)),
        out_shape=jax.ShapeDtypeStruct((t, d), BF16),
        compiler_params=_params(("parallel",)),
        name="modulate",
    )(x, scale, shift)


def _proj_kernel(h_ref, w_ref, o_ref):
    o_ref[...] = _dot(h_ref[...], w_ref[...])


def _project(h, w):
    t, d = h.shape
    n = w.shape[1]
    tm = 2 * ROW_TILE
    return pl.pallas_call(
        _proj_kernel,
        grid=(t // tm,),
        in_specs=[pl.BlockSpec((tm, d), lambda i: (i, 0)),
                  pl.BlockSpec((d, n), lambda i: (0, 0))],
        out_specs=pl.BlockSpec((tm, n), lambda i: (i, 0)),
        out_shape=jax.ShapeDtypeStruct((t, n), F32),
        compiler_params=_params(("parallel",)),
        name="in_proj",
    )(h, w)


def _rglru_kernel(p_ref, cw_ref, cb_ref, wa_ref, ba_ref, wx_ref, bx_ref, lam_ref, o_ref, tail_ref, h_ref):
    @pl.when(pl.program_id(0) == 0)
    def _():
        tail_ref[...] = jnp.zeros_like(tail_ref)
        h_ref[...] = jnp.zeros_like(h_ref)

    p = p_ref[...]
    n = p.shape[0]
    ax = p[:, :MIX_W]
    ag = p[:, MIX_W:]
    xc = _causal_conv(ax, tail_ref, cw_ref[...], cb_ref[...])
    xb = xc.astype(BF16)
    r = _sigmoid(_dot(xb, wa_ref[...]) + ba_ref[...])
    gi = _sigmoid(_dot(xb, wx_ref[...]) + bx_ref[...])
    log_a = (-LRU_C) * r * _softplus(-lam_ref[...])
    a = jnp.exp(log_a)
    u = jnp.sqrt(-jnp.tanh(log_a) * (a * a + 1.0)) * (gi * xc)
    row = _row_iota(a.shape)
    d = 1
    while d < n:
        keep = row >= d
        a_s = pltpu.roll(a, d, 0)
        u_s = pltpu.roll(u, d, 0)
        u = jnp.where(keep, a * u_s + u, u)
        a = jnp.where(keep, a * a_s, a)
        d *= 2
    h = u + a * h_ref[0:1]
    h_ref[...] = jnp.broadcast_to(h[n - 1:n], h_ref.shape)
    o_ref[...] = (h * _gelu_tanh(ag)).astype(o_ref.dtype)


def _rglru(proj, cw, cb, wa, ba, wx, bx, lam):
    t = proj.shape[0]
    n = LRU_CHUNK
    vec = pl.BlockSpec((1, MIX_W), lambda i: (0, 0))
    sq = pl.BlockSpec((MIX_W, MIX_W), lambda i: (0, 0))
    return pl.pallas_call(
        _rglru_kernel,
        grid=(t // n,),
        in_specs=[pl.BlockSpec((n, 2 * MIX_W), lambda i: (i, 0)),
                  pl.BlockSpec((CONV_K, MIX_W), lambda i: (0, 0)), vec, sq, vec, sq, vec, vec],
        out_specs=pl.BlockSpec((n, MIX_W), lambda i: (i, 0)),
        out_shape=jax.ShapeDtypeStruct((t, MIX_W), BF16),
        scratch_shapes=[pltpu.VMEM((SUBLANES + n, MIX_W), F32), pltpu.VMEM((SUBLANES, MIX_W), F32)],
        compiler_params=_params(("arbitrary",)),
        name="rglru",
    )(proj, cw, cb, wa, ba, wx, bx, lam)


def _chunked(chunk_fn, state_refs, p_ref, o_ref, *refs):
    @pl.when(pl.program_id(0) == 0)
    def _():
        for s in state_refs:
            s[...] = jnp.zeros_like(s)

    for sub in range(p_ref.shape[0] // SCAN_CHUNK):
        rows = pl.ds(sub * SCAN_CHUNK, SCAN_CHUNK)
        chunk_fn(p_ref.at[rows], *refs, o_ref.at[rows], *state_refs)


def _ssd_kernel(p_ref, cw_ref, cb_ref, dtbn_ref, an_ref, dx_ref, nw_ref, sel_ref, o_ref, tail_ref, st_ref):
    _chunked(_ssd_chunk, (tail_ref, st_ref), p_ref, o_ref,
             cw_ref, cb_ref, dtbn_ref, an_ref, dx_ref, nw_ref, sel_ref)


def _ssd_chunk(p_ref, cw_ref, cb_ref, dtbn_ref, an_ref, dx_ref, nw_ref, sel_ref, o_ref, tail_ref, st_ref):
    p = p_ref[...]
    n = p.shape[0]
    gw = SSD_HPG * SSD_P
    z = p[:, 0:MIX_W]
    xbc_raw = p[:, MIX_W:3 * MIX_W]
    dtn_raw = p[:, 3 * MIX_W:3 * MIX_W + LANES]
    xbc = _silu(_causal_conv(xbc_raw, tail_ref, cw_ref[...], cb_ref[...]))
    xv = xbc[:, :MIX_W]
    bm = xbc[:, MIX_W:MIX_W + SSD_G * SSD_N]
    cm = xbc[:, MIX_W + SSD_G * SSD_N:]
    dtn = _softplus(dtn_raw + dtbn_ref[...])
    csn = _cumsum_rows(dtn * an_ref[...])
    csn_t = jnp.transpose(csn)
    sel = sel_ref[...]
    csx = _expand_lanes(csn, sel)
    xdt = xv * _expand_lanes(dtn, sel)
    tril = _row_iota((n, n)) >= lax.broadcasted_iota(I32, (n, n), 1)
    lane_head = lax.shift_right_logical(lax.broadcasted_iota(I32, (n, gw), 1), SSD_P.bit_length() - 1)
    ys = []
    for g in range(SSD_G):
        seg = slice(g * gw, (g + 1) * gw)
        cg = cm[:, g * SSD_N:(g + 1) * SSD_N].astype(BF16)
        bg = bm[:, g * SSD_N:(g + 1) * SSD_N]
        scores = _dot_nt(cg, bg.astype(BF16))
        csg = csx[:, seg]
        xdt_g = xdt[:, seg]
        xdt_gb = xdt_g.astype(BF16)
        st = st_ref[g]
        y = _dot(cg, st.astype(BF16)) * jnp.exp(csg)
        for h in range(SSD_HPG):
            hh = g * SSD_HPG + h
            col = csx[:, hh * SSD_P:hh * SSD_P + 1]
            dec = jnp.exp(jnp.where(tril, col - csn_t[hh:hh + 1, :], NEG_BIG))
            y = y + jnp.where(lane_head == h, _dot((scores * dec).astype(BF16), xdt_gb), 0.0)
        last = csg[n - 1:n]
        xw = (xdt_g * jnp.exp(last - csg)).astype(BF16)
        st_ref[g] = jnp.exp(last) * st + _dot(jnp.transpose(bg).astype(BF16), xw)
        ys.append(y)
    y = jnp.concatenate(ys, axis=1) + dx_ref[...] * xv
    yz = y * _silu(z)
    ms = jnp.mean(yz * yz, axis=1, keepdims=True)
    o_ref[...] = (yz * lax.rsqrt(ms + EPS) * nw_ref[...]).astype(o_ref.dtype)


def _ssd(proj, cw, cb, dtb_n, a_n, d_x, nw):
    t = proj.shape[0]
    n = MIXER_ROWS
    cc = 2 * MIX_W
    vec = lambda w: pl.BlockSpec((1, w), lambda i: (0, 0))
    sel = _lane_selector(SSD_HEADS, SSD_P)
    return pl.pallas_call(
        _ssd_kernel,
        grid=(t // n,),
        in_specs=[pl.BlockSpec((n, B_COLS), lambda i: (i, 0)),
                  pl.BlockSpec((CONV_K, cc), lambda i: (0, 0)), vec(cc),
                  vec(LANES), vec(LANES), vec(MIX_W), vec(MIX_W),
                  pl.BlockSpec(sel.shape, lambda i: (0, 0))],
        out_specs=pl.BlockSpec((n, MIX_W), lambda i: (i, 0)),
        out_shape=jax.ShapeDtypeStruct((t, MIX_W), BF16),
        scratch_shapes=[pltpu.VMEM((SUBLANES + SCAN_CHUNK, cc), F32),
                        pltpu.VMEM((SSD_G, SSD_N, SSD_HPG * SSD_P), F32)],
        compiler_params=_params(("arbitrary",)),
        name="ssd",
    )(proj, cw, cb, dtb_n, a_n, d_x, nw, sel)


def _cummax_rows(x):
    n = x.shape[0]
    row = _row_iota(x.shape)
    d = 1
    while d < n:
        x = jnp.maximum(x, jnp.where(row >= d, pltpu.roll(x, d, 0), NEG_BIG))
        d *= 2
    return x


def _lane_selector(heads, width):
    k = lax.broadcasted_iota(I32, (LANES, heads * width), 0)
    j = lax.broadcasted_iota(I32, (LANES, heads * width), 1)
    return (k == j // width).astype(BF16)


def _mlstm_kernel(p_ref, cw_ref, cb_ref, gb_ref, nw_ref, sel_ref, o_ref, tail_ref, c_ref, n_ref, m_ref):
    _chunked(_mlstm_chunk, (tail_ref, c_ref, n_ref, m_ref), p_ref, o_ref,
             cw_ref, cb_ref, gb_ref, nw_ref, sel_ref)


def _mlstm_chunk(p_ref, cw_ref, cb_ref, gb_ref, nw_ref, sel_ref, o_ref, tail_ref, c_ref, n_ref, m_ref):
    p = p_ref[...]
    n = p.shape[0]
    dh = MLSTM_DH
    qk_raw = p[:, :2 * MIX_W]
    qk = _silu(_causal_conv(qk_raw, tail_ref, cw_ref[...], cb_ref[...]))
    v_all = p[:, 2 * MIX_W:3 * MIX_W]
    pre = p[:, 3 * MIX_W:3 * MIX_W + LANES] + gb_ref[...]
    og_all = _sigmoid(p[:, 3 * MIX_W + LANES:])
    nw_all = nw_ref[...]
    fpre = pltpu.roll(pre, LANES - MLSTM_HEADS, 1)
    cs_n = _cumsum_rows(jnp.minimum(fpre, 0.0) - _log1p_exp_neg_abs(fpre))
    r_n = pre - cs_n
    rt = jnp.transpose(r_n)
    sel = sel_ref[...]
    cs_b = _expand_lanes(cs_n, sel)
    ig_b = _expand_lanes(pre, sel)
    cm_b = _expand_lanes(_cummax_rows(r_n), sel)
    tril = _row_iota((n, n)) >= lax.broadcasted_iota(I32, (n, n), 1)
    m_old = [m_ref[h:h + 1, :] for h in range(MLSTM_HEADS)]
    c_old = [c_ref[h] for h in range(MLSTM_HEADS)]
    n_old = [n_ref[h:h + 1, :] for h in range(MLSTM_HEADS)]
    c_new, n_new, m_news, outs = [], [], [], []
    for h in range(MLSTM_HEADS):
        sl = slice(h * dh, (h + 1) * dh)
        q = qk[:, sl] * (dh ** -0.5)
        k = qk[:, MIX_W + h * dh:MIX_W + (h + 1) * dh]
        v = v_all[:, sl]
        qb = q.astype(BF16)
        vb = v.astype(BF16)
        cs = cs_b[:, sl]
        ig = ig_b[:, sl]
        m_prev = m_old[h]
        inter = cs + m_prev
        m_row = jnp.maximum(cs + cm_b[:, sl], inter)
        dexp = jnp.where(tril, (cs - m_row) + rt[h:h + 1, :], NEG_BIG)
        w = jnp.exp(dexp) * _dot_nt(qb, k.astype(BF16))
        w_inter = jnp.exp(inter - m_row)
        cmem = c_old[h]
        nmem = n_old[h]
        num = _dot(w.astype(BF16), vb) + w_inter * _dot(qb, cmem.astype(BF16))
        den = jnp.sum(w, axis=1, keepdims=True) + w_inter * jnp.sum(q * nmem, axis=1, keepdims=True)
        hs = num / jnp.maximum(jnp.abs(den), jnp.exp(-m_row))
        g_tot = cs[n - 1:n]
        s_end = g_tot - cs + ig
        m_new = jnp.maximum(g_tot + m_prev, jnp.max(s_end, axis=0, keepdims=True))
        kw = k * jnp.exp(s_end - m_new)
        decay = jnp.exp(g_tot + m_prev - m_new)
        c_new.append(decay * cmem + _dot(jnp.transpose(kw).astype(BF16), vb))
        n_new.append(decay * nmem + jnp.sum(kw, axis=0, keepdims=True))
        m_news.append(jnp.broadcast_to(m_new, (1, LANES)))
        ms = jnp.mean(hs * hs, axis=1, keepdims=True)
        outs.append((og_all[:, sl] * (hs * lax.rsqrt(ms + EPS) * nw_all[:, sl])).astype(o_ref.dtype))
    for h in range(MLSTM_HEADS):
        c_ref[h] = c_new[h]
        n_ref[h:h + 1, :] = n_new[h]
        m_ref[h:h + 1, :] = m_news[h]
        o_ref[:, h * dh:(h + 1) * dh] = outs[h]


def _mlstm(proj, cw, cb, gate_bias, nw):
    t = proj.shape[0]
    n = SCAN_CHUNK
    assert n == LANES
    cc = 2 * MIX_W
    vec = lambda w: pl.BlockSpec((1, w), lambda i: (0, 0))
    return pl.pallas_call(
        _mlstm_kernel,
        grid=(t // n,),
        in_specs=[pl.BlockSpec((n, C_COLS), lambda i: (i, 0)),
                  pl.BlockSpec((CONV_K, cc), lambda i: (0, 0)), vec(cc), vec(LANES), vec(MIX_W),
                  pl.BlockSpec((LANES, MLSTM_HEADS * LANES), lambda i: (0, 0))],
        out_specs=pl.BlockSpec((n, MIX_W), lambda i: (i, 0)),
        out_shape=jax.ShapeDtypeStruct((t, MIX_W), BF16),
        scratch_shapes=[pltpu.VMEM((SUBLANES + SCAN_CHUNK, cc), F32),
                        pltpu.VMEM((MLSTM_HEADS, MLSTM_DH, MLSTM_DH), F32),
                        pltpu.VMEM((SUBLANES, MLSTM_DH), F32),
                        pltpu.VMEM((SUBLANES, LANES), F32)],
        compiler_params=_params(("arbitrary",)),
        name="mlstm",
    )(proj, cw, cb, gate_bias, nw, _lane_selector(MLSTM_HEADS, LANES))


def _hgrn_kernel(p_ref, lb_ref, nw_ref, o_ref, st_ref):
    _chunked(_hgrn_chunk, (st_ref,), p_ref, o_ref, lb_ref, nw_ref)


def _hgrn_chunk(p_ref, lb_ref, nw_ref, o_ref, st_ref):
    p = p_ref[...]
    n = p.shape[0]
    dk = HGRN_DK
    q = _silu(p[:, :MIX_W])
    u = p[:, MIX_W:2 * MIX_W]
    v = p[:, 2 * MIX_W:3 * MIX_W]
    gt = p[:, 3 * MIX_W:]
    lb = lb_ref[...]
    f = lb + (1.0 - lb) * _sigmoid(u)
    k = (1.0 - lb) * _sigmoid(-u)
    cs = _cumsum_rows(jnp.log(jnp.maximum(f, 1e-30)))
    last = cs[n - 1:n]
    rowi = _row_iota((n, 1))
    vb = v.astype(BF16)

    scores = [None] * HGRN_HEADS
    rr = _row_iota((n, n))
    cc = lax.broadcasted_iota(I32, (n, n), 1)
    hs = n // 2
    while hs >= HGRN_DIAG:
        blk = 2 * hs
        nb = n // blk
        if blk > SUBLANES:
            ref = jnp.concatenate([jnp.broadcast_to(cs[b * blk + hs - 1:b * blk + hs], (blk, MIX_W))
                                   for b in range(nb)], axis=0)
        else:
            c3 = cs.reshape(n // SUBLANES, SUBLANES, MIX_W)
            sub = lax.broadcasted_iota(I32, c3.shape, 1)
            ref = None
            for j in reversed(range(SUBLANES // blk)):
                rj = jnp.broadcast_to(c3[:, j * blk + hs - 1:j * blk + hs, :], c3.shape)
                ref = rj if ref is None else jnp.where(sub < (j + 1) * blk, rj, ref)
            ref = ref.reshape(n, MIX_W)
        upper = (rowi & (blk - 1)) >= hs
        qj = jnp.where(upper, q * jnp.exp(jnp.minimum(cs - ref, 0.0)), 0.0).astype(BF16)
        kj = jnp.where(upper, 0.0, k * jnp.exp(jnp.minimum(ref - cs, 0.0))).astype(BF16)
        same = (rr & -blk) == (cc & -blk)
        for h in range(HGRN_HEADS):
            sl = slice(h * dk, (h + 1) * dk)
            s = _dot_nt(qj[:, sl], kj[:, sl])
            if nb > 1:
                s = jnp.where(same, s, 0.0)
            scores[h] = s if scores[h] is None else scores[h] + s
        hs //= 2

    band = rr - cc
    for dlt in range(HGRN_DIAG):
        if dlt == 0:
            t_all = q * k
        else:
            kd = pltpu.roll(k, dlt, 0)
            csd = pltpu.roll(cs, dlt, 0)
            t_all = q * kd * jnp.exp(jnp.minimum(cs - csd, 0.0))
        valid = (rowi & (HGRN_DIAG - 1)) >= dlt
        for h in range(HGRN_HEADS):
            sl = slice(h * dk, (h + 1) * dk)
            s = jnp.where(valid, jnp.sum(t_all[:, sl], axis=1, keepdims=True), 0.0)
            scores[h] = scores[h] + jnp.where(band == dlt, s, 0.0)

    qe = (q * jnp.exp(cs)).astype(BF16)
    ys = []
    for h in range(HGRN_HEADS):
        sl = slice(h * dk, (h + 1) * dk)
        ys.append(_dot(scores[h].astype(BF16), vb[:, sl]) + _dot_nt(qe[:, sl], st_ref[h].astype(BF16)))

    kw = k * jnp.exp(last - cs)
    elast = jnp.exp(last)
    for h in range(HGRN_HEADS):
        sl = slice(h * dk, (h + 1) * dk)
        st_ref[h] = elast[:, sl] * st_ref[h] + _dot(jnp.transpose(v[:, sl]).astype(BF16), kw[:, sl].astype(BF16))
        y = ys[h]
        ms = jnp.mean(y * y, axis=1, keepdims=True)
        o_ref[:, sl] = (y * lax.rsqrt(ms + EPS) * nw_ref[:, sl] * _silu(gt[:, sl])).astype(o_ref.dtype)


def _hgrn(proj, lb, nw):
    t = proj.shape[0]
    n = MIXER_ROWS
    vec = pl.BlockSpec((1, MIX_W), lambda i: (0, 0))
    return pl.pallas_call(
        _hgrn_kernel,
        grid=(t // n,),
        in_specs=[pl.BlockSpec((n, 4 * MIX_W), lambda i: (i, 0)), vec, vec],
        out_specs=pl.BlockSpec((n, MIX_W), lambda i: (i, 0)),
        out_shape=jax.ShapeDtypeStruct((t, MIX_W), BF16),
        scratch_shapes=[pltpu.VMEM((HGRN_HEADS, HGRN_DK, HGRN_DK), F32)],
        compiler_params=_params(("arbitrary",)),
        name="hgrn2",
    )(proj, lb, nw)


def _merge_kernel(h_ref, ba_ref, bb_ref, bc_ref, bd_ref, wg_ref, bg_ref, wu_ref, o_ref, wg_s, wu_s):
    @pl.when(pl.program_id(1) == 0)
    def _():
        wg_s[...] = wg_ref[0].astype(BF16)
        wu_s[...] = wu_ref[0].astype(BF16)

    h = h_ref[...]
    acc = None
    for br, b_ref in enumerate((ba_ref, bb_ref, bc_ref, bd_ref)):
        gate = _sigmoid(_dot(h, wg_s[br]) + bg_ref[br])
        term = gate * _dot(b_ref[...], wu_s[br])
        acc = term if acc is None else acc + term
    o_ref[...] = acc.astype(o_ref.dtype)


def _merge(layer, h, branches, wg, bg, wu):
    t, d = h.shape
    tm, tn = 1024, 256
    nb = len(branches)
    bspec = pl.BlockSpec((tm, MIX_W), lambda j, i: (i, 0))
    return pl.pallas_call(
        _merge_kernel,
        grid=(d // tn, t // tm),
        in_specs=[pl.BlockSpec((tm, d), lambda j, i: (i, 0)), bspec, bspec, bspec, bspec,
                  pl.BlockSpec((1, nb, d, tn), lambda j, i: (layer, 0, 0, j)),
                  pl.BlockSpec((nb, 1, tn), lambda j, i: (0, 0, j)),
                  pl.BlockSpec((1, nb, MIX_W, tn), lambda j, i: (layer, 0, 0, j))],
        out_specs=pl.BlockSpec((tm, tn), lambda j, i: (i, j)),
        out_shape=jax.ShapeDtypeStruct((t, d), BF16),
        scratch_shapes=[pltpu.VMEM((nb, d, tn), BF16), pltpu.VMEM((nb, MIX_W, tn), BF16)],
        compiler_params=_params(("parallel", "arbitrary")),
        name="merge",
    )(h, *branches, wg, bg, wu)


def _layer_norm_rows(z, w, b):
    mu = jnp.mean(z, axis=1, keepdims=True)
    zc = z - mu
    var = jnp.mean(zc * zc, axis=1, keepdims=True)
    return zc * lax.rsqrt(var + EPS) * w + b


def _pack_bf16_pairs(h):
    half = h.shape[1] // 2
    lo = lax.bitcast_convert_type(h[:, :half].astype(BF16).astype(F32), jnp.uint32)
    hi = lax.bitcast_convert_type(h[:, half:].astype(BF16).astype(F32), jnp.uint32)
    return lax.shift_right_logical(lo, jnp.uint32(16)) | (hi & jnp.uint32(0xFFFF0000))


def _unpack_bf16_pairs(w):
    lo = lax.bitcast_convert_type(lax.shift_left(w, jnp.uint32(16)), F32)
    hi = lax.bitcast_convert_type(w & jnp.uint32(0xFFFF0000), F32)
    return jnp.concatenate([lo, hi], axis=1).astype(BF16)


def _out_ln_route_kernel(m_ref, w_ref, x_ref, g_ref, lw_ref, lb_ref, sc_ref, sh_ref, rw_ref, rb_ref,
                         x_out, hp_out, r_out, cnt_out, run_ref):
    @pl.when(pl.program_id(0) == 0)
    def _():
        run_ref[...] = jnp.zeros_like(run_ref)

    y = _dot(m_ref[...], w_ref[...])
    xn = _layer_norm_rows(ALPHA * x_ref[...] + g_ref[...] * y, lw_ref[...], lb_ref[...])
    x_out[...] = xn
    h = xn * (1.0 + sc_ref[...]) + sh_ref[...]
    hp_out[...] = _pack_bf16_pairs(h)
    route, new1, new2 = _route_rows(h, rw_ref, rb_ref, run_ref[...])
    r_out[...] = route
    run_ref[0:1] = new1
    run_ref[1:2] = new2
    cnt_out[...] = run_ref[...]


def _out_ln_route(merged, w_out, x, gate, ln_w, ln_b, scale_next, shift_next, rw, rb):
    t, d = x.shape
    tm = LN_TILE
    vec = pl.BlockSpec((1, d), lambda i: (0, 0))
    row = pl.BlockSpec((tm, d), lambda i: (i, 0))
    rw_hi = rw.astype(BF16)
    return pl.pallas_call(
        _out_ln_route_kernel,
        grid=(t // tm,),
        in_specs=[row, pl.BlockSpec((d, d), lambda i: (0, 0)), row, vec, vec, vec, vec, vec,
                  pl.BlockSpec((2, d, LANES), lambda i: (0, 0, 0)),
                  pl.BlockSpec((1, LANES), lambda i: (0, 0))],
        out_specs=[row, pl.BlockSpec((tm, d // 2), lambda i: (i, 0)),
                   pl.BlockSpec((tm, LANES), lambda i: (i, 0)),
                   pl.BlockSpec((SUBLANES, LANES), lambda i: (0, 0))],
        out_shape=[jax.ShapeDtypeStruct((t, d), F32), jax.ShapeDtypeStruct((t, d // 2), jnp.uint32),
                   jax.ShapeDtypeStruct((t, LANES), F32), jax.ShapeDtypeStruct((SUBLANES, LANES), F32)],
        scratch_shapes=[pltpu.VMEM((SUBLANES, LANES), F32)],
        compiler_params=_params(("arbitrary",)),
        name="out_proj_ln_route",
    )(merged, w_out, x, gate, ln_w, ln_b, scale_next, shift_next,
      jnp.stack([rw_hi, (rw - rw_hi.astype(F32)).astype(BF16)]), rb)


def _route_rows(h, w_ref, b_ref, run):
    n = h.shape[0]
    h_hi = h.astype(BF16)
    h_lo = (h - h_hi.astype(F32)).astype(BF16)
    logits = _dot(h_hi, w_ref[0]) + _dot(h_lo, w_ref[0]) + _dot(h_hi, w_ref[1]) + b_ref[...]
    lane = lax.broadcasted_iota(I32, logits.shape, 1)
    lane_f = lane.astype(F32)
    big = float(LANES)
    gl = jnp.where(lane < N_GROUPS, logits, NEG_BIG)
    ge = jnp.exp(gl - jnp.max(gl, axis=1, keepdims=True))
    gp = ge / jnp.sum(ge, axis=1, keepdims=True)
    p_grp = jnp.max(gp, axis=1, keepdims=True)
    grp = jnp.min(jnp.where(gp == p_grp, lane_f, big), axis=1, keepdims=True)
    lo = N_GROUPS + EXP_PER_GROUP * grp
    el = jnp.where((lane_f >= lo) & (lane_f < lo + EXP_PER_GROUP), logits, NEG_BIG)
    v1 = jnp.max(el, axis=1, keepdims=True)
    i1 = jnp.min(jnp.where(el == v1, lane_f, big), axis=1, keepdims=True)
    el2 = jnp.where(lane_f == i1, NEG_BIG, el)
    v2 = jnp.max(el2, axis=1, keepdims=True)
    i2 = jnp.min(jnp.where(el2 == v2, lane_f, big), axis=1, keepdims=True)
    e2 = jnp.exp(v2 - v1)
    w1 = (1.0 / (1.0 + e2)) * p_grp
    w2 = (e2 / (1.0 + e2)) * p_grp
    oh1 = jnp.where(lane_f == i1, 1.0, 0.0)
    oh2 = jnp.where(lane_f == i2, 1.0, 0.0)
    strict = (_row_iota((n, n)) > lax.broadcasted_iota(I32, (n, n), 1)).astype(BF16)
    c1 = _dot(strict, oh1.astype(BF16)) + run[0:1]
    c2 = _dot(strict, oh2.astype(BF16)) + run[1:2]
    r1 = jnp.sum(oh1 * c1, axis=1, keepdims=True)
    r2 = jnp.sum(oh2 * c2, axis=1, keepdims=True)
    new1 = run[0:1] + jnp.sum(oh1, axis=0, keepdims=True)
    new2 = run[1:2] + jnp.sum(oh2, axis=0, keepdims=True)
    out = jnp.where(lane == 0, i1 - N_GROUPS, 0.0)
    out = jnp.where(lane == 1, i2 - N_GROUPS, out)
    out = jnp.where(lane == 2, w1, out)
    out = jnp.where(lane == 3, w2, out)
    out = jnp.where(lane == 4, r1, out)
    out = jnp.where(lane == 5, r2, out)
    return out, new1, new2


def _dispatch_kernel(zflag_ref, dest_ref, h_ref, xs_ref, zbuf, sem, zsem):
    nt = h_ref.shape[0]
    bm = zbuf.shape[0]

    @pl.when(pl.program_id(0) == 0)
    def _():
        zbuf[...] = jnp.zeros_like(zbuf)

        def zblock(b):
            return pltpu.make_async_copy(zbuf, xs_ref.at[pl.ds(pl.multiple_of(b * bm, bm), bm)], zsem)

        def start_block(b, carry):
            @pl.when(zflag_ref[b] != 0)
            def _():
                zblock(b).start()
            return carry

        def wait_block(b, carry):
            @pl.when(zflag_ref[b] != 0)
            def _():
                zblock(b).wait()
            return carry

        n_blocks = xs_ref.shape[0] // bm
        lax.fori_loop(0, n_blocks, start_block, 0)
        lax.fori_loop(0, n_blocks, wait_block, 0)

    def copy(r, j):
        return pltpu.make_async_copy(h_ref.at[pl.ds(r, 1)],
                                     xs_ref.at[pl.ds(dest_ref[0, 0, 2 * r + j], 1)], sem)

    def issue(r, carry):
        copy(r, 0).start()
        copy(r, 1).start()
        return carry

    def drain(r, carry):
        copy(r, 0).wait()
        copy(r, 1).wait()
        return carry

    lax.fori_loop(0, nt, issue, 0, unroll=8)
    lax.fori_loop(0, nt, drain, 0, unroll=8)


def _dispatch(h, dest, zflag, n_slots):
    t, w = h.shape
    nt = DISPATCH_TILE
    grid_spec = pltpu.PrefetchScalarGridSpec(
        num_scalar_prefetch=1,
        grid=(t // nt,),
        in_specs=[pl.BlockSpec((1, 1, 2 * nt), lambda i, zf: (i, 0, 0), memory_space=pltpu.SMEM),
                  pl.BlockSpec((nt, w), lambda i, zf: (i, 0))],
        out_specs=pl.BlockSpec(memory_space=pl.ANY),
        scratch_shapes=[pltpu.VMEM((MOE_BM, w), h.dtype), pltpu.SemaphoreType.DMA(()),
                        pltpu.SemaphoreType.DMA(())],
    )
    return pl.pallas_call(
        _dispatch_kernel,
        grid_spec=grid_spec,
        out_shape=jax.ShapeDtypeStruct((n_slots, w), h.dtype),
        compiler_params=pltpu.CompilerParams(dimension_semantics=("arbitrary",)),
        name="moe_dispatch",
    )(zflag, dest.reshape(t // nt, 1, 2 * nt), h)


def _expert_kernel(layer, be_ref, nu_ref, first_ref, gidx_ref, nxt_ref, xs_ref, wg_hbm, wu_hbm, wd_hbm, ys_ref,
                   wg_f, wu_f, wd_f, wg_s, wu_s, wd_s, sem):
    b = pl.program_id(0)
    used = b < nu_ref[0]

    def fetch(e, slot):
        return [pltpu.make_async_copy(src.at[layer, e], dst.at[slot], sem.at[slot, k])
                for k, (src, dst) in enumerate(((wg_hbm, wg_f), (wu_hbm, wu_f), (wd_hbm, wd_f)))]

    @pl.when(b == 0)
    def _():
        for c in fetch(be_ref[0], 0):
            c.start()

    @pl.when(used & (first_ref[b] != 0))
    def _():
        slot = lax.rem(gidx_ref[b], 2)
        for c in fetch(be_ref[b], slot):
            c.wait()
        wg_s[...] = wg_f[slot].astype(BF16)
        wu_s[...] = wu_f[slot].astype(BF16)
        wd_s[...] = wd_f[slot].astype(BF16)

        @pl.when(nxt_ref[b] >= 0)
        def _():
            for c in fetch(nxt_ref[b], 1 - slot):
                c.start()

    @pl.when(used)
    def _():
        rows = _unpack_bf16_pairs(xs_ref[...])
        hid = _silu(_dot(rows, wg_s[...])) * _dot(rows, wu_s[...])
        ys_ref[...] = _dot(hid.astype(BF16), wd_s[...])

    @pl.when(jnp.logical_not(used))
    def _():
        ys_ref[...] = jnp.zeros_like(ys_ref)


def _experts(layer, blk_e, n_used, xs, w_g, w_u, w_d):
    n_slots = xs.shape[0]
    d, f = w_g.shape[2], w_g.shape[3]
    bm = MOE_BM
    n_blocks = n_slots // bm
    blk = jnp.arange(n_blocks, dtype=I32)
    used = blk < n_used[0]
    first = (used & ((blk == 0) | (blk_e != jnp.concatenate([blk_e[:1], blk_e[:-1]])))).astype(I32)
    gidx = jnp.cumsum(first) - 1
    nxt_pos = lax.cummin(jnp.where(first != 0, blk, n_blocks), reverse=True)
    nxt_pos = jnp.concatenate([nxt_pos[1:], jnp.full((1,), n_blocks, I32)])
    nxt = jnp.where(nxt_pos < n_blocks, blk_e[jnp.minimum(nxt_pos, n_blocks - 1)], -1).astype(I32)
    grid_spec = pltpu.PrefetchScalarGridSpec(
        num_scalar_prefetch=5,
        grid=(n_blocks,),
        in_specs=[pl.BlockSpec((bm, d // 2), lambda b, *_: (b, 0)),
                  pl.BlockSpec(memory_space=pl.ANY), pl.BlockSpec(memory_space=pl.ANY),
                  pl.BlockSpec(memory_space=pl.ANY)],
        out_specs=pl.BlockSpec((bm, d), lambda b, *_: (b, 0)),
        scratch_shapes=[pltpu.VMEM((2, d, f), F32), pltpu.VMEM((2, d, f), F32), pltpu.VMEM((2, f, d), F32),
                        pltpu.VMEM((d, f), BF16), pltpu.VMEM((d, f), BF16), pltpu.VMEM((f, d), BF16),
                        pltpu.SemaphoreType.DMA((2, 3))],
    )
    return pl.pallas_call(
        functools.partial(_expert_kernel, layer),
        grid_spec=grid_spec,
        out_shape=jax.ShapeDtypeStruct((n_slots, d), F32),
        compiler_params=_params(("arbitrary",)),
        name="moe_experts",
    )(blk_e, n_used, first, gidx, nxt, xs, w_g, w_u, w_d)


def _combine_ln_kernel(dest_ref, dnext_ref, ys_ref, r_ref, x_ref, g_ref, lw_ref, lb_ref, sc_ref, sh_ref,
                       x_out, h_out, buf, sem):
    nt = buf.shape[2]
    i = pl.program_id(0)
    slot = lax.rem(i, 2)

    def copy(d_ref, s, r, j):
        return pltpu.make_async_copy(ys_ref.at[pl.ds(d_ref[0, 0, 2 * r + j], 1)],
                                     buf.at[s, j, pl.ds(r, 1)], sem.at[s])

    def gather(d_ref, s):
        def issue(r, carry):
            copy(d_ref, s, r, 0).start()
            copy(d_ref, s, r, 1).start()
            return carry
        lax.fori_loop(0, nt, issue, 0, unroll=8)

    @pl.when(i == 0)
    def _():
        gather(dest_ref, 0)

    @pl.when(i + 1 < pl.num_programs(0))
    def _():
        gather(dnext_ref, 1 - slot)

    def drain(r, carry):
        copy(dest_ref, slot, r, 0).wait()
        copy(dest_ref, slot, r, 1).wait()
        return carry

    lax.fori_loop(0, nt, drain, 0, unroll=8)
    rt = r_ref[...]
    y = buf[slot, 0] * rt[:, 2:3] + buf[slot, 1] * rt[:, 3:4]
    xn = _layer_norm_rows(ALPHA * x_ref[...] + g_ref[...] * y, lw_ref[...], lb_ref[...])
    x_out[...] = xn
    h_out[...] = (xn * (1.0 + sc_ref[...]) + sh_ref[...]).astype(h_out.dtype)


def _combine_ln(dest, ys, route, x, gate, ln_w, ln_b, scale_next, shift_next):
    t, d = x.shape
    tm = LN_TILE
    vec = pl.BlockSpec((1, d), lambda i: (0, 0))
    row = pl.BlockSpec((tm, d), lambda i: (i, 0))
    nsteps = t // tm
    dest3 = dest.reshape(nsteps, 1, 2 * tm)
    return pl.pallas_call(
        _combine_ln_kernel,
        grid=(nsteps,),
        in_specs=[pl.BlockSpec((1, 1, 2 * tm), lambda i: (i, 0, 0), memory_space=pltpu.SMEM),
                  pl.BlockSpec((1, 1, 2 * tm), lambda i: (jnp.minimum(i + 1, nsteps - 1), 0, 0),
                               memory_space=pltpu.SMEM),
                  pl.BlockSpec(memory_space=pl.ANY),
                  pl.BlockSpec((tm, LANES), lambda i: (i, 0)),
                  row, vec, vec, vec, vec, vec],
        out_specs=[row, row],
        out_shape=[jax.ShapeDtypeStruct((t, d), F32), jax.ShapeDtypeStruct((t, d), BF16)],
        scratch_shapes=[pltpu.VMEM((2, 2, tm, d), F32), pltpu.SemaphoreType.DMA((2,))],
        compiler_params=_params(("arbitrary",)),
        name="moe_combine_ln",
    )(dest3, dest3, ys, route, x, gate, ln_w, ln_b, scale_next, shift_next)


def _slot_kernel(r_ref, tab_ref, o_ref):
    rt = r_ref[...]
    lane = lax.broadcasted_iota(I32, rt.shape, 1)
    lane_f = lane.astype(F32)
    tab = tab_ref[...]
    d1 = jnp.sum(jnp.where(lane_f == rt[:, 0:1] + N_GROUPS, tab[0:1], 0.0), axis=1, keepdims=True) + rt[:, 4:5]
    d2 = jnp.sum(jnp.where(lane_f == rt[:, 1:2] + N_GROUPS, tab[1:2], 0.0), axis=1, keepdims=True) + rt[:, 5:6]
    o_ref[...] = jnp.where(lane == 0, d1, jnp.where(lane == 1, d2, 0.0)).astype(I32)


def _moe_plan(route, counts):
    bm = MOE_BM
    t = route.shape[0]
    tm = 1024
    n_blocks = (2 * t) // bm + N_EXPERTS
    c1 = counts[0].astype(I32)
    c2 = counts[1].astype(I32)
    pcounts = (c1 + c2 + bm - 1) // bm * bm
    pends = jnp.cumsum(pcounts)
    pstart = pends - pcounts
    tab = jnp.zeros((SUBLANES, LANES), F32).at[0].set(pstart.astype(F32)).at[1].set((pstart + c1).astype(F32))
    slots = pl.pallas_call(
        _slot_kernel,
        grid=(t // tm,),
        in_specs=[pl.BlockSpec((tm, LANES), lambda i: (i, 0)),
                  pl.BlockSpec((SUBLANES, LANES), lambda i: (0, 0))],
        out_specs=pl.BlockSpec((tm, LANES), lambda i: (i, 0)),
        out_shape=jax.ShapeDtypeStruct((t, LANES), I32),
        compiler_params=_params(("parallel",)),
        name="moe_slots",
    )(route, tab)
    dest = slots[:, :2]
    ends = pends[N_GROUPS:N_GROUPS + N_EXPERTS]
    starts_of_blocks = jnp.arange(n_blocks, dtype=I32) * bm
    blk_e = jnp.minimum(jnp.sum((ends[None, :] <= starts_of_blocks[:, None]).astype(I32), axis=1),
                        N_EXPERTS - 1).astype(I32)
    n_used = (pends[-1] // bm).astype(I32).reshape(1)
    blk = jnp.arange(n_blocks, dtype=I32)
    next_e = jnp.concatenate([blk_e[1:], blk_e[-1:]])
    zflag = ((blk >= n_used[0] - 1) | (next_e != blk_e)).astype(I32)
    return dest, blk_e, n_used, zflag, n_blocks * bm


def _block_diag(w):
    nb, bw, _ = w.shape
    eye = jnp.eye(nb, dtype=w.dtype)
    return (eye[:, None, :, None] * w[:, :, None, :]).reshape(nb * bw, nb * bw)


def _pad_lanes(v, width=LANES):
    return jnp.pad(v, [(0, 0)] * (v.ndim - 1) + [(0, width - v.shape[-1])])


def _split_w_in(w):
    o = 0
    parts = []
    for s in (MIX_W, MIX_W, MIX_W, MIX_W, SSD_G * SSD_N, SSD_G * SSD_N, SSD_HEADS, MIX_W, MIX_W, MIX_W,
              MLSTM_HEADS, MLSTM_HEADS, MIX_W, MIX_W, MIX_W, MIX_W, MIX_W):
        parts.append(w[:, o:o + s])
        o += s
    a_x, a_g, b_z, b_x, b_b, b_c, b_dt, c_q, c_k, c_v, c_i, c_f, c_o, d_q, d_f, d_i, d_g = parts
    wa = jnp.concatenate([a_x, a_g], axis=1)
    wb = jnp.concatenate([b_z, b_x, b_b, b_c, _pad_lanes(b_dt)], axis=1)
    wc = jnp.concatenate([c_q, c_k, c_v, _pad_lanes(jnp.concatenate([c_i, c_f], axis=1)), c_o], axis=1)
    wd = jnp.concatenate([d_q, d_f, d_i, d_g], axis=1)
    return [m.astype(BF16) for m in (wa, wb, wc, wd)]


def kernel(x, c, ada_w, ada_b, w_in, lru_conv_w, lru_conv_b, lru_wa, lru_ba, lru_wx, lru_bx, lru_lambda, ssd_conv_w, ssd_conv_b, ssd_dt_bias, ssd_a_log, ssd_d, ssd_norm_w, mlstm_conv_w, mlstm_conv_b, mlstm_i_bias, mlstm_f_bias, mlstm_norm_w, hgrn_lb_param, hgrn_norm_w, w_up, w_gate, b_gate, w_out, ln1_w, ln1_b, router_group_w, router_group_b, router_expert_w, router_expert_b, exp_w_gate, exp_w_up, exp_w_down, ln2_w, ln2_b):
    bsz, seq, d = x.shape
    depth = w_in.shape[0]
    assert bsz == 1 and d == D_MODEL
    xs = x.reshape(seq, d)
    ada = _ada_table(c, ada_w, ada_b)
    lb_all = _lb_table(hgrn_lb_param)
    row = lambda v: v.reshape(1, -1)

    def mod_vec(l, k):
        return ada[l, k * d:(k + 1) * d].reshape(1, d)

    h = _modulate(xs, mod_vec(0, 1), mod_vec(0, 0))
    for l in range(depth):
        wa, wb, wc, wd = _split_w_in(w_in[l])
        br_a = _rglru(_project(h, wa), lru_conv_w[l], row(lru_conv_b[l]),
                      _block_diag(lru_wa[l]).astype(BF16), row(lru_ba[l]),
                      _block_diag(lru_wx[l]).astype(BF16), row(lru_bx[l]), row(lru_lambda[l]))
        a_heads = -jnp.exp(ssd_a_log[l])
        br_b = _ssd(_project(h, wb), ssd_conv_w[l], row(ssd_conv_b[l]),
                    row(_pad_lanes(ssd_dt_bias[l])), row(_pad_lanes(a_heads)),
                    row(jnp.repeat(ssd_d[l], SSD_P)), row(ssd_norm_w[l]))
        gate_bias = _pad_lanes(jnp.concatenate([mlstm_i_bias[l], mlstm_f_bias[l]]))
        br_c = _mlstm(_project(h, wc), mlstm_conv_w[l], row(mlstm_conv_b[l]), row(gate_bias),
                      row(mlstm_norm_w[l]))
        br_d = _hgrn(_project(h, wd), row(lb_all[l]), row(hgrn_norm_w[l]))
        merged = _merge(l, h, (br_a, br_b, br_c, br_d), w_gate, b_gate[l].reshape(4, 1, d), w_up)
        wr = _pad_lanes(jnp.concatenate([router_group_w[l], router_expert_w[l]], axis=1))
        br = row(_pad_lanes(jnp.concatenate([router_group_b[l], router_expert_b[l]])))
        xs, h2p, route, counts = _out_ln_route(merged, w_out[l].astype(BF16), xs, mod_vec(l, 2), row(ln1_w[l]),
                                               row(ln1_b[l]), mod_vec(l, 4), mod_vec(l, 3), wr, br)
        dest, blk_e, n_used, zflag, n_slots = _moe_plan(route, counts)
        xsort = _dispatch(h2p, dest, zflag, n_slots)
        ysort = _experts(l, blk_e, n_used, xsort, exp_w_gate, exp_w_up, exp_w_down)
        nl = min(l + 1, depth - 1)
        xs, h = _combine_ln(dest, ysort, route, xs, mod_vec(l, 5), row(ln2_w[l]), row(ln2_b[l]),
                            mod_vec(nl, 1), mod_vec(nl, 0))
    return xs.reshape(bsz, seq, d)
```

```python
import functools

import jax
import jax.numpy as jnp
from jax import lax
from jax.experimental import pallas as pl
from jax.experimental.pallas import tpu as pltpu

F32 = jnp.float32
BF16 = jnp.bfloat16
I32 = jnp.int32

D_MODEL = 2048
DEPTH = 4
MIX_W = 512
CONV_K = 4
LRU_BLOCKS = 8
LRU_BW = MIX_W // LRU_BLOCKS
LRU_C = 8.0
SSD_HEADS = 8
SSD_P = 64
SSD_G = 2
SSD_HPG = 4
SSD_N = 128
MLSTM_HEADS = 4
MLSTM_DH = 128
HGRN_HEADS = 4
HGRN_DK = 128
N_GROUPS = 4
EXP_PER_GROUP = 8
N_EXPERTS = 32
D_FF = 512
ALPHA = (2.0 * DEPTH) ** 0.25
EPS = 1e-5
NEG_BIG = -1e30

LANES = 128
SUBLANES = 8
VMEM_LIMIT = 56 * 1024 * 1024

SCAN_CHUNK = 128
MIXER_ROWS = 256
HGRN_DIAG = 2
LRU_CHUNK = 256
MOE_BM = 256
ROW_TILE = 512
LN_TILE = 256
DISPATCH_TILE = 512

B_COLS = 3 * MIX_W + LANES
C_COLS = 3 * MIX_W + LANES + MIX_W


def _params(sem, vmem=VMEM_LIMIT):
    return pltpu.CompilerParams(dimension_semantics=sem, vmem_limit_bytes=vmem)


def _dot(a, b):
    return jnp.dot(a, b, preferred_element_type=F32)


def _dot_nt(a, b):
    return lax.dot_general(a, b, (((1,), (1,)), ((), ())), preferred_element_type=F32)


def _sigmoid(x):
    return jax.nn.sigmoid(x)


def _silu(x):
    return x * jax.nn.sigmoid(x)


def _log1p_exp_neg_abs(x):
    return jnp.log(1.0 + jnp.exp(-jnp.abs(x)))


def _softplus(x):
    return jnp.maximum(x, 0.0) + _log1p_exp_neg_abs(x)


def _gelu_tanh(x):
    return 0.5 * x * (1.0 + jnp.tanh(0.7978845608028654 * (x + 0.044715 * (x * x * x))))


def _row_iota(shape):
    return lax.broadcasted_iota(I32, shape, 0)


def _split3(x):
    a = x.astype(BF16)
    r = x - a.astype(F32)
    b = r.astype(BF16)
    return a, b, (r - b.astype(F32)).astype(BF16)


def _cumsum_rows(x):
    n = x.shape[0]
    row = _row_iota(x.shape)
    d = 1
    while d < n:
        x = x + jnp.where(row >= d, pltpu.roll(x, d, 0), 0.0)
        d *= 2
    return x


def _expand_lanes(x, sel):
    a, b, c = _split3(x)
    return _dot(a, sel) + _dot(b, sel) + _dot(c, sel)


def _causal_conv(x, ext_ref, w, b):
    n = x.shape[0]
    ext_ref[SUBLANES:, :] = x
    acc = x * w[CONV_K - 1:CONV_K] + b
    for j in range(1, CONV_K):
        acc = acc + ext_ref[SUBLANES - j:SUBLANES - j + n, :] * w[CONV_K - 1 - j:CONV_K - j]
    ext_ref[0:SUBLANES, :] = x[n - SUBLANES:]
    return acc


def _ada_kernel(c_ref, wa_ref, wb_ref, b_ref, o_ref):
    sc = _silu(c_ref[...])
    half = wa_ref.shape[2]
    for k, w_ref in enumerate((wa_ref, wb_ref)):
        cols = slice(k * half, (k + 1) * half)
        o_ref[0, :, cols] = jnp.dot(sc, w_ref[0], preferred_element_type=F32,
                                    precision=lax.Precision.HIGHEST) + b_ref[0, :, cols]


def _ada_table(c, ada_w, ada_b):
    depth, d, e = ada_w.shape
    tn = 2048
    c8 = jnp.broadcast_to(c.reshape(1, d), (SUBLANES, d))
    out = pl.pallas_call(
        _ada_kernel,
        grid=(depth, e // tn),
        in_specs=[pl.BlockSpec((SUBLANES, d), lambda l, j: (0, 0)),
                  pl.BlockSpec((1, d, tn // 2), lambda l, j: (l, 0, 2 * j)),
                  pl.BlockSpec((1, d, tn // 2), lambda l, j: (l, 0, 2 * j + 1)),
                  pl.BlockSpec((1, 1, tn), lambda l, j: (l, 0, j))],
        out_specs=pl.BlockSpec((1, SUBLANES, tn), lambda l, j: (l, 0, j)),
        out_shape=jax.ShapeDtypeStruct((depth, SUBLANES, e), F32),
        compiler_params=_params(("parallel", "parallel")),
        name="ada_table",
    )(c8, ada_w, ada_w, ada_b.reshape(depth, 1, e))
    return out[:, 0, :]


def _lb_kernel(p_ref, o_ref):
    p = p_ref[...]
    depth = p.shape[0]
    rows = [p[l:l + 1] for l in range(depth)]
    m = rows[0]
    for r in rows[1:]:
        m = jnp.maximum(m, r)
    es = [jnp.exp(r - m) for r in rows]
    tot = es[0]
    for e in es[1:]:
        tot = tot + e
    soft = [e / tot for e in es]
    acc = soft[0]
    o_ref[0:1, :] = acc - soft[0]
    for l in range(1, depth):
        acc = acc + soft[l]
        o_ref[l:l + 1, :] = acc - soft[0]


def _lb_table(p):
    return pl.pallas_call(
        _lb_kernel,
        out_shape=jax.ShapeDtypeStruct(p.shape, F32),
        name="hgrn_lb_table",
    )(p)


def _modulate_kernel(x_ref, sc_ref, sh_ref, o_ref):
    o_ref[...] = (x_ref[...] * (1.0 + sc_ref[...]) + sh_ref[...]).astype(o_ref.dtype)


def _modulate(x, scale, shift):
    t, d = x.shape
    tm = ROW_TILE
    return pl.pallas_call(
        _modulate_kernel,
        grid=(t // tm,),
        in_specs=[pl.BlockSpec((tm, d), lambda i: (i, 0)),
                  pl.BlockSpec((1, d), lambda i: (0, 0)),
                  pl.BlockSpec((1, d), lambda i: (0, 0))],
        out_specs=pl.BlockSpec((tm, d), lambda i: (i, 0)),
        out_shape=jax.ShapeDtypeStruct((t, d), BF16),
        compiler_params=_params(("parallel",)),
        name="modulate",
    )(x, scale, shift)


def _proj_kernel(h_ref, w_ref, o_ref):
    o_ref[...] = _dot(h_ref[...], w_ref[...])


def _project(h, w):
    t, d = h.shape
    n = w.shape[1]
    tm = 2 * ROW_TILE
    return pl.pallas_call(
        _proj_kernel,
        grid=(t // tm,),
        in_specs=[pl.BlockSpec((tm, d), lambda i: (i, 0)),
                  pl.BlockSpec((d, n), lambda i: (0, 0))],
        out_specs=pl.BlockSpec((tm, n), lambda i: (i, 0)),
        out_shape=jax.ShapeDtypeStruct((t, n), F32),
        compiler_params=_params(("parallel",)),
        name="in_proj",
    )(h, w)


def _rglru_kernel(p_ref, cw_ref, cb_ref, wa_ref, ba_ref, wx_ref, bx_ref, lam_ref, o_ref, tail_ref, h_ref):
    @pl.when(pl.program_id(0) == 0)
    def _():
        tail_ref[...] = jnp.zeros_like(tail_ref)
        h_ref[...] = jnp.zeros_like(h_ref)

    p = p_ref[...]
    n = p.shape[0]
    ax = p[:, :MIX_W]
    ag = p[:, MIX_W:]
    xc = _causal_conv(ax, tail_ref, cw_ref[...], cb_ref[...])
    xb = xc.astype(BF16)
    r = _sigmoid(_dot(xb, wa_ref[...]) + ba_ref[...])
    gi = _sigmoid(_dot(xb, wx_ref[...]) + bx_ref[...])
    log_a = (-LRU_C) * r * _softplus(-lam_ref[...])
    a = jnp.exp(log_a)
    u = jnp.sqrt(-jnp.tanh(log_a) * (a * a + 1.0)) * (gi * xc)
    row = _row_iota(a.shape)
    d = 1
    while d < n:
        keep = row >= d
        a_s = pltpu.roll(a, d, 0)
        u_s = pltpu.roll(u, d, 0)
        u = jnp.where(keep, a * u_s + u, u)
        a = jnp.where(keep, a * a_s, a)
        d *= 2
    h = u + a * h_ref[0:1]
    h_ref[...] = jnp.broadcast_to(h[n - 1:n], h_ref.shape)
    o_ref[...] = (h * _gelu_tanh(ag)).astype(o_ref.dtype)


def _rglru(proj, cw, cb, wa, ba, wx, bx, lam):
    t = proj.shape[0]
    n = LRU_CHUNK
    vec = pl.BlockSpec((1, MIX_W), lambda i: (0, 0))
    sq = pl.BlockSpec((MIX_W, MIX_W), lambda i: (0, 0))
    return pl.pallas_call(
        _rglru_kernel,
        grid=(t // n,),
        in_specs=[pl.BlockSpec((n, 2 * MIX_W), lambda i: (i, 0)),
                  pl.BlockSpec((CONV_K, MIX_W), lambda i: (0, 0)), vec, sq, vec, sq, vec, vec],
        out_specs=pl.BlockSpec((n, MIX_W), lambda i: (i, 0)),
        out_shape=jax.ShapeDtypeStruct((t, MIX_W), BF16),
        scratch_shapes=[pltpu.VMEM((SUBLANES + n, MIX_W), F32), pltpu.VMEM((SUBLANES, MIX_W), F32)],
        compiler_params=_params(("arbitrary",)),
        name="rglru",
    )(proj, cw, cb, wa, ba, wx, bx, lam)


def _chunked(chunk_fn, state_refs, p_ref, o_ref, *refs):
    @pl.when(pl.program_id(0) == 0)
    def _():
        for s in state_refs:
            s[...] = jnp.zeros_like(s)

    for sub in range(p_ref.shape[0] // SCAN_CHUNK):
        rows = pl.ds(sub * SCAN_CHUNK, SCAN_CHUNK)
        chunk_fn(p_ref.at[rows], *refs, o_ref.at[rows], *state_refs)


def _ssd_kernel(p_ref, cw_ref, cb_ref, dtbn_ref, an_ref, dx_ref, nw_ref, sel_ref, o_ref, tail_ref, st_ref):
    _chunked(_ssd_chunk, (tail_ref, st_ref), p_ref, o_ref,
             cw_ref, cb_ref, dtbn_ref, an_ref, dx_ref, nw_ref, sel_ref)


def _ssd_chunk(p_ref, cw_ref, cb_ref, dtbn_ref, an_ref, dx_ref, nw_ref, sel_ref, o_ref, tail_ref, st_ref):
    p = p_ref[...]
    n = p.shape[0]
    gw = SSD_HPG * SSD_P
    z = p[:, 0:MIX_W]
    xbc_raw = p[:, MIX_W:3 * MIX_W]
    dtn_raw = p[:, 3 * MIX_W:3 * MIX_W + LANES]
    xbc = _silu(_causal_conv(xbc_raw, tail_ref, cw_ref[...], cb_ref[...]))
    xv = xbc[:, :MIX_W]
    bm = xbc[:, MIX_W:MIX_W + SSD_G * SSD_N]
    cm = xbc[:, MIX_W + SSD_G * SSD_N:]
    dtn = _softplus(dtn_raw + dtbn_ref[...])
    csn = _cumsum_rows(dtn * an_ref[...])
    csn_t = jnp.transpose(csn)
    sel = sel_ref[...]
    csx = _expand_lanes(csn, sel)
    xdt = xv * _expand_lanes(dtn, sel)
    tril = _row_iota((n, n)) >= lax.broadcasted_iota(I32, (n, n), 1)
    lane_head = lax.shift_right_logical(lax.broadcasted_iota(I32, (n, gw), 1), SSD_P.bit_length() - 1)
    ys = []
    for g in range(SSD_G):
        seg = slice(g * gw, (g + 1) * gw)
        cg = cm[:, g * SSD_N:(g + 1) * SSD_N].astype(BF16)
        bg = bm[:, g * SSD_N:(g + 1) * SSD_N]
        scores = _dot_nt(cg, bg.astype(BF16))
        csg = csx[:, seg]
        xdt_g = xdt[:, seg]
        xdt_gb = xdt_g.astype(BF16)
        st = st_ref[g]
        y = _dot(cg, st.astype(BF16)) * jnp.exp(csg)
        for h in range(SSD_HPG):
            hh = g * SSD_HPG + h
            col = csx[:, hh * SSD_P:hh * SSD_P + 1]
            dec = jnp.exp(jnp.where(tril, col - csn_t[hh:hh + 1, :], NEG_BIG))
            y = y + jnp.where(lane_head == h, _dot((scores * dec).astype(BF16), xdt_gb), 0.0)
        last = csg[n - 1:n]
        xw = (xdt_g * jnp.exp(last - csg)).astype(BF16)
        st_ref[g] = jnp.exp(last) * st + _dot(jnp.transpose(bg).astype(BF16), xw)
        ys.append(y)
    y = jnp.concatenate(ys, axis=1) + dx_ref[...] * xv
    yz = y * _silu(z)
    ms = jnp.mean(yz * yz, axis=1, keepdims=True)
    o_ref[...] = (yz * lax.rsqrt(ms + EPS) * nw_ref[...]).astype(o_ref.dtype)


def _ssd(proj, cw, cb, dtb_n, a_n, d_x, nw):
    t = proj.shape[0]
    n = MIXER_ROWS
    cc = 2 * MIX_W
    vec = lambda w: pl.BlockSpec((1, w), lambda i: (0, 0))
    sel = _lane_selector(SSD_HEADS, SSD_P)
    return pl.pallas_call(
        _ssd_kernel,
        grid=(t // n,),
        in_specs=[pl.BlockSpec((n, B_COLS), lambda i: (i, 0)),
                  pl.BlockSpec((CONV_K, cc), lambda i: (0, 0)), vec(cc),
                  vec(LANES), vec(LANES), vec(MIX_W), vec(MIX_W),
                  pl.BlockSpec(sel.shape, lambda i: (0, 0))],
        out_specs=pl.BlockSpec((n, MIX_W), lambda i: (i, 0)),
        out_shape=jax.ShapeDtypeStruct((t, MIX_W), BF16),
        scratch_shapes=[pltpu.VMEM((SUBLANES + SCAN_CHUNK, cc), F32),
                        pltpu.VMEM((SSD_G, SSD_N, SSD_HPG * SSD_P), F32)],
        compiler_params=_params(("arbitrary",)),
        name="ssd",
    )(proj, cw, cb, dtb_n, a_n, d_x, nw, sel)


def _cummax_rows(x):
    n = x.shape[0]
    row = _row_iota(x.shape)
    d = 1
    while d < n:
        x = jnp.maximum(x, jnp.where(row >= d, pltpu.roll(x, d, 0), NEG_BIG))
        d *= 2
    return x


def _lane_selector(heads, width):
    k = lax.broadcasted_iota(I32, (LANES, heads * width), 0)
    j = lax.broadcasted_iota(I32, (LANES, heads * width), 1)
    return (k == j // width).astype(BF16)


def _mlstm_kernel(p_ref, cw_ref, cb_ref, gb_ref, nw_ref, sel_ref, o_ref, tail_ref, c_ref, n_ref, m_ref):
    _chunked(_mlstm_chunk, (tail_ref, c_ref, n_ref, m_ref), p_ref, o_ref,
             cw_ref, cb_ref, gb_ref, nw_ref, sel_ref)


def _mlstm_chunk(p_ref, cw_ref, cb_ref, gb_ref, nw_ref, sel_ref, o_ref, tail_ref, c_ref, n_ref, m_ref):
    p = p_ref[...]
    n = p.shape[0]
    dh = MLSTM_DH
    qk_raw = p[:, :2 * MIX_W]
    qk = _silu(_causal_conv(qk_raw, tail_ref, cw_ref[...], cb_ref[...]))
    v_all = p[:, 2 * MIX_W:3 * MIX_W]
    pre = p[:, 3 * MIX_W:3 * MIX_W + LANES] + gb_ref[...]
    og_all = _sigmoid(p[:, 3 * MIX_W + LANES:])
    nw_all = nw_ref[...]
    fpre = pltpu.roll(pre, LANES - MLSTM_HEADS, 1)
    cs_n = _cumsum_rows(jnp.minimum(fpre, 0.0) - _log1p_exp_neg_abs(fpre))
    r_n = pre - cs_n
    rt = jnp.transpose(r_n)
    sel = sel_ref[...]
    cs_b = _expand_lanes(cs_n, sel)
    ig_b = _expand_lanes(pre, sel)
    cm_b = _expand_lanes(_cummax_rows(r_n), sel)
    tril = _row_iota((n, n)) >= lax.broadcasted_iota(I32, (n, n), 1)
    m_old = [m_ref[h:h + 1, :] for h in range(MLSTM_HEADS)]
    c_old = [c_ref[h] for h in range(MLSTM_HEADS)]
    n_old = [n_ref[h:h + 1, :] for h in range(MLSTM_HEADS)]
    c_new, n_new, m_news, outs = [], [], [], []
    for h in range(MLSTM_HEADS):
        sl = slice(h * dh, (h + 1) * dh)
        q = qk[:, sl] * (dh ** -0.5)
        k = qk[:, MIX_W + h * dh:MIX_W + (h + 1) * dh]
        v = v_all[:, sl]
        qb = q.astype(BF16)
        vb = v.astype(BF16)
        cs = cs_b[:, sl]
        ig = ig_b[:, sl]
        m_prev = m_old[h]
        inter = cs + m_prev
        m_row = jnp.maximum(cs + cm_b[:, sl], inter)
        dexp = jnp.where(tril, (cs - m_row) + rt[h:h + 1, :], NEG_BIG)
        w = jnp.exp(dexp) * _dot_nt(qb, k.astype(BF16))
        w_inter = jnp.exp(inter - m_row)
        cmem = c_old[h]
        nmem = n_old[h]
        num = _dot(w.astype(BF16), vb) + w_inter * _dot(qb, cmem.astype(BF16))
        den = jnp.sum(w, axis=1, keepdims=True) + w_inter * jnp.sum(q * nmem, axis=1, keepdims=True)
        hs = num / jnp.maximum(jnp.abs(den), jnp.exp(-m_row))
        g_tot = cs[n - 1:n]
        s_end = g_tot - cs + ig
        m_new = jnp.maximum(g_tot + m_prev, jnp.max(s_end, axis=0, keepdims=True))
        kw = k * jnp.exp(s_end - m_new)
        decay = jnp.exp(g_tot + m_prev - m_new)
        c_new.append(decay * cmem + _dot(jnp.transpose(kw).astype(BF16), vb))
        n_new.append(decay * nmem + jnp.sum(kw, axis=0, keepdims=True))
        m_news.append(jnp.broadcast_to(m_new, (1, LANES)))
        ms = jnp.mean(hs * hs, axis=1, keepdims=True)
        outs.append((og_all[:, sl] * (hs * lax.rsqrt(ms + EPS) * nw_all[:, sl])).astype(o_ref.dtype))
    for h in range(MLSTM_HEADS):
        c_ref[h] = c_new[h]
        n_ref[h:h + 1, :] = n_new[h]
        m_ref[h:h + 1, :] = m_news[h]
        o_ref[:, h * dh:(h + 1) * dh] = outs[h]


def _mlstm(proj, cw, cb, gate_bias, nw):
    t = proj.shape[0]
    n = SCAN_CHUNK
    assert n == LANES
    cc = 2 * MIX_W
    vec = lambda w: pl.BlockSpec((1, w), lambda i: (0, 0))
    return pl.pallas_call(
        _mlstm_kernel,
        grid=(t // n,),
        in_specs=[pl.BlockSpec((n, C_COLS), lambda i: (i, 0)),
                  pl.BlockSpec((CONV_K, cc), lambda i: (0, 0)), vec(cc), vec(LANES), vec(MIX_W),
                  pl.BlockSpec((LANES, MLSTM_HEADS * LANES), lambda i: (0, 0))],
        out_specs=pl.BlockSpec((n, MIX_W), lambda i: (i, 0)),
        out_shape=jax.ShapeDtypeStruct((t, MIX_W), BF16),
        scratch_shapes=[pltpu.VMEM((SUBLANES + SCAN_CHUNK, cc), F32),
                        pltpu.VMEM((MLSTM_HEADS, MLSTM_DH, MLSTM_DH), F32),
                        pltpu.VMEM((SUBLANES, MLSTM_DH), F32),
                        pltpu.VMEM((SUBLANES, LANES), F32)],
        compiler_params=_params(("arbitrary",)),
        name="mlstm",
    )(proj, cw, cb, gate_bias, nw, _lane_selector(MLSTM_HEADS, LANES))


def _hgrn_kernel(p_ref, lb_ref, nw_ref, o_ref, st_ref):
    _chunked(_hgrn_chunk, (st_ref,), p_ref, o_ref, lb_ref, nw_ref)


def _hgrn_chunk(p_ref, lb_ref, nw_ref, o_ref, st_ref):
    p = p_ref[...]
    n = p.shape[0]
    dk = HGRN_DK
    q = _silu(p[:, :MIX_W])
    u = p[:, MIX_W:2 * MIX_W]
    v = p[:, 2 * MIX_W:3 * MIX_W]
    gt = p[:, 3 * MIX_W:]
    lb = lb_ref[...]
    f = lb + (1.0 - lb) * _sigmoid(u)
    k = (1.0 - lb) * _sigmoid(-u)
    cs = _cumsum_rows(jnp.log(jnp.maximum(f, 1e-30)))
    last = cs[n - 1:n]
    rowi = _row_iota((n, 1))
    vb = v.astype(BF16)

    scores = [None] * HGRN_HEADS
    rr = _row_iota((n, n))
    cc = lax.broadcasted_iota(I32, (n, n), 1)
    hs = n // 2
    while hs >= HGRN_DIAG:
        blk = 2 * hs
        nb = n // blk
        if blk > SUBLANES:
            ref = jnp.concatenate([jnp.broadcast_to(cs[b * blk + hs - 1:b * blk + hs], (blk, MIX_W))
                                   for b in range(nb)], axis=0)
        else:
            c3 = cs.reshape(n // SUBLANES, SUBLANES, MIX_W)
            sub = lax.broadcasted_iota(I32, c3.shape, 1)
            ref = None
            for j in reversed(range(SUBLANES // blk)):
                rj = jnp.broadcast_to(c3[:, j * blk + hs - 1:j * blk + hs, :], c3.shape)
                ref = rj if ref is None else jnp.where(sub < (j + 1) * blk, rj, ref)
            ref = ref.reshape(n, MIX_W)
        upper = (rowi & (blk - 1)) >= hs
        qj = jnp.where(upper, q * jnp.exp(jnp.minimum(cs - ref, 0.0)), 0.0).astype(BF16)
        kj = jnp.where(upper, 0.0, k * jnp.exp(jnp.minimum(ref - cs, 0.0))).astype(BF16)
        same = (rr & -blk) == (cc & -blk)
        for h in range(HGRN_HEADS):
            sl = slice(h * dk, (h + 1) * dk)
            s = _dot_nt(qj[:, sl], kj[:, sl])
            if nb > 1:
                s = jnp.where(same, s, 0.0)
            scores[h] = s if scores[h] is None else scores[h] + s
        hs //= 2

    band = rr - cc
    for dlt in range(HGRN_DIAG):
        if dlt == 0:
            t_all = q * k
        else:
            kd = pltpu.roll(k, dlt, 0)
            csd = pltpu.roll(cs, dlt, 0)
            t_all = q * kd * jnp.exp(jnp.minimum(cs - csd, 0.0))
        valid = (rowi & (HGRN_DIAG - 1)) >= dlt
        for h in range(HGRN_HEADS):
            sl = slice(h * dk, (h + 1) * dk)
            s = jnp.where(valid, jnp.sum(t_all[:, sl], axis=1, keepdims=True), 0.0)
            scores[h] = scores[h] + jnp.where(band == dlt, s, 0.0)

    qe = (q * jnp.exp(cs)).astype(BF16)
    ys = []
    for h in range(HGRN_HEADS):
        sl = slice(h * dk, (h + 1) * dk)
        ys.append(_dot(scores[h].astype(BF16), vb[:, sl]) + _dot_nt(qe[:, sl], st_ref[h].astype(BF16)))

    kw = k * jnp.exp(last - cs)
    elast = jnp.exp(last)
    for h in range(HGRN_HEADS):
        sl = slice(h * dk, (h + 1) * dk)
        st_ref[h] = elast[:, sl] * st_ref[h] + _dot(jnp.transpose(v[:, sl]).astype(BF16), kw[:, sl].astype(BF16))
        y = ys[h]
        ms = jnp.mean(y * y, axis=1, keepdims=True)
        o_ref[:, sl] = (y * lax.rsqrt(ms + EPS) * nw_ref[:, sl] * _silu(gt[:, sl])).astype(o_ref.dtype)


def _hgrn(proj, lb, nw):
    t = proj.shape[0]
    n = MIXER_ROWS
    vec = pl.BlockSpec((1, MIX_W), lambda i: (0, 0))
    return pl.pallas_call(
        _hgrn_kernel,
        grid=(t // n,),
        in_specs=[pl.BlockSpec((n, 4 * MIX_W), lambda i: (i, 0)), vec, vec],
        out_specs=pl.BlockSpec((n, MIX_W), lambda i: (i, 0)),
        out_shape=jax.ShapeDtypeStruct((t, MIX_W), BF16),
        scratch_shapes=[pltpu.VMEM((HGRN_HEADS, HGRN_DK, HGRN_DK), F32)],
        compiler_params=_params(("arbitrary",)),
        name="hgrn2",
    )(proj, lb, nw)


def _merge_kernel(h_ref, ba_ref, bb_ref, bc_ref, bd_ref, wg_ref, bg_ref, wu_ref, o_ref, wg_s, wu_s):
    @pl.when(pl.program_id(1) == 0)
    def _():
        wg_s[...] = wg_ref[0].astype(BF16)
        wu_s[...] = wu_ref[0].astype(BF16)

    h = h_ref[...]
    acc = None
    for br, b_ref in enumerate((ba_ref, bb_ref, bc_ref, bd_ref)):
        gate = _sigmoid(_dot(h, wg_s[br]) + bg_ref[br])
        term = gate * _dot(b_ref[...], wu_s[br])
        acc = term if acc is None else acc + term
    o_ref[...] = acc.astype(o_ref.dtype)


def _merge(layer, h, branches, wg, bg, wu):
    t, d = h.shape
    tm, tn = 1024, 256
    nb = len(branches)
    bspec = pl.BlockSpec((tm, MIX_W), lambda j, i: (i, 0))
    return pl.pallas_call(
        _merge_kernel,
        grid=(d // tn, t // tm),
        in_specs=[pl.BlockSpec((tm, d), lambda j, i: (i, 0)), bspec, bspec, bspec, bspec,
                  pl.BlockSpec((1, nb, d, tn), lambda j, i: (layer, 0, 0, j)),
                  pl.BlockSpec((nb, 1, tn), lambda j, i: (0, 0, j)),
                  pl.BlockSpec((1, nb, MIX_W, tn), lambda j, i: (layer, 0, 0, j))],
        out_specs=pl.BlockSpec((tm, tn), lambda j, i: (i, j)),
        out_shape=jax.ShapeDtypeStruct((t, d), BF16),
        scratch_shapes=[pltpu.VMEM((nb, d, tn), BF16), pltpu.VMEM((nb, MIX_W, tn), BF16)],
        compiler_params=_params(("parallel", "arbitrary")),
        name="merge",
    )(h, *branches, wg, bg, wu)


def _layer_norm_rows(z, w, b):
    mu = jnp.mean(z, axis=1, keepdims=True)
    zc = z - mu
    var = jnp.mean(zc * zc, axis=1, keepdims=True)
    return zc * lax.rsqrt(var + EPS) * w + b


def _pack_bf16_pairs(h):
    half = h.shape[1] // 2
    lo = lax.bitcast_convert_type(h[:, :half].astype(BF16).astype(F32), jnp.uint32)
    hi = lax.bitcast_convert_type(h[:, half:].astype(BF16).astype(F32), jnp.uint32)
    return lax.shift_right_logical(lo, jnp.uint32(16)) | (hi & jnp.uint32(0xFFFF0000))


def _unpack_bf16_pairs(w):
    lo = lax.bitcast_convert_type(lax.shift_left(w, jnp.uint32(16)), F32)
    hi = lax.bitcast_convert_type(w & jnp.uint32(0xFFFF0000), F32)
    return jnp.concatenate([lo, hi], axis=1).astype(BF16)


def _out_ln_route_kernel(m_ref, w_ref, x_ref, g_ref, lw_ref, lb_ref, sc_ref, sh_ref, rw_ref, rb_ref,
                         x_out, hp_out, r_out, cnt_out, run_ref):
    @pl.when(pl.program_id(0) == 0)
    def _():
        run_ref[...] = jnp.zeros_like(run_ref)

    y = _dot(m_ref[...], w_ref[...])
    xn = _layer_norm_rows(ALPHA * x_ref[...] + g_ref[...] * y, lw_ref[...], lb_ref[...])
    x_out[...] = xn
    h = xn * (1.0 + sc_ref[...]) + sh_ref[...]
    hp_out[...] = _pack_bf16_pairs(h)
    route, new1, new2 = _route_rows(h, rw_ref, rb_ref, run_ref[...])
    r_out[...] = route
    run_ref[0:1] = new1
    run_ref[1:2] = new2
    cnt_out[...] = run_ref[...]


def _out_ln_route(merged, w_out, x, gate, ln_w, ln_b, scale_next, shift_next, rw, rb):
    t, d = x.shape
    tm = LN_TILE
    vec = pl.BlockSpec((1, d), lambda i: (0, 0))
    row = pl.BlockSpec((tm, d), lambda i: (i, 0))
    rw_hi = rw.astype(BF16)
    return pl.pallas_call(
        _out_ln_route_kernel,
        grid=(t // tm,),
        in_specs=[row, pl.BlockSpec((d, d), lambda i: (0, 0)), row, vec, vec, vec, vec, vec,
                  pl.BlockSpec((d, 2 * LANES), lambda i: (0, 0)),
                  pl.BlockSpec((1, LANES), lambda i: (0, 0))],
        out_specs=[row, pl.BlockSpec((tm, d // 2), lambda i: (i, 0)),
                   pl.BlockSpec((tm, LANES), lambda i: (i, 0)),
                   pl.BlockSpec((SUBLANES, LANES), lambda i: (0, 0))],
        out_shape=[jax.ShapeDtypeStruct((t, d), F32), jax.ShapeDtypeStruct((t, d // 2), jnp.uint32),
                   jax.ShapeDtypeStruct((t, LANES), F32), jax.ShapeDtypeStruct((SUBLANES, LANES), F32)],
        scratch_shapes=[pltpu.VMEM((SUBLANES, LANES), F32)],
        compiler_params=_params(("arbitrary",)),
        name="out_proj_ln_route",
    )(merged, w_out, x, gate, ln_w, ln_b, scale_next, shift_next,
      jnp.concatenate([rw_hi, (rw - rw_hi.astype(F32)).astype(BF16)], axis=1), rb)


def _route_rows(h, w_ref, b_ref, run):
    n = h.shape[0]
    h_hi = h.astype(BF16)
    h_lo = (h - h_hi.astype(F32)).astype(BF16)
    both = _dot(h_hi, w_ref[...])
    logits = both[:, :LANES] + both[:, LANES:] + _dot(h_lo, w_ref[:, :LANES]) + b_ref[...]
    lane = lax.broadcasted_iota(I32, logits.shape, 1)
    lane_f = lane.astype(F32)
    big = float(LANES)
    gl = jnp.where(lane < N_GROUPS, logits, NEG_BIG)
    ge = jnp.exp(gl - jnp.max(gl, axis=1, keepdims=True))
    gp = ge / jnp.sum(ge, axis=1, keepdims=True)
    p_grp = jnp.max(gp, axis=1, keepdims=True)
    grp = jnp.min(jnp.where(gp == p_grp, lane_f, big), axis=1, keepdims=True)
    lo = N_GROUPS + EXP_PER_GROUP * grp
    el = jnp.where((lane_f >= lo) & (lane_f < lo + EXP_PER_GROUP), logits, NEG_BIG)
    v1 = jnp.max(el, axis=1, keepdims=True)
    i1 = jnp.min(jnp.where(el == v1, lane_f, big), axis=1, keepdims=True)
    el2 = jnp.where(lane_f == i1, NEG_BIG, el)
    v2 = jnp.max(el2, axis=1, keepdims=True)
    i2 = jnp.min(jnp.where(el2 == v2, lane_f, big), axis=1, keepdims=True)
    e2 = jnp.exp(v2 - v1)
    w1 = (1.0 / (1.0 + e2)) * p_grp
    w2 = (e2 / (1.0 + e2)) * p_grp
    oh1 = jnp.where(lane_f == i1, 1.0, 0.0)
    oh2 = jnp.where(lane_f == i2, 1.0, 0.0)
    strict = (_row_iota((n, n)) > lax.broadcasted_iota(I32, (n, n), 1)).astype(BF16)
    c1 = _dot(strict, oh1.astype(BF16)) + run[0:1]
    c2 = _dot(strict, oh2.astype(BF16)) + run[1:2]
    r1 = jnp.sum(oh1 * c1, axis=1, keepdims=True)
    r2 = jnp.sum(oh2 * c2, axis=1, keepdims=True)
    new1 = run[0:1] + jnp.sum(oh1, axis=0, keepdims=True)
    new2 = run[1:2] + jnp.sum(oh2, axis=0, keepdims=True)
    out = jnp.where(lane == 0, i1 - N_GROUPS, 0.0)
    out = jnp.where(lane == 1, i2 - N_GROUPS, out)
    out = jnp.where(lane == 2, w1, out)
    out = jnp.where(lane == 3, w2, out)
    out = jnp.where(lane == 4, r1, out)
    out = jnp.where(lane == 5, r2, out)
    return out, new1, new2


def _dispatch_kernel(zflag_ref, dest_ref, h_ref, xs_ref, zbuf, sem, zsem):
    nt = h_ref.shape[0]
    bm = zbuf.shape[0]

    @pl.when(pl.program_id(0) == 0)
    def _():
        zbuf[...] = jnp.zeros_like(zbuf)

        def zblock(b):
            return pltpu.make_async_copy(zbuf, xs_ref.at[pl.ds(pl.multiple_of(b * bm, bm), bm)], zsem)

        def start_block(b, carry):
            @pl.when(zflag_ref[b] != 0)
            def _():
                zblock(b).start()
            return carry

        def wait_block(b, carry):
            @pl.when(zflag_ref[b] != 0)
            def _():
                zblock(b).wait()
            return carry

        n_blocks = xs_ref.shape[0] // bm
        lax.fori_loop(0, n_blocks, start_block, 0)
        lax.fori_loop(0, n_blocks, wait_block, 0)

    def copy(r, j):
        return pltpu.make_async_copy(h_ref.at[pl.ds(r, 1)],
                                     xs_ref.at[pl.ds(dest_ref[0, 0, 2 * r + j], 1)], sem)

    def issue(r, carry):
        copy(r, 0).start()
        copy(r, 1).start()
        return carry

    def drain(r, carry):
        copy(r, 0).wait()
        copy(r, 1).wait()
        return carry

    lax.fori_loop(0, nt, issue, 0, unroll=8)
    lax.fori_loop(0, nt, drain, 0, unroll=8)


def _dispatch(h, dest, zflag, n_slots):
    t, w = h.shape
    nt = DISPATCH_TILE
    grid_spec = pltpu.PrefetchScalarGridSpec(
        num_scalar_prefetch=1,
        grid=(t // nt,),
        in_specs=[pl.BlockSpec((1, 1, 2 * nt), lambda i, zf: (i, 0, 0), memory_space=pltpu.SMEM),
                  pl.BlockSpec((nt, w), lambda i, zf: (i, 0))],
        out_specs=pl.BlockSpec(memory_space=pl.ANY),
        scratch_shapes=[pltpu.VMEM((MOE_BM, w), h.dtype), pltpu.SemaphoreType.DMA(()),
                        pltpu.SemaphoreType.DMA(())],
    )
    return pl.pallas_call(
        _dispatch_kernel,
        grid_spec=grid_spec,
        out_shape=jax.ShapeDtypeStruct((n_slots, w), h.dtype),
        compiler_params=pltpu.CompilerParams(dimension_semantics=("arbitrary",)),
        name="moe_dispatch",
    )(zflag, dest.reshape(t // nt, 1, 2 * nt), h)


def _expert_kernel(layer, be_ref, nu_ref, first_ref, gidx_ref, nxt_ref, xs_ref, wg_hbm, wu_hbm, wd_hbm, ys_ref,
                   wg_f, wu_f, wd_f, wg_s, wu_s, wd_s, sem):
    b = pl.program_id(0)
    used = b < nu_ref[0]

    def fetch(e, slot):
        return [pltpu.make_async_copy(src.at[layer, e], dst.at[slot], sem.at[slot, k])
                for k, (src, dst) in enumerate(((wg_hbm, wg_f), (wu_hbm, wu_f), (wd_hbm, wd_f)))]

    @pl.when(b == 0)
    def _():
        for c in fetch(be_ref[0], 0):
            c.start()

    @pl.when(used & (first_ref[b] != 0))
    def _():
        slot = lax.rem(gidx_ref[b], 2)
        for c in fetch(be_ref[b], slot):
            c.wait()
        wg_s[...] = wg_f[slot].astype(BF16)
        wu_s[...] = wu_f[slot].astype(BF16)
        wd_s[...] = wd_f[slot].astype(BF16)

        @pl.when(nxt_ref[b] >= 0)
        def _():
            for c in fetch(nxt_ref[b], 1 - slot):
                c.start()

    @pl.when(used)
    def _():
        rows = _unpack_bf16_pairs(xs_ref[...])
        hid = _silu(_dot(rows, wg_s[...])) * _dot(rows, wu_s[...])
        ys_ref[...] = _dot(hid.astype(BF16), wd_s[...])

    @pl.when(jnp.logical_not(used))
    def _():
        ys_ref[...] = jnp.zeros_like(ys_ref)


def _experts(layer, blk_e, n_used, xs, w_g, w_u, w_d):
    n_slots = xs.shape[0]
    d, f = w_g.shape[2], w_g.shape[3]
    bm = MOE_BM
    n_blocks = n_slots // bm
    blk = jnp.arange(n_blocks, dtype=I32)
    used = blk < n_used[0]
    first = (used & ((blk == 0) | (blk_e != jnp.concatenate([blk_e[:1], blk_e[:-1]])))).astype(I32)
    gidx = jnp.cumsum(first) - 1
    nxt_pos = lax.cummin(jnp.where(first != 0, blk, n_blocks), reverse=True)
    nxt_pos = jnp.concatenate([nxt_pos[1:], jnp.full((1,), n_blocks, I32)])
    nxt = jnp.where(nxt_pos < n_blocks, blk_e[jnp.minimum(nxt_pos, n_blocks - 1)], -1).astype(I32)
    grid_spec = pltpu.PrefetchScalarGridSpec(
        num_scalar_prefetch=5,
        grid=(n_blocks,),
        in_specs=[pl.BlockSpec((bm, d // 2), lambda b, *_: (b, 0)),
                  pl.BlockSpec(memory_space=pl.ANY), pl.BlockSpec(memory_space=pl.ANY),
                  pl.BlockSpec(memory_space=pl.ANY)],
        out_specs=pl.BlockSpec((bm, d), lambda b, *_: (b, 0)),
        scratch_shapes=[pltpu.VMEM((2, d, f), F32), pltpu.VMEM((2, d, f), F32), pltpu.VMEM((2, f, d), F32),
                        pltpu.VMEM((d, f), BF16), pltpu.VMEM((d, f), BF16), pltpu.VMEM((f, d), BF16),
                        pltpu.SemaphoreType.DMA((2, 3))],
    )
    return pl.pallas_call(
        functools.partial(_expert_kernel, layer),
        grid_spec=grid_spec,
        out_shape=jax.ShapeDtypeStruct((n_slots, d), F32),
        compiler_params=_params(("arbitrary",)),
        name="moe_experts",
    )(blk_e, n_used, first, gidx, nxt, xs, w_g, w_u, w_d)


def _combine_ln_kernel(dest_ref, dnext_ref, ys_ref, r_ref, x_ref, g_ref, lw_ref, lb_ref, sc_ref, sh_ref,
                       x_out, h_out, buf, sem):
    nt = buf.shape[2]
    i = pl.program_id(0)
    slot = lax.rem(i, 2)

    def copy(d_ref, s, r, j):
        return pltpu.make_async_copy(ys_ref.at[pl.ds(d_ref[0, 0, 2 * r + j], 1)],
                                     buf.at[s, j, pl.ds(r, 1)], sem.at[s])

    def gather(d_ref, s):
        def issue(r, carry):
            copy(d_ref, s, r, 0).start()
            copy(d_ref, s, r, 1).start()
            return carry
        lax.fori_loop(0, nt, issue, 0, unroll=8)

    @pl.when(i == 0)
    def _():
        gather(dest_ref, 0)

    @pl.when(i + 1 < pl.num_programs(0))
    def _():
        gather(dnext_ref, 1 - slot)

    def drain(r, carry):
        copy(dest_ref, slot, r, 0).wait()
        copy(dest_ref, slot, r, 1).wait()
        return carry

    lax.fori_loop(0, nt, drain, 0, unroll=8)
    rt = r_ref[...]
    y = buf[slot, 0] * rt[:, 2:3] + buf[slot, 1] * rt[:, 3:4]
    xn = _layer_norm_rows(ALPHA * x_ref[...] + g_ref[...] * y, lw_ref[...], lb_ref[...])
    x_out[...] = xn
    h_out[...] = (xn * (1.0 + sc_ref[...]) + sh_ref[...]).astype(h_out.dtype)


def _combine_ln(dest, ys, route, x, gate, ln_w, ln_b, scale_next, shift_next):
    t, d = x.shape
    tm = LN_TILE
    vec = pl.BlockSpec((1, d), lambda i: (0, 0))
    row = pl.BlockSpec((tm, d), lambda i: (i, 0))
    nsteps = t // tm
    dest3 = dest.reshape(nsteps, 1, 2 * tm)
    return pl.pallas_call(
        _combine_ln_kernel,
        grid=(nsteps,),
        in_specs=[pl.BlockSpec((1, 1, 2 * tm), lambda i: (i, 0, 0), memory_space=pltpu.SMEM),
                  pl.BlockSpec((1, 1, 2 * tm), lambda i: (jnp.minimum(i + 1, nsteps - 1), 0, 0),
                               memory_space=pltpu.SMEM),
                  pl.BlockSpec(memory_space=pl.ANY),
                  pl.BlockSpec((tm, LANES), lambda i: (i, 0)),
                  row, vec, vec, vec, vec, vec],
        out_specs=[row, row],
        out_shape=[jax.ShapeDtypeStruct((t, d), F32), jax.ShapeDtypeStruct((t, d), BF16)],
        scratch_shapes=[pltpu.VMEM((2, 2, tm, d), F32), pltpu.SemaphoreType.DMA((2,))],
        compiler_params=_params(("arbitrary",)),
        name="moe_combine_ln",
    )(dest3, dest3, ys, route, x, gate, ln_w, ln_b, scale_next, shift_next)


def _slot_kernel(r_ref, tab_ref, o_ref):
    rt = r_ref[...]
    lane = lax.broadcasted_iota(I32, rt.shape, 1)
    lane_f = lane.astype(F32)
    tab = tab_ref[...]
    d1 = jnp.sum(jnp.where(lane_f == rt[:, 0:1] + N_GROUPS, tab[0:1], 0.0), axis=1, keepdims=True) + rt[:, 4:5]
    d2 = jnp.sum(jnp.where(lane_f == rt[:, 1:2] + N_GROUPS, tab[1:2], 0.0), axis=1, keepdims=True) + rt[:, 5:6]
    o_ref[...] = jnp.where(lane == 0, d1, jnp.where(lane == 1, d2, 0.0)).astype(I32)


def _moe_plan(route, counts):
    bm = MOE_BM
    t = route.shape[0]
    tm = 1024
    n_blocks = (2 * t) // bm + N_EXPERTS
    c1 = counts[0].astype(I32)
    c2 = counts[1].astype(I32)
    pcounts = (c1 + c2 + bm - 1) // bm * bm
    pends = jnp.cumsum(pcounts)
    pstart = pends - pcounts
    tab = jnp.zeros((SUBLANES, LANES), F32).at[0].set(pstart.astype(F32)).at[1].set((pstart + c1).astype(F32))
    slots = pl.pallas_call(
        _slot_kernel,
        grid=(t // tm,),
        in_specs=[pl.BlockSpec((tm, LANES), lambda i: (i, 0)),
                  pl.BlockSpec((SUBLANES, LANES), lambda i: (0, 0))],
        out_specs=pl.BlockSpec((tm, LANES), lambda i: (i, 0)),
        out_shape=jax.ShapeDtypeStruct((t, LANES), I32),
        compiler_params=_params(("parallel",)),
        name="moe_slots",
    )(route, tab)
    dest = slots[:, :2]
    ends = pends[N_GROUPS:N_GROUPS + N_EXPERTS]
    starts_of_blocks = jnp.arange(n_blocks, dtype=I32) * bm
    blk_e = jnp.minimum(jnp.sum((ends[None, :] <= starts_of_blocks[:, None]).astype(I32), axis=1),
                        N_EXPERTS - 1).astype(I32)
    n_used = (pends[-1] // bm).astype(I32).reshape(1)
    blk = jnp.arange(n_blocks, dtype=I32)
    next_e = jnp.concatenate([blk_e[1:], blk_e[-1:]])
    zflag = ((blk >= n_used[0] - 1) | (next_e != blk_e)).astype(I32)
    return dest, blk_e, n_used, zflag, n_blocks * bm


def _block_diag(w):
    nb, bw, _ = w.shape
    eye = jnp.eye(nb, dtype=w.dtype)
    return (eye[:, None, :, None] * w[:, :, None, :]).reshape(nb * bw, nb * bw)


def _pad_lanes(v, width=LANES):
    return jnp.pad(v, [(0, 0)] * (v.ndim - 1) + [(0, width - v.shape[-1])])


def _split_w_in(w):
    o = 0
    parts = []
    for s in (MIX_W, MIX_W, MIX_W, MIX_W, SSD_G * SSD_N, SSD_G * SSD_N, SSD_HEADS, MIX_W, MIX_W, MIX_W,
              MLSTM_HEADS, MLSTM_HEADS, MIX_W, MIX_W, MIX_W, MIX_W, MIX_W):
        parts.append(w[:, o:o + s])
        o += s
    a_x, a_g, b_z, b_x, b_b, b_c, b_dt, c_q, c_k, c_v, c_i, c_f, c_o, d_q, d_f, d_i, d_g = parts
    wa = jnp.concatenate([a_x, a_g], axis=1)
    wb = jnp.concatenate([b_z, b_x, b_b, b_c, _pad_lanes(b_dt)], axis=1)
    wc = jnp.concatenate([c_q, c_k, c_v, _pad_lanes(jnp.concatenate([c_i, c_f], axis=1)), c_o], axis=1)
    wd = jnp.concatenate([d_q, d_f, d_i, d_g], axis=1)
    return [m.astype(BF16) for m in (wa, wb, wc, wd)]


def kernel(x, c, ada_w, ada_b, w_in, lru_conv_w, lru_conv_b, lru_wa, lru_ba, lru_wx, lru_bx, lru_lambda, ssd_conv_w, ssd_conv_b, ssd_dt_bias, ssd_a_log, ssd_d, ssd_norm_w, mlstm_conv_w, mlstm_conv_b, mlstm_i_bias, mlstm_f_bias, mlstm_norm_w, hgrn_lb_param, hgrn_norm_w, w_up, w_gate, b_gate, w_out, ln1_w, ln1_b, router_group_w, router_group_b, router_expert_w, router_expert_b, exp_w_gate, exp_w_up, exp_w_down, ln2_w, ln2_b):
    bsz, seq, d = x.shape
    depth = w_in.shape[0]
    assert bsz == 1 and d == D_MODEL
    xs = x.reshape(seq, d)
    ada = _ada_table(c, ada_w, ada_b)
    lb_all = _lb_table(hgrn_lb_param)
    row = lambda v: v.reshape(1, -1)

    def mod_vec(l, k):
        return ada[l, k * d:(k + 1) * d].reshape(1, d)

    h = _modulate(xs, mod_vec(0, 1), mod_vec(0, 0))
    for l in range(depth):
        wa, wb, wc, wd = _split_w_in(w_in[l])
        br_a = _rglru(_project(h, wa), lru_conv_w[l], row(lru_conv_b[l]),
                      _block_diag(lru_wa[l]).astype(BF16), row(lru_ba[l]),
                      _block_diag(lru_wx[l]).astype(BF16), row(lru_bx[l]), row(lru_lambda[l]))
        a_heads = -jnp.exp(ssd_a_log[l])
        br_b = _ssd(_project(h, wb), ssd_conv_w[l], row(ssd_conv_b[l]),
                    row(_pad_lanes(ssd_dt_bias[l])), row(_pad_lanes(a_heads)),
                    row(jnp.repeat(ssd_d[l], SSD_P)), row(ssd_norm_w[l]))
        gate_bias = _pad_lanes(jnp.concatenate([mlstm_i_bias[l], mlstm_f_bias[l]]))
        br_c = _mlstm(_project(h, wc), mlstm_conv_w[l], row(mlstm_conv_b[l]), row(gate_bias),
                      row(mlstm_norm_w[l]))
        br_d = _hgrn(_project(h, wd), row(lb_all[l]), row(hgrn_norm_w[l]))
        merged = _merge(l, h, (br_a, br_b, br_c, br_d), w_gate, b_gate[l].reshape(4, 1, d), w_up)
        wr = _pad_lanes(jnp.concatenate([router_group_w[l], router_expert_w[l]], axis=1))
        br = row(_pad_lanes(jnp.concatenate([router_group_b[l], router_expert_b[l]])))
        xs, h2p, route, counts = _out_ln_route(merged, w_out[l].astype(BF16), xs, mod_vec(l, 2), row(ln1_w[l]),
                                               row(ln1_b[l]), mod_vec(l, 4), mod_vec(l, 3), wr, br)
        dest, blk_e, n_used, zflag, n_slots = _moe_plan(route, counts)
        xsort = _dispatch(h2p, dest, zflag, n_slots)
        ysort = _experts(l, blk_e, n_used, xsort, exp_w_gate, exp_w_up, exp_w_down)
        nl = min(l + 1, depth - 1)
        xs, h = _combine_ln(dest, ysort, route, xs, mod_vec(l, 5), row(ln2_w[l]), row(ln2_b[l]),
                            mod_vec(nl, 1), mod_vec(nl, 0))
    return xs.reshape(bsz, seq, d)
```

```python
import functools

import jax
import jax.numpy as jnp
from jax import lax
from jax.experimental import pallas as pl
from jax.experimental.pallas import tpu as pltpu

F32 = jnp.float32
BF16 = jnp.bfloat16
I32 = jnp.int32

D_MODEL = 2048
DEPTH = 4
MIX_W = 512
CONV_K = 4
LRU_BLOCKS = 8
LRU_BW = MIX_W // LRU_BLOCKS
LRU_C = 8.0
SSD_HEADS = 8
SSD_P = 64
SSD_G = 2
SSD_HPG = 4
SSD_N = 128
MLSTM_HEADS = 4
MLSTM_DH = 128
HGRN_HEADS = 4
HGRN_DK = 128
N_GROUPS = 4
EXP_PER_GROUP = 8
N_EXPERTS = 32
D_FF = 512
ALPHA = (2.0 * DEPTH) ** 0.25
EPS = 1e-5
NEG_BIG = -1e30

LANES = 128
SUBLANES = 8
VMEM_LIMIT = 56 * 1024 * 1024

SCAN_CHUNK = 128
MIXER_ROWS = 256
HGRN_DIAG = 2
LRU_CHUNK = 256
MOE_BM = 256
ROW_TILE = 512
LN_TILE = 256
DISPATCH_TILE = 512

B_COLS = 3 * MIX_W + LANES
C_COLS = 3 * MIX_W + LANES + MIX_W


def _params(sem, vmem=VMEM_LIMIT):
    return pltpu.CompilerParams(dimension_semantics=sem, vmem_limit_bytes=vmem)


def _dot(a, b):
    return jnp.dot(a, b, preferred_element_type=F32)


def _dot_nt(a, b):
    return lax.dot_general(a, b, (((1,), (1,)), ((), ())), preferred_element_type=F32)


def _sigmoid(x):
    return jax.nn.sigmoid(x)


def _silu(x):
    return x * jax.nn.sigmoid(x)


def _log1p_exp_neg_abs(x):
    return jnp.log(1.0 + jnp.exp(-jnp.abs(x)))


def _softplus(x):
    return jnp.maximum(x, 0.0) + _log1p_exp_neg_abs(x)


def _gelu_tanh(x):
    return 0.5 * x * (1.0 + jnp.tanh(0.7978845608028654 * (x + 0.044715 * (x * x * x))))


def _row_iota(shape):
    return lax.broadcasted_iota(I32, shape, 0)


def _split3(x):
    a = x.astype(BF16)
    r = x - a.astype(F32)
    b = r.astype(BF16)
    return a, b, (r - b.astype(F32)).astype(BF16)


def _cumsum_rows(x):
    n = x.shape[0]
    row = _row_iota(x.shape)
    d = 1
    while d < n:
        x = x + jnp.where(row >= d, pltpu.roll(x, d, 0), 0.0)
        d *= 2
    return x


def _expand_lanes(x, sel):
    a, b, c = _split3(x)
    return _dot(a, sel) + _dot(b, sel) + _dot(c, sel)


def _causal_conv(x, ext_ref, w, b):
    n = x.shape[0]
    ext_ref[SUBLANES:, :] = x
    acc = x * w[CONV_K - 1:CONV_K] + b
    for j in range(1, CONV_K):
        acc = acc + ext_ref[SUBLANES - j:SUBLANES - j + n, :] * w[CONV_K - 1 - j:CONV_K - j]
    ext_ref[0:SUBLANES, :] = x[n - SUBLANES:]
    return acc


def _ada_kernel(c_ref, wa_ref, wb_ref, b_ref, o_ref):
    sc = _silu(c_ref[...])
    half = wa_ref.shape[2]
    for k, w_ref in enumerate((wa_ref, wb_ref)):
        cols = slice(k * half, (k + 1) * half)
        o_ref[0, :, cols] = jnp.dot(sc, w_ref[0], preferred_element_type=F32,
                                    precision=lax.Precision.HIGHEST) + b_ref[0, :, cols]


def _ada_table(c, ada_w, ada_b):
    depth, d, e = ada_w.shape
    tn = 2048
    c8 = jnp.broadcast_to(c.reshape(1, d), (SUBLANES, d))
    out = pl.pallas_call(
        _ada_kernel,
        grid=(depth, e // tn),
        in_specs=[pl.BlockSpec((SUBLANES, d), lambda l, j: (0, 0)),
                  pl.BlockSpec((1, d, tn // 2), lambda l, j: (l, 0, 2 * j)),
                  pl.BlockSpec((1, d, tn // 2), lambda l, j: (l, 0, 2 * j + 1)),
                  pl.BlockSpec((1, 1, tn), lambda l, j: (l, 0, j))],
        out_specs=pl.BlockSpec((1, SUBLANES, tn), lambda l, j: (l, 0, j)),
        out_shape=jax.ShapeDtypeStruct((depth, SUBLANES, e), F32),
        compiler_params=_params(("parallel", "parallel")),
        name="ada_table",
    )(c8, ada_w, ada_w, ada_b.reshape(depth, 1, e))
    return out[:, 0, :]


def _lb_kernel(p_ref, o_ref):
    p = p_ref[...]
    depth = p.shape[0]
    rows = [p[l:l + 1] for l in range(depth)]
    m = rows[0]
    for r in rows[1:]:
        m = jnp.maximum(m, r)
    es = [jnp.exp(r - m) for r in rows]
    tot = es[0]
    for e in es[1:]:
        tot = tot + e
    soft = [e / tot for e in es]
    acc = soft[0]
    o_ref[0:1, :] = acc - soft[0]
    for l in range(1, depth):
        acc = acc + soft[l]
        o_ref[l:l + 1, :] = acc - soft[0]


def _lb_table(p):
    return pl.pallas_call(
        _lb_kernel,
        out_shape=jax.ShapeDtypeStruct(p.shape, F32),
        name="hgrn_lb_table",
    )(p)


def _modulate_kernel(x_ref, sc_ref, sh_ref, o_ref):
    o_ref[...] = (x_ref[...] * (1.0 + sc_ref[...]) + sh_ref[...]).astype(o_ref.dtype)


def _modulate(x, scale, shift):
    t, d = x.shape
    tm = ROW_TILE
    return pl.pallas_call(
        _modulate_kernel,
        grid=(t // tm,),
        in_specs=[pl.BlockSpec((tm, d), lambda i: (i, 0)),
                  pl.BlockSpec((1, d), lambda i: (0, 0)),
                  pl.BlockSpec((1, d), lambda i: (0, 0))],
        out_specs=pl.BlockSpec((tm, d), lambda i: (i, 0)),
        out_shape=jax.ShapeDtypeStruct((t, d), BF16),
        compiler_params=_params(("parallel",)),
        name="modulate",
    )(x, scale, shift)


def _proj_kernel(h_ref, w_ref, o_ref):
    o_ref[...] = _dot(h_ref[...], w_ref[...])


def _project(h, w):
    t, d = h.shape
    n = w.shape[1]
    tm = 2 * ROW_TILE
    return pl.pallas_call(
        _proj_kernel,
        grid=(t // tm,),
        in_specs=[pl.BlockSpec((tm, d), lambda i: (i, 0)),
                  pl.BlockSpec((d, n), lambda i: (0, 0))],
        out_specs=pl.BlockSpec((tm, n), lambda i: (i, 0)),
        out_shape=jax.ShapeDtypeStruct((t, n), F32),
        compiler_params=_params(("parallel",)),
        name="in_proj",
    )(h, w)


def _rglru_kernel(p_ref, cw_ref, cb_ref, wa_ref, ba_ref, wx_ref, bx_ref, lam_ref, o_ref, tail_ref, h_ref):
    @pl.when(pl.program_id(0) == 0)
    def _():
        tail_ref[...] = jnp.zeros_like(tail_ref)
        h_ref[...] = jnp.zeros_like(h_ref)

    p = p_ref[...]
    n = p.shape[0]
    ax = p[:, :MIX_W]
    ag = p[:, MIX_W:]
    xc = _causal_conv(ax, tail_ref, cw_ref[...], cb_ref[...])
    xb = xc.astype(BF16)
    r = _sigmoid(_dot(xb, wa_ref[...]) + ba_ref[...])
    gi = _sigmoid(_dot(xb, wx_ref[...]) + bx_ref[...])
    log_a = (-LRU_C) * r * _softplus(-lam_ref[...])
    a = jnp.exp(log_a)
    u = jnp.sqrt(-jnp.tanh(log_a) * (a * a + 1.0)) * (gi * xc)
    row = _row_iota(a.shape)
    d = 1
    while d < n:
        keep = row >= d
        a_s = pltpu.roll(a, d, 0)
        u_s = pltpu.roll(u, d, 0)
        u = jnp.where(keep, a * u_s + u, u)
        a = jnp.where(keep, a * a_s, a)
        d *= 2
    h = u + a * h_ref[0:1]
    h_ref[...] = jnp.broadcast_to(h[n - 1:n], h_ref.shape)
    o_ref[...] = (h * _gelu_tanh(ag)).astype(o_ref.dtype)


def _rglru(proj, cw, cb, wa, ba, wx, bx, lam):
    t = proj.shape[0]
    n = LRU_CHUNK
    vec = pl.BlockSpec((1, MIX_W), lambda i: (0, 0))
    sq = pl.BlockSpec((MIX_W, MIX_W), lambda i: (0, 0))
    return pl.pallas_call(
        _rglru_kernel,
        grid=(t // n,),
        in_specs=[pl.BlockSpec((n, 2 * MIX_W), lambda i: (i, 0)),
                  pl.BlockSpec((CONV_K, MIX_W), lambda i: (0, 0)), vec, sq, vec, sq, vec, vec],
        out_specs=pl.BlockSpec((n, MIX_W), lambda i: (i, 0)),
        out_shape=jax.ShapeDtypeStruct((t, MIX_W), BF16),
        scratch_shapes=[pltpu.VMEM((SUBLANES + n, MIX_W), F32), pltpu.VMEM((SUBLANES, MIX_W), F32)],
        compiler_params=_params(("arbitrary",)),
        name="rglru",
    )(proj, cw, cb, wa, ba, wx, bx, lam)


def _chunked(chunk_fn, state_refs, p_ref, o_ref, *refs):
    @pl.when(pl.program_id(0) == 0)
    def _():
        for s in state_refs:
            s[...] = jnp.zeros_like(s)

    for sub in range(p_ref.shape[0] // SCAN_CHUNK):
        rows = pl.ds(sub * SCAN_CHUNK, SCAN_CHUNK)
        chunk_fn(p_ref.at[rows], *refs, o_ref.at[rows], *state_refs)


def _ssd_kernel(p_ref, cw_ref, cb_ref, dtbn_ref, an_ref, dx_ref, nw_ref, sel_ref, o_ref, tail_ref, st_ref):
    _chunked(_ssd_chunk, (tail_ref, st_ref), p_ref, o_ref,
             cw_ref, cb_ref, dtbn_ref, an_ref, dx_ref, nw_ref, sel_ref)


def _ssd_chunk(p_ref, cw_ref, cb_ref, dtbn_ref, an_ref, dx_ref, nw_ref, sel_ref, o_ref, tail_ref, st_ref):
    p = p_ref[...]
    n = p.shape[0]
    gw = SSD_HPG * SSD_P
    z = p[:, 0:MIX_W]
    xbc_raw = p[:, MIX_W:3 * MIX_W]
    dtn_raw = p[:, 3 * MIX_W:3 * MIX_W + LANES]
    xbc = _silu(_causal_conv(xbc_raw, tail_ref, cw_ref[...], cb_ref[...]))
    xv = xbc[:, :MIX_W]
    bm = xbc[:, MIX_W:MIX_W + SSD_G * SSD_N]
    cm = xbc[:, MIX_W + SSD_G * SSD_N:]
    dtn = _softplus(dtn_raw + dtbn_ref[...])
    csn = _cumsum_rows(dtn * an_ref[...])
    csn_t = jnp.transpose(csn)
    sel = sel_ref[...]
    csx = _expand_lanes(csn, sel)
    xdt = xv * _expand_lanes(dtn, sel)
    tril = _row_iota((n, n)) >= lax.broadcasted_iota(I32, (n, n), 1)
    lane_head = lax.shift_right_logical(lax.broadcasted_iota(I32, (n, gw), 1), SSD_P.bit_length() - 1)
    ys = []
    for g in range(SSD_G):
        seg = slice(g * gw, (g + 1) * gw)
        cg = cm[:, g * SSD_N:(g + 1) * SSD_N].astype(BF16)
        bg = bm[:, g * SSD_N:(g + 1) * SSD_N]
        scores = _dot_nt(cg, bg.astype(BF16))
        csg = csx[:, seg]
        xdt_g = xdt[:, seg]
        xdt_gb = xdt_g.astype(BF16)
        st = st_ref[g]
        y = _dot(cg, st.astype(BF16)) * jnp.exp(csg)
        for h in range(SSD_HPG):
            hh = g * SSD_HPG + h
            col = csx[:, hh * SSD_P:hh * SSD_P + 1]
            dec = jnp.exp(jnp.where(tril, col - csn_t[hh:hh + 1, :], NEG_BIG))
            y = y + jnp.where(lane_head == h, _dot((scores * dec).astype(BF16), xdt_gb), 0.0)
        last = csg[n - 1:n]
        xw = (xdt_g * jnp.exp(last - csg)).astype(BF16)
        st_ref[g] = jnp.exp(last) * st + _dot(jnp.transpose(bg).astype(BF16), xw)
        ys.append(y)
    y = jnp.concatenate(ys, axis=1) + dx_ref[...] * xv
    yz = y * _silu(z)
    ms = jnp.mean(yz * yz, axis=1, keepdims=True)
    o_ref[...] = (yz * lax.rsqrt(ms + EPS) * nw_ref[...]).astype(o_ref.dtype)


def _ssd(proj, cw, cb, dtb_n, a_n, d_x, nw):
    t = proj.shape[0]
    n = MIXER_ROWS
    cc = 2 * MIX_W
    vec = lambda w: pl.BlockSpec((1, w), lambda i: (0, 0))
    sel = _lane_selector(SSD_HEADS, SSD_P)
    return pl.pallas_call(
        _ssd_kernel,
        grid=(t // n,),
        in_specs=[pl.BlockSpec((n, B_COLS), lambda i: (i, 0)),
                  pl.BlockSpec((CONV_K, cc), lambda i: (0, 0)), vec(cc),
                  vec(LANES), vec(LANES), vec(MIX_W), vec(MIX_W),
                  pl.BlockSpec(sel.shape, lambda i: (0, 0))],
        out_specs=pl.BlockSpec((n, MIX_W), lambda i: (i, 0)),
        out_shape=jax.ShapeDtypeStruct((t, MIX_W), BF16),
        scratch_shapes=[pltpu.VMEM((SUBLANES + SCAN_CHUNK, cc), F32),
                        pltpu.VMEM((SSD_G, SSD_N, SSD_HPG * SSD_P), F32)],
        compiler_params=_params(("arbitrary",)),
        name="ssd",
    )(proj, cw, cb, dtb_n, a_n, d_x, nw, sel)


def _cummax_rows(x):
    n = x.shape[0]
    row = _row_iota(x.shape)
    d = 1
    while d < n:
        x = jnp.maximum(x, jnp.where(row >= d, pltpu.roll(x, d, 0), NEG_BIG))
        d *= 2
    return x


def _lane_selector(heads, width):
    k = lax.broadcasted_iota(I32, (LANES, heads * width), 0)
    j = lax.broadcasted_iota(I32, (LANES, heads * width), 1)
    return (k == j // width).astype(BF16)


def _mlstm_kernel(p_ref, cw_ref, cb_ref, gb_ref, nw_ref, sel_ref, o_ref, tail_ref, c_ref, n_ref, m_ref):
    _chunked(_mlstm_chunk, (tail_ref, c_ref, n_ref, m_ref), p_ref, o_ref,
             cw_ref, cb_ref, gb_ref, nw_ref, sel_ref)


def _mlstm_chunk(p_ref, cw_ref, cb_ref, gb_ref, nw_ref, sel_ref, o_ref, tail_ref, c_ref, n_ref, m_ref):
    p = p_ref[...]
    n = p.shape[0]
    dh = MLSTM_DH
    qk_raw = p[:, :2 * MIX_W]
    qk = _silu(_causal_conv(qk_raw, tail_ref, cw_ref[...], cb_ref[...]))
    v_all = p[:, 2 * MIX_W:3 * MIX_W]
    pre = p[:, 3 * MIX_W:3 * MIX_W + LANES] + gb_ref[...]
    og_all = _sigmoid(p[:, 3 * MIX_W + LANES:])
    nw_all = nw_ref[...]
    fpre = pltpu.roll(pre, LANES - MLSTM_HEADS, 1)
    cs_n = _cumsum_rows(jnp.minimum(fpre, 0.0) - _log1p_exp_neg_abs(fpre))
    r_n = pre - cs_n
    rt = jnp.transpose(r_n)
    sel = sel_ref[...]
    cs_b = _expand_lanes(cs_n, sel)
    ig_b = _expand_lanes(pre, sel)
    cm_b = _expand_lanes(_cummax_rows(r_n), sel)
    tril = _row_iota((n, n)) >= lax.broadcasted_iota(I32, (n, n), 1)
    m_old = [m_ref[h:h + 1, :] for h in range(MLSTM_HEADS)]
    c_old = [c_ref[h] for h in range(MLSTM_HEADS)]
    n_old = [n_ref[h:h + 1, :] for h in range(MLSTM_HEADS)]
    c_new, n_new, m_news, outs = [], [], [], []
    for h in range(MLSTM_HEADS):
        sl = slice(h * dh, (h + 1) * dh)
        q = qk[:, sl] * (dh ** -0.5)
        k = qk[:, MIX_W + h * dh:MIX_W + (h + 1) * dh]
        v = v_all[:, sl]
        qb = q.astype(BF16)
        vb = v.astype(BF16)
        cs = cs_b[:, sl]
        ig = ig_b[:, sl]
        m_prev = m_old[h]
        inter = cs + m_prev
        m_row = jnp.maximum(cs + cm_b[:, sl], inter)
        dexp = jnp.where(tril, (cs - m_row) + rt[h:h + 1, :], NEG_BIG)
        w = jnp.exp(dexp) * _dot_nt(qb, k.astype(BF16))
        w_inter = jnp.exp(inter - m_row)
        cmem = c_old[h]
        nmem = n_old[h]
        num = _dot(w.astype(BF16), vb) + w_inter * _dot(qb, cmem.astype(BF16))
        den = jnp.sum(w, axis=1, keepdims=True) + w_inter * jnp.sum(q * nmem, axis=1, keepdims=True)
        hs = num / jnp.maximum(jnp.abs(den), jnp.exp(-m_row))
        g_tot = cs[n - 1:n]
        s_end = g_tot - cs + ig
        m_new = jnp.maximum(g_tot + m_prev, jnp.max(s_end, axis=0, keepdims=True))
        kw = k * jnp.exp(s_end - m_new)
        decay = jnp.exp(g_tot + m_prev - m_new)
        c_new.append(decay * cmem + _dot(jnp.transpose(kw).astype(BF16), vb))
        n_new.append(decay * nmem + jnp.sum(kw, axis=0, keepdims=True))
        m_news.append(jnp.broadcast_to(m_new, (1, LANES)))
        ms = jnp.mean(hs * hs, axis=1, keepdims=True)
        outs.append((og_all[:, sl] * (hs * lax.rsqrt(ms + EPS) * nw_all[:, sl])).astype(o_ref.dtype))
    for h in range(MLSTM_HEADS):
        c_ref[h] = c_new[h]
        n_ref[h:h + 1, :] = n_new[h]
        m_ref[h:h + 1, :] = m_news[h]
        o_ref[:, h * dh:(h + 1) * dh] = outs[h]


def _mlstm(proj, cw, cb, gate_bias, nw):
    t = proj.shape[0]
    n = SCAN_CHUNK
    assert n == LANES
    cc = 2 * MIX_W
    vec = lambda w: pl.BlockSpec((1, w), lambda i: (0, 0))
    return pl.pallas_call(
        _mlstm_kernel,
        grid=(t // n,),
        in_specs=[pl.BlockSpec((n, C_COLS), lambda i: (i, 0)),
                  pl.BlockSpec((CONV_K, cc), lambda i: (0, 0)), vec(cc), vec(LANES), vec(MIX_W),
                  pl.BlockSpec((LANES, MLSTM_HEADS * LANES), lambda i: (0, 0))],
        out_specs=pl.BlockSpec((n, MIX_W), lambda i: (i, 0)),
        out_shape=jax.ShapeDtypeStruct((t, MIX_W), BF16),
        scratch_shapes=[pltpu.VMEM((SUBLANES + SCAN_CHUNK, cc), F32),
                        pltpu.VMEM((MLSTM_HEADS, MLSTM_DH, MLSTM_DH), F32),
                        pltpu.VMEM((SUBLANES, MLSTM_DH), F32),
                        pltpu.VMEM((SUBLANES, LANES), F32)],
        compiler_params=_params(("arbitrary",)),
        name="mlstm",
    )(proj, cw, cb, gate_bias, nw, _lane_selector(MLSTM_HEADS, LANES))


def _hgrn_kernel(p_ref, lb_ref, nw_ref, o_ref, st_ref):
    _chunked(_hgrn_chunk, (st_ref,), p_ref, o_ref, lb_ref, nw_ref)


def _hgrn_chunk(p_ref, lb_ref, nw_ref, o_ref, st_ref):
    p = p_ref[...]
    n = p.shape[0]
    dk = HGRN_DK
    q = _silu(p[:, :MIX_W])
    u = p[:, MIX_W:2 * MIX_W]
    v = p[:, 2 * MIX_W:3 * MIX_W]
    gt = p[:, 3 * MIX_W:]
    lb = lb_ref[...]
    f = lb + (1.0 - lb) * _sigmoid(u)
    k = (1.0 - lb) * _sigmoid(-u)
    cs = _cumsum_rows(jnp.log(jnp.maximum(f, 1e-30)))
    last = cs[n - 1:n]
    rowi = _row_iota((n, 1))
    vb = v.astype(BF16)

    scores = [None] * HGRN_HEADS
    rr = _row_iota((n, n))
    cc = lax.broadcasted_iota(I32, (n, n), 1)
    hs = n // 2
    while hs >= HGRN_DIAG:
        blk = 2 * hs
        nb = n // blk
        if blk > SUBLANES:
            ref = jnp.concatenate([jnp.broadcast_to(cs[b * blk + hs - 1:b * blk + hs], (blk, MIX_W))
                                   for b in range(nb)], axis=0)
        else:
            c3 = cs.reshape(n // SUBLANES, SUBLANES, MIX_W)
            sub = lax.broadcasted_iota(I32, c3.shape, 1)
            ref = None
            for j in reversed(range(SUBLANES // blk)):
                rj = jnp.broadcast_to(c3[:, j * blk + hs - 1:j * blk + hs, :], c3.shape)
                ref = rj if ref is None else jnp.where(sub < (j + 1) * blk, rj, ref)
            ref = ref.reshape(n, MIX_W)
        upper = (rowi & (blk - 1)) >= hs
        qj = jnp.where(upper, q * jnp.exp(jnp.minimum(cs - ref, 0.0)), 0.0).astype(BF16)
        kj = jnp.where(upper, 0.0, k * jnp.exp(jnp.minimum(ref - cs, 0.0))).astype(BF16)
        same = (rr & -blk) == (cc & -blk)
        for h in range(HGRN_HEADS):
            sl = slice(h * dk, (h + 1) * dk)
            s = _dot_nt(qj[:, sl], kj[:, sl])
            if nb > 1:
                s = jnp.where(same, s, 0.0)
            scores[h] = s if scores[h] is None else scores[h] + s
        hs //= 2

    band = rr - cc
    for dlt in range(HGRN_DIAG):
        if dlt == 0:
            t_all = q * k
        else:
            kd = pltpu.roll(k, dlt, 0)
            csd = pltpu.roll(cs, dlt, 0)
            t_all = q * kd * jnp.exp(jnp.minimum(cs - csd, 0.0))
        valid = (rowi & (HGRN_DIAG - 1)) >= dlt
        for h in range(HGRN_HEADS):
            sl = slice(h * dk, (h + 1) * dk)
            s = jnp.where(valid, jnp.sum(t_all[:, sl], axis=1, keepdims=True), 0.0)
            scores[h] = scores[h] + jnp.where(band == dlt, s, 0.0)

    qe = (q * jnp.exp(cs)).astype(BF16)
    ys = []
    for h in range(HGRN_HEADS):
        sl = slice(h * dk, (h + 1) * dk)
        ys.append(_dot(scores[h].astype(BF16), vb[:, sl]) + _dot_nt(qe[:, sl], st_ref[h].astype(BF16)))

    kw = k * jnp.exp(last - cs)
    elast = jnp.exp(last)
    for h in range(HGRN_HEADS):
        sl = slice(h * dk, (h + 1) * dk)
        st_ref[h] = elast[:, sl] * st_ref[h] + _dot(jnp.transpose(v[:, sl]).astype(BF16), kw[:, sl].astype(BF16))
        y = ys[h]
        ms = jnp.mean(y * y, axis=1, keepdims=True)
        o_ref[:, sl] = (y * lax.rsqrt(ms + EPS) * nw_ref[:, sl] * _silu(gt[:, sl])).astype(o_ref.dtype)


def _hgrn(proj, lb, nw):
    t = proj.shape[0]
    n = MIXER_ROWS
    vec = pl.BlockSpec((1, MIX_W), lambda i: (0, 0))
    return pl.pallas_call(
        _hgrn_kernel,
        grid=(t // n,),
        in_specs=[pl.BlockSpec((n, 4 * MIX_W), lambda i: (i, 0)), vec, vec],
        out_specs=pl.BlockSpec((n, MIX_W), lambda i: (i, 0)),
        out_shape=jax.ShapeDtypeStruct((t, MIX_W), BF16),
        scratch_shapes=[pltpu.VMEM((HGRN_HEADS, HGRN_DK, HGRN_DK), F32)],
        compiler_params=_params(("arbitrary",)),
        name="hgrn2",
    )(proj, lb, nw)


def _merge_kernel(h_ref, ba_ref, bb_ref, bc_ref, bd_ref, wg_ref, bg_ref, wu_ref, o_ref, wg_s, wu_s):
    @pl.when(pl.program_id(1) == 0)
    def _():
        wg_s[...] = wg_ref[0].astype(BF16)
        wu_s[...] = wu_ref[0].astype(BF16)

    h = h_ref[...]
    acc = None
    for br, b_ref in enumerate((ba_ref, bb_ref, bc_ref, bd_ref)):
        gate = _sigmoid(_dot(h, wg_s[br]) + bg_ref[br])
        term = gate * _dot(b_ref[...], wu_s[br])
        acc = term if acc is None else acc + term
    o_ref[...] = acc.astype(o_ref.dtype)


def _merge(layer, h, branches, wg, bg, wu):
    t, d = h.shape
    tm, tn = 1024, 256
    nb = len(branches)
    bspec = pl.BlockSpec((tm, MIX_W), lambda j, i: (i, 0))
    return pl.pallas_call(
        _merge_kernel,
        grid=(d // tn, t // tm),
        in_specs=[pl.BlockSpec((tm, d), lambda j, i: (i, 0)), bspec, bspec, bspec, bspec,
                  pl.BlockSpec((1, nb, d, tn), lambda j, i: (layer, 0, 0, j)),
                  pl.BlockSpec((nb, 1, tn), lambda j, i: (0, 0, j)),
                  pl.BlockSpec((1, nb, MIX_W, tn), lambda j, i: (layer, 0, 0, j))],
        out_specs=pl.BlockSpec((tm, tn), lambda j, i: (i, j)),
        out_shape=jax.ShapeDtypeStruct((t, d), BF16),
        scratch_shapes=[pltpu.VMEM((nb, d, tn), BF16), pltpu.VMEM((nb, MIX_W, tn), BF16)],
        compiler_params=_params(("parallel", "arbitrary")),
        name="merge",
    )(h, *branches, wg, bg, wu)


def _layer_norm_rows(z, w, b):
    mu = jnp.mean(z, axis=1, keepdims=True)
    zc = z - mu
    var = jnp.mean(zc * zc, axis=1, keepdims=True)
    return zc * lax.rsqrt(var + EPS) * w + b


def _pack_bf16_pairs(h):
    half = h.shape[1] // 2
    lo = lax.bitcast_convert_type(h[:, :half].astype(BF16).astype(F32), jnp.uint32)
    hi = lax.bitcast_convert_type(h[:, half:].astype(BF16).astype(F32), jnp.uint32)
    return lax.shift_right_logical(lo, jnp.uint32(16)) | (hi & jnp.uint32(0xFFFF0000))


def _unpack_bf16_pairs(w):
    lo = lax.bitcast_convert_type(lax.shift_left(w, jnp.uint32(16)), F32)
    hi = lax.bitcast_convert_type(w & jnp.uint32(0xFFFF0000), F32)
    return jnp.concatenate([lo, hi], axis=1).astype(BF16)


def _out_ln_route_kernel(m_ref, w_ref, x_ref, g_ref, lw_ref, lb_ref, sc_ref, sh_ref, rw_ref, rb_ref,
                         x_out, hp_out, r_out, cnt_out, run_ref):
    @pl.when(pl.program_id(0) == 0)
    def _():
        run_ref[...] = jnp.zeros_like(run_ref)

    y = _dot(m_ref[...], w_ref[...])
    xn = _layer_norm_rows(ALPHA * x_ref[...] + g_ref[...] * y, lw_ref[...], lb_ref[...])
    x_out[...] = xn
    h = xn * (1.0 + sc_ref[...]) + sh_ref[...]
    hp_out[...] = _pack_bf16_pairs(h)
    route, new1, new2 = _route_rows(h, rw_ref, rb_ref, run_ref[...])
    r_out[...] = route
    run_ref[0:1] = new1
    run_ref[1:2] = new2
    cnt_out[...] = run_ref[...]


def _out_ln_route(merged, w_out, x, gate, ln_w, ln_b, scale_next, shift_next, rw, rb):
    t, d = x.shape
    tm = LN_TILE
    vec = pl.BlockSpec((1, d), lambda i: (0, 0))
    row = pl.BlockSpec((tm, d), lambda i: (i, 0))
    rw_hi = rw.astype(BF16)
    return pl.pallas_call(
        _out_ln_route_kernel,
        grid=(t // tm,),
        in_specs=[row, pl.BlockSpec((d, d), lambda i: (0, 0)), row, vec, vec, vec, vec, vec,
                  pl.BlockSpec((d, 2 * LANES), lambda i: (0, 0)),
                  pl.BlockSpec((1, LANES), lambda i: (0, 0))],
        out_specs=[row, pl.BlockSpec((tm, d // 2), lambda i: (i, 0)),
                   pl.BlockSpec((tm, LANES), lambda i: (i, 0)),
                   pl.BlockSpec((SUBLANES, LANES), lambda i: (0, 0))],
        out_shape=[jax.ShapeDtypeStruct((t, d), F32), jax.ShapeDtypeStruct((t, d // 2), jnp.uint32),
                   jax.ShapeDtypeStruct((t, LANES), F32), jax.ShapeDtypeStruct((SUBLANES, LANES), F32)],
        scratch_shapes=[pltpu.VMEM((SUBLANES, LANES), F32)],
        compiler_params=_params(("arbitrary",)),
        name="out_proj_ln_route",
    )(merged, w_out, x, gate, ln_w, ln_b, scale_next, shift_next,
      jnp.concatenate([rw_hi, (rw - rw_hi.astype(F32)).astype(BF16)], axis=1), rb)


def _route_rows(h, w_ref, b_ref, run):
    n = h.shape[0]
    h_hi = h.astype(BF16)
    h_lo = (h - h_hi.astype(F32)).astype(BF16)
    both = _dot(h_hi, w_ref[...])
    logits = both[:, :LANES] + both[:, LANES:] + _dot(h_lo, w_ref[:, :LANES]) + b_ref[...]
    lane = lax.broadcasted_iota(I32, logits.shape, 1)
    lane_f = lane.astype(F32)
    big = float(LANES)
    gl = jnp.where(lane < N_GROUPS, logits, NEG_BIG)
    ge = jnp.exp(gl - jnp.max(gl, axis=1, keepdims=True))
    gp = ge / jnp.sum(ge, axis=1, keepdims=True)
    p_grp = jnp.max(gp, axis=1, keepdims=True)
    grp = jnp.min(jnp.where(gp == p_grp, lane_f, big), axis=1, keepdims=True)
    lo = N_GROUPS + EXP_PER_GROUP * grp
    el = jnp.where((lane_f >= lo) & (lane_f < lo + EXP_PER_GROUP), logits, NEG_BIG)
    v1 = jnp.max(el, axis=1, keepdims=True)
    i1 = jnp.min(jnp.where(el == v1, lane_f, big), axis=1, keepdims=True)
    el2 = jnp.where(lane_f == i1, NEG_BIG, el)
    v2 = jnp.max(el2, axis=1, keepdims=True)
    i2 = jnp.min(jnp.where(el2 == v2, lane_f, big), axis=1, keepdims=True)
    e2 = jnp.exp(v2 - v1)
    w1 = (1.0 / (1.0 + e2)) * p_grp
    w2 = (e2 / (1.0 + e2)) * p_grp
    oh1 = jnp.where(lane_f == i1, 1.0, 0.0)
    oh2 = jnp.where(lane_f == i2, 1.0, 0.0)
    strict = (_row_iota((n, n)) > lax.broadcasted_iota(I32, (n, n), 1)).astype(BF16)
    c1 = _dot(strict, oh1.astype(BF16)) + run[0:1]
    c2 = _dot(strict, oh2.astype(BF16)) + run[1:2]
    r1 = jnp.sum(oh1 * c1, axis=1, keepdims=True)
    r2 = jnp.sum(oh2 * c2, axis=1, keepdims=True)
    new1 = run[0:1] + jnp.sum(oh1, axis=0, keepdims=True)
    new2 = run[1:2] + jnp.sum(oh2, axis=0, keepdims=True)
    out = jnp.where(lane == 0, i1 - N_GROUPS, 0.0)
    out = jnp.where(lane == 1, i2 - N_GROUPS, out)
    out = jnp.where(lane == 2, w1, out)
    out = jnp.where(lane == 3, w2, out)
    out = jnp.where(lane == 4, r1, out)
    out = jnp.where(lane == 5, r2, out)
    return out, new1, new2


def _dispatch_kernel(zflag_ref, dest_ref, h_ref, xs_ref, zbuf, sem, zsem):
    nt = h_ref.shape[0]
    bm = zbuf.shape[0]

    @pl.when(pl.program_id(0) == 0)
    def _():
        zbuf[...] = jnp.zeros_like(zbuf)

        def zblock(b):
            return pltpu.make_async_copy(zbuf, xs_ref.at[pl.ds(pl.multiple_of(b * bm, bm), bm)], zsem)

        def start_block(b, carry):
            @pl.when(zflag_ref[b] != 0)
            def _():
                zblock(b).start()
            return carry

        def wait_block(b, carry):
            @pl.when(zflag_ref[b] != 0)
            def _():
                zblock(b).wait()
            return carry

        n_blocks = xs_ref.shape[0] // bm
        lax.fori_loop(0, n_blocks, start_block, 0)
        lax.fori_loop(0, n_blocks, wait_block, 0)

    def copy(r, j):
        return pltpu.make_async_copy(h_ref.at[pl.ds(r, 1)],
                                     xs_ref.at[pl.ds(dest_ref[0, 0, 2 * r + j], 1)], sem)

    def issue(r, carry):
        copy(r, 0).start(priority=0)
        copy(r, 1).start(priority=1)
        return carry

    def drain(r, carry):
        copy(r, 0).wait()
        copy(r, 1).wait()
        return carry

    lax.fori_loop(0, nt, issue, 0, unroll=8)
    lax.fori_loop(0, nt, drain, 0, unroll=8)


def _dispatch(h, dest, zflag, n_slots):
    t, w = h.shape
    nt = DISPATCH_TILE
    grid_spec = pltpu.PrefetchScalarGridSpec(
        num_scalar_prefetch=1,
        grid=(t // nt,),
        in_specs=[pl.BlockSpec((1, 1, 2 * nt), lambda i, zf: (i, 0, 0), memory_space=pltpu.SMEM),
                  pl.BlockSpec((nt, w), lambda i, zf: (i, 0))],
        out_specs=pl.BlockSpec(memory_space=pl.ANY),
        scratch_shapes=[pltpu.VMEM((MOE_BM, w), h.dtype), pltpu.SemaphoreType.DMA(()),
                        pltpu.SemaphoreType.DMA(())],
    )
    return pl.pallas_call(
        _dispatch_kernel,
        grid_spec=grid_spec,
        out_shape=jax.ShapeDtypeStruct((n_slots, w), h.dtype),
        compiler_params=pltpu.CompilerParams(dimension_semantics=("arbitrary",)),
        name="moe_dispatch",
    )(zflag, dest.reshape(t // nt, 1, 2 * nt), h)


def _expert_kernel(layer, be_ref, nu_ref, first_ref, gidx_ref, nxt_ref, xs_ref, wg_hbm, wu_hbm, wd_hbm, ys_ref,
                   wg_f, wu_f, wd_f, wg_s, wu_s, wd_s, sem):
    b = pl.program_id(0)
    used = b < nu_ref[0]

    def fetch(e, slot):
        return [pltpu.make_async_copy(src.at[layer, e], dst.at[slot], sem.at[slot, k])
                for k, (src, dst) in enumerate(((wg_hbm, wg_f), (wu_hbm, wu_f), (wd_hbm, wd_f)))]

    @pl.when(b == 0)
    def _():
        for c in fetch(be_ref[0], 0):
            c.start()

    @pl.when(used & (first_ref[b] != 0))
    def _():
        slot = lax.rem(gidx_ref[b], 2)
        for c in fetch(be_ref[b], slot):
            c.wait()
        wg_s[...] = wg_f[slot].astype(BF16)
        wu_s[...] = wu_f[slot].astype(BF16)
        wd_s[...] = wd_f[slot].astype(BF16)

        @pl.when(nxt_ref[b] >= 0)
        def _():
            for c in fetch(nxt_ref[b], 1 - slot):
                c.start()

    @pl.when(used)
    def _():
        rows = _unpack_bf16_pairs(xs_ref[...])
        hid = _silu(_dot(rows, wg_s[...])) * _dot(rows, wu_s[...])
        ys_ref[...] = _dot(hid.astype(BF16), wd_s[...])

    @pl.when(jnp.logical_not(used))
    def _():
        ys_ref[...] = jnp.zeros_like(ys_ref)


def _experts(layer, blk_e, n_used, xs, w_g, w_u, w_d):
    n_slots = xs.shape[0]
    d, f = w_g.shape[2], w_g.shape[3]
    bm = MOE_BM
    n_blocks = n_slots // bm
    blk = jnp.arange(n_blocks, dtype=I32)
    used = blk < n_used[0]
    first = (used & ((blk == 0) | (blk_e != jnp.concatenate([blk_e[:1], blk_e[:-1]])))).astype(I32)
    gidx = jnp.cumsum(first) - 1
    nxt_pos = lax.cummin(jnp.where(first != 0, blk, n_blocks), reverse=True)
    nxt_pos = jnp.concatenate([nxt_pos[1:], jnp.full((1,), n_blocks, I32)])
    nxt = jnp.where(nxt_pos < n_blocks, blk_e[jnp.minimum(nxt_pos, n_blocks - 1)], -1).astype(I32)
    grid_spec = pltpu.PrefetchScalarGridSpec(
        num_scalar_prefetch=5,
        grid=(n_blocks,),
        in_specs=[pl.BlockSpec((bm, d // 2), lambda b, *_: (b, 0)),
                  pl.BlockSpec(memory_space=pl.ANY), pl.BlockSpec(memory_space=pl.ANY),
                  pl.BlockSpec(memory_space=pl.ANY)],
        out_specs=pl.BlockSpec((bm, d), lambda b, *_: (b, 0)),
        scratch_shapes=[pltpu.VMEM((2, d, f), F32), pltpu.VMEM((2, d, f), F32), pltpu.VMEM((2, f, d), F32),
                        pltpu.VMEM((d, f), BF16), pltpu.VMEM((d, f), BF16), pltpu.VMEM((f, d), BF16),
                        pltpu.SemaphoreType.DMA((2, 3))],
    )
    return pl.pallas_call(
        functools.partial(_expert_kernel, layer),
        grid_spec=grid_spec,
        out_shape=jax.ShapeDtypeStruct((n_slots, d), F32),
        compiler_params=_params(("arbitrary",)),
        name="moe_experts",
    )(blk_e, n_used, first, gidx, nxt, xs, w_g, w_u, w_d)


def _combine_ln_kernel(dest_ref, dnext_ref, ys_ref, r_ref, x_ref, g_ref, lw_ref, lb_ref, sc_ref, sh_ref,
                       x_out, h_out, buf, sem):
    nt = buf.shape[2]
    i = pl.program_id(0)
    slot = lax.rem(i, 2)

    def copy(d_ref, s, r, j):
        return pltpu.make_async_copy(ys_ref.at[pl.ds(d_ref[0, 0, 2 * r + j], 1)],
                                     buf.at[s, j, pl.ds(r, 1)], sem.at[s])

    def gather(d_ref, s):
        def issue(r, carry):
            copy(d_ref, s, r, 0).start(priority=0)
            copy(d_ref, s, r, 1).start(priority=1)
            return carry
        lax.fori_loop(0, nt, issue, 0, unroll=8)

    @pl.when(i == 0)
    def _():
        gather(dest_ref, 0)

    @pl.when(i + 1 < pl.num_programs(0))
    def _():
        gather(dnext_ref, 1 - slot)

    def drain(r, carry):
        copy(dest_ref, slot, r, 0).wait()
        copy(dest_ref, slot, r, 1).wait()
        return carry

    lax.fori_loop(0, nt, drain, 0, unroll=8)
    rt = r_ref[...]
    y = buf[slot, 0] * rt[:, 2:3] + buf[slot, 1] * rt[:, 3:4]
    xn = _layer_norm_rows(ALPHA * x_ref[...] + g_ref[...] * y, lw_ref[...], lb_ref[...])
    x_out[...] = xn
    h_out[...] = (xn * (1.0 + sc_ref[...]) + sh_ref[...]).astype(h_out.dtype)


def _combine_ln(dest, ys, route, x, gate, ln_w, ln_b, scale_next, shift_next):
    t, d = x.shape
    tm = LN_TILE
    vec = pl.BlockSpec((1, d), lambda i: (0, 0))
    row = pl.BlockSpec((tm, d), lambda i: (i, 0))
    nsteps = t // tm
    dest3 = dest.reshape(nsteps, 1, 2 * tm)
    return pl.pallas_call(
        _combine_ln_kernel,
        grid=(nsteps,),
        in_specs=[pl.BlockSpec((1, 1, 2 * tm), lambda i: (i, 0, 0), memory_space=pltpu.SMEM),
                  pl.BlockSpec((1, 1, 2 * tm), lambda i: (jnp.minimum(i + 1, nsteps - 1), 0, 0),
                               memory_space=pltpu.SMEM),
                  pl.BlockSpec(memory_space=pl.ANY),
                  pl.BlockSpec((tm, LANES), lambda i: (i, 0)),
                  row, vec, vec, vec, vec, vec],
        out_specs=[row, row],
        out_shape=[jax.ShapeDtypeStruct((t, d), F32), jax.ShapeDtypeStruct((t, d), BF16)],
        scratch_shapes=[pltpu.VMEM((2, 2, tm, d), F32), pltpu.SemaphoreType.DMA((2,))],
        compiler_params=_params(("arbitrary",)),
        name="moe_combine_ln",
    )(dest3, dest3, ys, route, x, gate, ln_w, ln_b, scale_next, shift_next)


def _slot_kernel(r_ref, tab_ref, o_ref):
    rt = r_ref[...]
    lane = lax.broadcasted_iota(I32, rt.shape, 1)
    lane_f = lane.astype(F32)
    tab = tab_ref[...]
    d1 = jnp.sum(jnp.where(lane_f == rt[:, 0:1] + N_GROUPS, tab[0:1], 0.0), axis=1, keepdims=True) + rt[:, 4:5]
    d2 = jnp.sum(jnp.where(lane_f == rt[:, 1:2] + N_GROUPS, tab[1:2], 0.0), axis=1, keepdims=True) + rt[:, 5:6]
    o_ref[...] = jnp.where(lane == 0, d1, jnp.where(lane == 1, d2, 0.0)).astype(I32)


def _moe_plan(route, counts):
    bm = MOE_BM
    t = route.shape[0]
    tm = 1024
    n_blocks = (2 * t) // bm + N_EXPERTS
    c1 = counts[0].astype(I32)
    c2 = counts[1].astype(I32)
    pcounts = (c1 + c2 + bm - 1) // bm * bm
    pends = jnp.cumsum(pcounts)
    pstart = pends - pcounts
    tab = jnp.zeros((SUBLANES, LANES), F32).at[0].set(pstart.astype(F32)).at[1].set((pstart + c1).astype(F32))
    slots = pl.pallas_call(
        _slot_kernel,
        grid=(t // tm,),
        in_specs=[pl.BlockSpec((tm, LANES), lambda i: (i, 0)),
                  pl.BlockSpec((SUBLANES, LANES), lambda i: (0, 0))],
        out_specs=pl.BlockSpec((tm, LANES), lambda i: (i, 0)),
        out_shape=jax.ShapeDtypeStruct((t, LANES), I32),
        compiler_params=_params(("parallel",)),
        name="moe_slots",
    )(route, tab)
    dest = slots[:, :2]
    ends = pends[N_GROUPS:N_GROUPS + N_EXPERTS]
    starts_of_blocks = jnp.arange(n_blocks, dtype=I32) * bm
    blk_e = jnp.minimum(jnp.sum((ends[None, :] <= starts_of_blocks[:, None]).astype(I32), axis=1),
                        N_EXPERTS - 1).astype(I32)
    n_used = (pends[-1] // bm).astype(I32).reshape(1)
    blk = jnp.arange(n_blocks, dtype=I32)
    next_e = jnp.concatenate([blk_e[1:], blk_e[-1:]])
    zflag = ((blk >= n_used[0] - 1) | (next_e != blk_e)).astype(I32)
    return dest, blk_e, n_used, zflag, n_blocks * bm


def _block_diag(w):
    nb, bw, _ = w.shape
    eye = jnp.eye(nb, dtype=w.dtype)
    return (eye[:, None, :, None] * w[:, :, None, :]).reshape(nb * bw, nb * bw)


def _pad_lanes(v, width=LANES):
    return jnp.pad(v, [(0, 0)] * (v.ndim - 1) + [(0, width - v.shape[-1])])


def _split_w_in(w):
    o = 0
    parts = []
    for s in (MIX_W, MIX_W, MIX_W, MIX_W, SSD_G * SSD_N, SSD_G * SSD_N, SSD_HEADS, MIX_W, MIX_W, MIX_W,
              MLSTM_HEADS, MLSTM_HEADS, MIX_W, MIX_W, MIX_W, MIX_W, MIX_W):
        parts.append(w[:, o:o + s])
        o += s
    a_x, a_g, b_z, b_x, b_b, b_c, b_dt, c_q, c_k, c_v, c_i, c_f, c_o, d_q, d_f, d_i, d_g = parts
    wa = jnp.concatenate([a_x, a_g], axis=1)
    wb = jnp.concatenate([b_z, b_x, b_b, b_c, _pad_lanes(b_dt)], axis=1)
    wc = jnp.concatenate([c_q, c_k, c_v, _pad_lanes(jnp.concatenate([c_i, c_f], axis=1)), c_o], axis=1)
    wd = jnp.concatenate([d_q, d_f, d_i, d_g], axis=1)
    return [m.astype(BF16) for m in (wa, wb, wc, wd)]


def kernel(x, c, ada_w, ada_b, w_in, lru_conv_w, lru_conv_b, lru_wa, lru_ba, lru_wx, lru_bx, lru_lambda, ssd_conv_w, ssd_conv_b, ssd_dt_bias, ssd_a_log, ssd_d, ssd_norm_w, mlstm_conv_w, mlstm_conv_b, mlstm_i_bias, mlstm_f_bias, mlstm_norm_w, hgrn_lb_param, hgrn_norm_w, w_up, w_gate, b_gate, w_out, ln1_w, ln1_b, router_group_w, router_group_b, router_expert_w, router_expert_b, exp_w_gate, exp_w_up, exp_w_down, ln2_w, ln2_b):
    bsz, seq, d = x.shape
    depth = w_in.shape[0]
    assert bsz == 1 and d == D_MODEL
    xs = x.reshape(seq, d)
    ada = _ada_table(c, ada_w, ada_b)
    lb_all = _lb_table(hgrn_lb_param)
    row = lambda v: v.reshape(1, -1)

    def mod_vec(l, k):
        return ada[l, k * d:(k + 1) * d].reshape(1, d)

    h = _modulate(xs, mod_vec(0, 1), mod_vec(0, 0))
    for l in range(depth):
        wa, wb, wc, wd = _split_w_in(w_in[l])
        br_a = _rglru(_project(h, wa), lru_conv_w[l], row(lru_conv_b[l]),
                      _block_diag(lru_wa[l]).astype(BF16), row(lru_ba[l]),
                      _block_diag(lru_wx[l]).astype(BF16), row(lru_bx[l]), row(lru_lambda[l]))
        a_heads = -jnp.exp(ssd_a_log[l])
        br_b = _ssd(_project(h, wb), ssd_conv_w[l], row(ssd_conv_b[l]),
                    row(_pad_lanes(ssd_dt_bias[l])), row(_pad_lanes(a_heads)),
                    row(jnp.repeat(ssd_d[l], SSD_P)), row(ssd_norm_w[l]))
        gate_bias = _pad_lanes(jnp.concatenate([mlstm_i_bias[l], mlstm_f_bias[l]]))
        br_c = _mlstm(_project(h, wc), mlstm_conv_w[l], row(mlstm_conv_b[l]), row(gate_bias),
                      row(mlstm_norm_w[l]))
        br_d = _hgrn(_project(h, wd), row(lb_all[l]), row(hgrn_norm_w[l]))
        merged = _merge(l, h, (br_a, br_b, br_c, br_d), w_gate, b_gate[l].reshape(4, 1, d), w_up)
        wr = _pad_lanes(jnp.concatenate([router_group_w[l], router_expert_w[l]], axis=1))
        br = row(_pad_lanes(jnp.concatenate([router_group_b[l], router_expert_b[l]])))
        xs, h2p, route, counts = _out_ln_route(merged, w_out[l].astype(BF16), xs, mod_vec(l, 2), row(ln1_w[l]),
                                               row(ln1_b[l]), mod_vec(l, 4), mod_vec(l, 3), wr, br)
        dest, blk_e, n_used, zflag, n_slots = _moe_plan(route, counts)
        xsort = _dispatch(h2p, dest, zflag, n_slots)
        ysort = _experts(l, blk_e, n_used, xsort, exp_w_gate, exp_w_up, exp_w_down)
        nl = min(l + 1, depth - 1)
        xs, h = _combine_ln(dest, ysort, route, xs, mod_vec(l, 5), row(ln2_w[l]), row(ln2_b[l]),
                            mod_vec(nl, 1), mod_vec(nl, 0))
    return xs.reshape(bsz, seq, d)
```
